```python
import jax, jax.numpy as jnp
from jax import lax
import numpy as np

D_MODEL = 2048
BATCH = 4
SEQ = 2048
DEPTH = 1
DEC_BATCH = 32
DEC_SEQ = 1
PAST_LEN = 16384
PAGE_SIZE = 128

D_RWKV = D_MODEL // 2
D_NSA = D_MODEL - D_RWKV
HEAD_DIM_R = 64
H_R = D_RWKV // HEAD_DIM_R
LORA_W = 96
LORA_A = 96
LORA_G = 64
HEAD_DIM = 64
N_HEADS = D_NSA // HEAD_DIM
N_KV = 4
GQA = N_HEADS // N_KV
KV_W = N_KV * HEAD_DIM
L_CMP = 32
D_CMP = 16
L_SLC = 64
N_SELECT = 16
WINDOW = 512
PHI_HIDDEN = 64
Q_BLOCK = 128
SLC_Q_CHUNK = 64
D_FF = ((8 * D_MODEL + 3 * 256 - 1) // (3 * 256)) * 256
R_COLS = 3 * D_RWKV + LORA_W + LORA_A + LORA_G
D_IN = R_COLS + D_NSA + 6 * KV_W + 3 * N_HEADS
RMS_EPS = 1e-6
LNX_EPS = 64e-5
SCALE = HEAD_DIM ** -0.5

kernel_name = 'hymba_rwkv7_nsa_decode_step'


def rms_norm(x, g):
    xf = x.astype(jnp.float32)
    y = xf * lax.rsqrt(jnp.mean(xf * xf, axis=-1, keepdims=True) + RMS_EPS)
    return (y * g.astype(jnp.float32)).astype(x.dtype)


def masked_softmax(s, mask, axis=-1):
    s = jnp.where(mask, s.astype(jnp.float32), -jnp.inf)
    m = jnp.max(s, axis=axis, keepdims=True)
    m = jnp.where(jnp.isfinite(m), m, 0.0)
    e = jnp.where(mask, jnp.exp(s - m), 0.0)
    return e / jnp.maximum(jnp.sum(e, axis=axis, keepdims=True), 1e-30)


def token_shift(p, prev, mu):
    shifted = jnp.concatenate([prev[:, None, :].astype(p.dtype), p[:, :-1]], axis=1)
    return p + (shifted - p) * mu


def rwkv7_mix(xr, wkv0, lw):
    B, T, _ = xr.shape
    f32 = jnp.float32
    o1, o2, o3 = D_RWKV, 2 * D_RWKV, 3 * D_RWKV
    o4, o5 = o3 + LORA_W, o3 + LORA_W + LORA_A
    r, k, v = xr[..., :o1], xr[..., o1:o2], xr[..., o2:o3]
    wl, al, gl = xr[..., o3:o4], xr[..., o4:o5], xr[..., o5:]
    w_log = -jax.nn.softplus(-(lw['w0'] + jnp.tanh(wl) @ lw['w2']).astype(f32)) - 0.5
    decay = jnp.exp(-jnp.exp(w_log))
    a = jax.nn.sigmoid((lw['a0'] + al @ lw['a2']).astype(f32))
    g = jax.nn.sigmoid(gl) @ lw['g2']
    heads = lambda z: z.astype(f32).reshape(B, T, H_R, HEAD_DIM_R)
    kk = heads(k * lw['k_k'])
    kk = kk / jnp.maximum(jnp.sqrt(jnp.sum(kk * kk, axis=-1, keepdims=True)), 1e-12)
    kf = heads(k * (1.0 + (a - 1.0) * lw['k_a']))
    rf, vf, af, wf = heads(r), heads(v), heads(a), heads(decay)
    seq = lambda z: jnp.swapaxes(z, 0, 1)

    def step(S, inp):
        r_t, w_t, k_t, v_t, kk_t, a_t = inp
        sa = jnp.einsum('bhij,bhj->bhi', S, -kk_t)
        S = (S * w_t[:, :, None, :] + sa[..., None] * (kk_t * a_t)[:, :, None, :]
             + v_t[..., None] * k_t[:, :, None, :])
        return S, jnp.einsum('bhij,bhj->bhi', S, r_t)

    S_fin, ys = lax.scan(step, wkv0.astype(f32), tuple(seq(z) for z in (rf, wf, kf, vf, kk, af)))
    y = seq(ys)
    mean = jnp.mean(y, axis=-1, keepdims=True)
    var = jnp.mean(jnp.square(y - mean), axis=-1, keepdims=True)
    yn = ((y - mean) * lax.rsqrt(var + LNX_EPS)).reshape(B, T, D_RWKV) * lw['lnx_w'] + lw['lnx_b']
    bonus = jnp.sum(rf * kf * lw['r_k'].astype(f32).reshape(H_R, HEAD_DIM_R), axis=-1, keepdims=True) * vf
    out = (yn + bonus.reshape(B, T, D_RWKV)) * g
    return out.astype(xr.dtype), S_fin


def nsa_heads(rest):
    B, T = rest.shape[:2]
    q = rest[..., :D_NSA].reshape(B, T, N_KV, GQA, HEAD_DIM)
    kv = rest[..., D_NSA:D_NSA + 6 * KV_W].reshape(B, T, 3, 2, N_KV, HEAD_DIM)
    gates = jax.nn.sigmoid(rest[..., D_NSA + 6 * KV_W:].astype(jnp.float32)).reshape(B, T, 3, N_KV, GQA, 1)
    return q, kv[:, :, 0], kv[:, :, 1], kv[:, :, 2], gates


def pad_rows(z):
    pad = (-z.shape[1]) % L_SLC
    return jnp.pad(z, [(0, 0), (0, pad)] + [(0, 0)] * (z.ndim - 2))


def gather_full(pool, page_table, new_rows):
    B, S = new_rows.shape[:2]
    past = pool[page_table].reshape((B, -1) + pool.shape[2:])
    pad = (-(past.shape[1] + S)) % L_SLC
    zeros = jnp.zeros((B, pad) + new_rows.shape[2:], new_rows.dtype)
    return jnp.concatenate([past.astype(new_rows.dtype), new_rows, zeros], axis=1)


def compress(rows, lw):
    B, N = rows.shape[:2]
    R = L_CMP // D_CMP
    chunks = rows.reshape(B, N // D_CMP, D_CMP, 2, N_KV, HEAD_DIM)
    w1 = lw['phi_w1'].reshape(2, L_CMP, HEAD_DIM, PHI_HIDDEN)
    part = jnp.einsum('bcjehd,esjdf->bcsehf', chunks, w1.reshape(2, R, D_CMP, HEAD_DIM, PHI_HIDDEN))
    n_blk = N // D_CMP - R + 1
    pe_term = jnp.einsum('ejd,ejdf->ef', lw['phi_pe'], w1)[:, None, :]
    h = sum(part[:, s:s + n_blk, s] for s in range(R)) + pe_term
    return jnp.einsum('bcehf,efd->bcehd', jax.nn.silu(h), lw['phi_w2'])


def cmp_attend_select(q, kvc, q_pos, n_blocks):
    kc, vc = kvc[:, :, 0], kvc[:, :, 1]
    NC = kc.shape[1]
    s = jnp.einsum('bthgd,bchd->bthgc', q, kc).astype(jnp.float32) * SCALE
    c_end = jnp.arange(NC) * D_CMP + L_CMP - 1
    mask = (c_end[None, :] <= q_pos[:, None])[None, :, None, None, :]
    p = masked_softmax(s, mask)
    o_cmp = jnp.einsum('bthgc,bchd->bthgd', p.astype(vc.dtype), vc)
    imp = jnp.sum(p, axis=3)
    c_start = jnp.arange(NC) * D_CMP
    s_start = jnp.arange(n_blocks) * L_SLC
    overlap = ((c_start[:, None] < s_start[None, :] + L_SLC)
               & (c_start[:, None] + L_CMP > s_start[None, :])).astype(jnp.float32)
    imp_s = jnp.einsum('btkc,cs->btks', imp, overlap)
    q_blk = (q_pos // L_SLC)[:, None]
    blk = jnp.arange(n_blocks)[None, :]
    forced = ((blk == 0) | (blk == q_blk) | (blk == q_blk - 1))[None, :, None, :]
    causal = (blk <= q_blk)[None, :, None, :]
    score = jnp.where(forced, jnp.inf, jnp.where(causal, imp_s, -jnp.inf))
    top_val, top_idx = lax.top_k(score, min(N_SELECT, n_blocks))
    return o_cmp, top_idx, top_val > -jnp.inf


def slc_attend(q, q_pos, sel_idx, sel_ok, kv_blocks):
    B = q.shape[0]
    bi = jnp.arange(B)[:, None, None, None]
    hi = jnp.arange(N_KV)[None, None, :, None]
    g = kv_blocks[bi, sel_idx, :, :, hi, :]
    s = jnp.einsum('bthgd,bthnld->bthgnl', q, g[..., 0, :]).astype(jnp.float32) * SCALE
    key_pos = sel_idx[..., None] * L_SLC + jnp.arange(L_SLC)
    mask = sel_ok[..., None] & (key_pos <= q_pos[None, :, None, None, None])
    p = masked_softmax(s, mask[:, :, :, None], axis=(-2, -1))
    return jnp.einsum('bthgnl,bthnld->bthgd', p.astype(g.dtype), g[..., 1, :])


def window_prompt(q, win_kv):
    B, T = q.shape[:2]
    nb, nw = T // Q_BLOCK, WINDOW // Q_BLOCK
    padded = jnp.pad(win_kv, ((0, 0), (WINDOW, 0), (0, 0), (0, 0), (0, 0)))
    padded = padded.reshape(B, nb + nw, Q_BLOCK, 2, N_KV, HEAD_DIM)
    band = jnp.concatenate([padded[:, i:i + nb] for i in range(nw + 1)], axis=2)
    qb = q.reshape(B, nb, Q_BLOCK, N_KV, GQA, HEAD_DIM)
    qi = jnp.arange(Q_BLOCK)
    ki = jnp.arange(WINDOW + Q_BLOCK) - WINDOW
    diff = qi[:, None] - ki[None, :]

    def one_block(args):
        q_blk, kv_blk, n = args
        s = jnp.einsum('bqhgd,bkhd->bqhgk', q_blk, kv_blk[:, :, 0]).astype(jnp.float32) * SCALE
        mask = ((n * Q_BLOCK + ki)[None, :] >= 0) & (diff >= 0) & (diff < WINDOW)
        p = masked_softmax(s, mask[None, :, None, None, :])
        return jnp.einsum('bqhgk,bkhd->bqhgd', p.astype(kv_blk.dtype), kv_blk[:, :, 1])

    o = lax.map(one_block, (jnp.swapaxes(qb, 0, 1), jnp.swapaxes(band, 0, 1), jnp.arange(nb)))
    return jnp.swapaxes(o, 0, 1).reshape(q.shape)


def window_sample(q, q_pos, win_all, k_pos):
    s = jnp.einsum('bthgd,bkhd->bthgk', q, win_all[:, :, 0]).astype(jnp.float32) * SCALE
    diff = q_pos[:, None] - k_pos[None, :]
    mask = (diff >= 0) & (diff < WINDOW)
    p = masked_softmax(s, mask[None, :, None, None, :])
    return jnp.einsum('bthgk,bkhd->bthgd', p.astype(win_all.dtype), win_all[:, :, 1])


def nsa_combine(gates, o_cmp, o_slc, o_win):
    o = gates[:, :, 0] * o_cmp + gates[:, :, 1] * o_slc + gates[:, :, 2] * o_win
    B, T = o.shape[:2]
    return o.reshape(B, T, D_NSA).astype(o_cmp.dtype)


def token_mix_prompt(xn, lw):
    B, T, _ = xn.shape
    proj = xn @ lw['w_in']
    xr = token_shift(proj[..., :R_COLS], jnp.zeros((B, R_COLS), proj.dtype), lw['mu'])
    rwkv_out, wkv_fin = rwkv7_mix(xr, jnp.zeros((B, H_R, HEAD_DIM_R, HEAD_DIM_R), jnp.float32), lw)
    q, cmp_kv, slc_kv, win_kv, gates = nsa_heads(proj[..., R_COLS:])
    q_pos = jnp.arange(T)
    cmp_full, slc_full = pad_rows(cmp_kv), pad_rows(slc_kv)
    o_cmp, sel_idx, sel_ok = cmp_attend_select(q, compress(cmp_full, lw), q_pos, slc_full.shape[1] // L_SLC)
    slc_blocks = slc_full.reshape(B, -1, L_SLC, 2, N_KV, HEAD_DIM)
    c = min(SLC_Q_CHUNK, T)
    nc = T // c
    chunk = lambda z: jnp.swapaxes(z.reshape((B, nc, c) + z.shape[2:]), 0, 1)
    o_slc = lax.map(lambda a: slc_attend(a[0], a[1], a[2], a[3], slc_blocks),
                    (chunk(q), q_pos.reshape(nc, c), chunk(sel_idx), chunk(sel_ok)))
    o_slc = jnp.swapaxes(o_slc, 0, 1).reshape(q.shape)
    o_win = window_prompt(q, win_kv)
    nsa_out = nsa_combine(gates, o_cmp, o_slc, o_win)
    mix = jnp.concatenate([rwkv_out, nsa_out.astype(rwkv_out.dtype)], axis=-1) @ lw['w_o']
    new_state = (cmp_kv, slc_kv, win_kv[:, -min(WINDOW, T):], wkv_fin.astype(xn.dtype), xn[:, -1])
    return mix, new_state


def token_mix_sample(xn, cache_cmp, cache_slc, cache_win, wkv, shift, page_table, lw):
    B, S, _ = xn.shape
    proj = xn @ lw['w_in']
    prev = shift.astype(xn.dtype) @ lw['w_in'][:, :R_COLS]
    xr = token_shift(proj[..., :R_COLS], prev, lw['mu'])
    rwkv_out, wkv_fin = rwkv7_mix(xr, wkv, lw)
    q, cmp_kv, slc_kv, win_kv, gates = nsa_heads(proj[..., R_COLS:])
    past_len = page_table.shape[1] * cache_cmp.shape[1]
    q_pos = past_len + jnp.arange(S)
    cmp_full = gather_full(cache_cmp, page_table, cmp_kv)
    slc_full = gather_full(cache_slc, page_table, slc_kv)
    o_cmp, sel_idx, sel_ok = cmp_attend_select(q, compress(cmp_full, lw), q_pos, slc_full.shape[1] // L_SLC)
    o_slc = slc_attend(q, q_pos, sel_idx, sel_ok, slc_full.reshape(B, -1, L_SLC, 2, N_KV, HEAD_DIM))
    win_all = jnp.concatenate([cache_win.astype(win_kv.dtype), win_kv], axis=1)
    n_buf = cache_win.shape[1]
    k_pos = past_len - n_buf + jnp.arange(n_buf + S)
    o_win = window_sample(q, q_pos, win_all, k_pos)
    nsa_out = nsa_combine(gates, o_cmp, o_slc, o_win)
    mix = jnp.concatenate([rwkv_out, nsa_out.astype(rwkv_out.dtype)], axis=-1) @ lw['w_o']
    new_state = (cmp_kv, slc_kv, win_all[:, -min(WINDOW, n_buf + S):], wkv_fin.astype(xn.dtype), xn[:, -1])
    return mix, new_state


def channel_mix(x, lw):
    h = rms_norm(x, lw['g_ffn'])
    return x + (jax.nn.silu(h @ lw['w_gate']) * (h @ lw['w_up'])) @ lw['w_down']


def setup_inputs(seed: int = 0) -> dict:
    key = jax.random.key(seed)
    keys = jax.random.split(key, 32)
    f32 = jnp.float32
    nrm = lambda i, shape, scale: jax.random.normal(keys[i], shape, f32) * scale
    n_pages = PAST_LEN // PAGE_SIZE
    n_pool = (5 * DEC_BATCH * n_pages + 3) // 4
    n_buf = min(WINDOW, PAST_LEN)
    page_table = jax.random.permutation(keys[8], n_pool)[:DEC_BATCH * n_pages]
    page_table = page_table.reshape(DEC_BATCH, n_pages).astype(jnp.int32)
    return {
        'x_prompt': nrm(0, (BATCH, SEQ, D_MODEL), 1.0),
        'x_sample': nrm(1, (DEC_BATCH, DEC_SEQ, D_MODEL), 1.0),
        'cache_cmp_kv': nrm(2, (DEPTH, n_pool, PAGE_SIZE, 2, N_KV, HEAD_DIM), 1.0),
        'cache_slc_kv': nrm(3, (DEPTH, n_pool, PAGE_SIZE, 2, N_KV, HEAD_DIM), 1.0),
        'cache_win_kv': nrm(4, (DEPTH, DEC_BATCH, n_buf, 2, N_KV, HEAD_DIM), 1.0),
        'state_wkv': nrm(5, (DEPTH, DEC_BATCH, H_R, HEAD_DIM_R, HEAD_DIM_R), 0.5),
        'state_shift': nrm(6, (DEPTH, DEC_BATCH, D_MODEL), 1.0),
        'page_table': page_table,
        'g_mix': 1.0 + nrm(9, (DEPTH, D_MODEL), 0.05),
        'w_in': nrm(10, (DEPTH, D_MODEL, D_IN), D_MODEL ** -0.5),
        'mu': jax.random.uniform(keys[11], (DEPTH, R_COLS), f32),
        'w0': nrm(12, (DEPTH, D_RWKV), 0.5),
        'w2': nrm(13, (DEPTH, LORA_W, D_RWKV), LORA_W ** -0.5),
        'a0': nrm(14, (DEPTH, D_RWKV), 0.5),
        'a2': nrm(15, (DEPTH, LORA_A, D_RWKV), LORA_A ** -0.5),
        'g2': nrm(16, (DEPTH, LORA_G, D_RWKV), LORA_G ** -0.5),
        'k_k': 0.85 + nrm(17, (DEPTH, D_RWKV), 0.05),
        'k_a': 1.0 + nrm(18, (DEPTH, D_RWKV), 0.05),
        'r_k': nrm(19, (DEPTH, D_RWKV), 0.1),
        'lnx_w': 1.0 + nrm(20, (DEPTH, D_RWKV), 0.05),
        'lnx_b': nrm(21, (DEPTH, D_RWKV), 0.02),
        'phi_pe': nrm(22, (DEPTH, 2, L_CMP, HEAD_DIM), 0.1),
        'phi_w1': nrm(23, (DEPTH, 2, L_CMP * HEAD_DIM, PHI_HIDDEN), (L_CMP * HEAD_DIM) ** -0.5),
        'phi_w2': nrm(24, (DEPTH, 2, PHI_HIDDEN, HEAD_DIM), PHI_HIDDEN ** -0.5),
        'w_o': nrm(25, (DEPTH, D_MODEL, D_MODEL), D_MODEL ** -0.5),
        'g_ffn': 1.0 + nrm(26, (DEPTH, D_MODEL), 0.05),
        'w_gate': nrm(27, (DEPTH, D_MODEL, D_FF), D_MODEL ** -0.5),
        'w_up': nrm(28, (DEPTH, D_MODEL, D_FF), D_MODEL ** -0.5),
        'w_down': nrm(29, (DEPTH, D_FF, D_MODEL), D_FF ** -0.5),
        'g_final': 1.0 + nrm(30, (D_MODEL,), 0.05),
    }


def reference(x_prompt, x_sample, cache_cmp_kv, cache_slc_kv, cache_win_kv, state_wkv, state_shift, page_table,
              g_mix, w_in, mu, w0, w2, a0, a2, g2, k_k, k_a, r_k, lnx_w, lnx_b, phi_pe, phi_w1, phi_w2,
              w_o, g_ffn, w_gate, w_up, w_down, g_final):
    names = ('g_mix', 'w_in', 'mu', 'w0', 'w2', 'a0', 'a2', 'g2', 'k_k', 'k_a', 'r_k', 'lnx_w', 'lnx_b',
             'phi_pe', 'phi_w1', 'phi_w2', 'w_o', 'g_ffn', 'w_gate', 'w_up', 'w_down')
    params = (g_mix, w_in, mu, w0, w2, a0, a2, g2, k_k, k_a, r_k, lnx_w, lnx_b,
              phi_pe, phi_w1, phi_w2, w_o, g_ffn, w_gate, w_up, w_down)
    xp, xs = x_prompt, x_sample
    st_p, st_s = [], []
    for layer in range(DEPTH):
        lw = {n: p[layer] for n, p in zip(names, params)}
        mix_p, sp = token_mix_prompt(rms_norm(xp, lw['g_mix']), lw)
        xp = channel_mix(xp + mix_p, lw)
        mix_s, ss = token_mix_sample(rms_norm(xs, lw['g_mix']), cache_cmp_kv[layer], cache_slc_kv[layer],
                                     cache_win_kv[layer], state_wkv[layer], state_shift[layer], page_table, lw)
        xs = channel_mix(xs + mix_s, lw)
        st_p.append(sp)
        st_s.append(ss)
    stk = lambda grp, i: jnp.stack([s[i] for s in grp], axis=0)
    y_prompt = rms_norm(xp, g_final)
    y_sample = rms_norm(xs, g_final)
    return (y_prompt, y_sample,
            stk(st_p, 0), stk(st_p, 1), stk(st_p, 2), stk(st_p, 3), stk(st_p, 4),
            stk(st_s, 0), stk(st_s, 1), stk(st_s, 2), stk(st_s, 3), stk(st_s, 4))
```

```python
import functools

import jax
import jax.numpy as jnp
import numpy as np
from jax import lax
from jax.experimental import pallas as pl
from jax.experimental.pallas import tpu as pltpu

D_MODEL = 2048
D_RWKV = 1024
D_NSA = 1024
HEAD_DIM_R = 64
H_R = 16
LORA_W = 96
LORA_A = 96
LORA_G = 64
LORA_ALL = LORA_W + LORA_A + LORA_G
HEAD_DIM = 64
N_HEADS = 16
N_KV = 4
GQA = 4
KV_W = 256
L_CMP = 32
D_CMP = 16
L_SLC = 64
N_SELECT = 16
WINDOW = 512
PHI_HIDDEN = 64
R_COLS = 3 * D_RWKV + LORA_ALL
RMS_EPS = 1e-6
LNX_EPS = 64e-5
SCALE = HEAD_DIM ** -0.5
NEG = -1e30

C_Q, C_R, C_K, C_V = 0, 1024, 2048, 3072
C_CMP, C_SLC, C_WIN = 4096, 4608, 5120
C_LORA, C_GATE = 5632, 5888
D_IN_PAD = 6144

VMEM_LIMIT = 48 * 1024 * 1024
HI = lax.Precision.HIGHEST


def _cp(sem, limit=VMEM_LIMIT):
    return pltpu.CompilerParams(dimension_semantics=sem, vmem_limit_bytes=limit)


def _bdot(a, b):
    return jnp.dot(a.astype(jnp.bfloat16), b.astype(jnp.bfloat16), preferred_element_type=jnp.float32)


def _hdot(a, b):
    return jnp.dot(a, b, precision=HI, preferred_element_type=jnp.float32)


def _dot_nt(a, b, precision=None):
    return lax.dot_general(a, b, (((1,), (1,)), ((), ())), precision=precision,
                           preferred_element_type=jnp.float32)


def _sigmoid(x):
    return 1.0 / (1.0 + jnp.exp(-x))


def _silu(x):
    return x * _sigmoid(x)


def _rmsnorm_rows_kernel(x_ref, g_ref, o_ref):
    x = x_ref[...]
    ms = jnp.mean(x * x, axis=-1, keepdims=True)
    o_ref[...] = x * lax.rsqrt(ms + RMS_EPS) * g_ref[...]


def rmsnorm_rows(x, g):
    m, d = x.shape
    return pl.pallas_call(
        _rmsnorm_rows_kernel,
        out_shape=jax.ShapeDtypeStruct((m, d), jnp.float32),
        name="rmsnorm_rows",
    )(x, g.reshape(1, d))


def _proj_kernel(x_ref, g_ref, w_ref, o_ref, xn_ref, *, apply_norm):
    @pl.when(pl.program_id(1) == 0)
    def _():
        x = x_ref[...]
        if apply_norm:
            ms = jnp.mean(x * x, axis=-1, keepdims=True)
            x = x * lax.rsqrt(ms + RMS_EPS) * g_ref[...]
        xn_ref[...] = x.astype(jnp.bfloat16)

    o_ref[...] = jnp.dot(xn_ref[...], w_ref[...], preferred_element_type=jnp.float32)


def proj_matmul(x, g, w_bf16, *, apply_norm, tm, tn):
    m, d = x.shape
    n = w_bf16.shape[1]
    return pl.pallas_call(
        functools.partial(_proj_kernel, apply_norm=apply_norm),
        grid=(m // tm, n // tn),
        in_specs=[
            pl.BlockSpec((tm, d), lambda i, j: (i, 0)),
            pl.BlockSpec((1, d), lambda i, j: (0, 0)),
            pl.BlockSpec((d, tn), lambda i, j: (0, j)),
        ],
        out_specs=pl.BlockSpec((tm, tn), lambda i, j: (i, j)),
        out_shape=jax.ShapeDtypeStruct((m, n), jnp.float32),
        scratch_shapes=[pltpu.VMEM((tm, d), jnp.bfloat16)],
        compiler_params=_cp(("parallel", "arbitrary")),
        name="proj_matmul",
    )(x, g.reshape(1, d), w_bf16)


def _rwkv_prep_kernel(pr_ref, pk_ref, pv_ref, pl_ref, prev_ref, mu_ref, w0_ref, a0_ref, kk_ref, ka_ref,
                      w2_ref, a2_ref, g2_ref,
                      r_out, w_out, k_out, v_out, kk_out, a_out, g_out, carry_ref, *, row_prev):
    tm = pr_ref.shape[1]
    pr, pk, pv, plo = pr_ref[0], pk_ref[0], pv_ref[0], pl_ref[0]
    if row_prev:
        def shift_mix(p, lo, hi):
            return p + (prev_ref[0, :, lo:hi] - p) * mu_ref[:, lo:hi]
    else:
        @pl.when(pl.program_id(1) == 0)
        def _():
            carry_ref[...] = prev_ref[0]

        row0 = lax.broadcasted_iota(jnp.int32, (tm, 1), 0) == 0

        def shift_mix(p, lo, hi):
            prev = pltpu.roll(p, 1, axis=0)
            prev = jnp.where(row0, carry_ref[:, lo:hi], prev)
            return p + (prev - p) * mu_ref[:, lo:hi]

    xr = shift_mix(pr, 0, 1024)
    xk = shift_mix(pk, 1024, 2048)
    xv = shift_mix(pv, 2048, 3072)
    xl = shift_mix(plo, 3072, 3328)
    if not row_prev:
        last = tm - 1
        carry_ref[:, 0:1024] = pr[last:last + 1]
        carry_ref[:, 1024:2048] = pk[last:last + 1]
        carry_ref[:, 2048:3072] = pv[last:last + 1]
        carry_ref[:, 3072:3328] = plo[last:last + 1]

    wl = _hdot(jnp.tanh(xl), w2_ref[...])
    al = _hdot(xl, a2_ref[...])
    gl = _hdot(_sigmoid(xl), g2_ref[...])
    z = -(w0_ref[...] + wl)
    softplus = jnp.maximum(z, 0.0) + jnp.log(1.0 + jnp.exp(-jnp.abs(z)))
    w_log = -softplus - 0.5
    decay = jnp.exp(-jnp.exp(w_log))
    a = _sigmoid(a0_ref[...] + al)
    r_out[0] = xr
    w_out[0] = decay
    k_out[0] = xk * (1.0 + (a - 1.0) * ka_ref[...])
    v_out[0] = xv
    kk_out[0] = xk * kk_ref[...]
    a_out[0] = a
    g_out[0] = gl


def rwkv_prep(proj, prev, mu_p, w0, a0, k_k, k_a, w2p, a2p, g2p, *, b, t, tm, row_prev=False):
    proj3 = proj.reshape(b, t, D_IN_PAD)
    nt = t // tm
    prev_spec = (pl.BlockSpec((1, tm, R_COLS), lambda bi, ti: (bi, ti, 0)) if row_prev
                 else pl.BlockSpec((1, 1, R_COLS), lambda bi, ti: (bi, 0, 0)))
    row = lambda blk, w: pl.BlockSpec((1, tm, w), lambda bi, ti: (bi, ti, blk))
    vec = lambda w: pl.BlockSpec((1, w), lambda bi, ti: (0, 0))
    mat = pl.BlockSpec((LORA_ALL, D_RWKV), lambda bi, ti: (0, 0))
    out = pl.BlockSpec((1, tm, D_RWKV), lambda bi, ti: (bi, ti, 0))
    shp = jax.ShapeDtypeStruct((b, t, D_RWKV), jnp.float32)
    return pl.pallas_call(
        functools.partial(_rwkv_prep_kernel, row_prev=row_prev),
        grid=(b, nt),
        in_specs=[row(C_R // 1024, 1024), row(C_K // 1024, 1024), row(C_V // 1024, 1024),
                  row(C_LORA // 256, 256),
                  prev_spec,
                  vec(R_COLS), vec(D_RWKV), vec(D_RWKV), vec(D_RWKV), vec(D_RWKV),
                  mat, mat, mat],
        out_specs=[out] * 7,
        out_shape=[shp] * 7,
        scratch_shapes=[pltpu.VMEM((1, R_COLS), jnp.float32)],
        compiler_params=_cp(("parallel", "arbitrary")),
        name="rwkv_prep",
    )(proj3, proj3, proj3, proj3, prev, mu_p.reshape(1, -1), w0.reshape(1, -1), a0.reshape(1, -1),
      k_k.reshape(1, -1), k_a.reshape(1, -1), w2p, a2p, g2p)


def _rwkv_scan_kernel(r_ref, w_ref, k_ref, kk_ref, a_ref, v_ref, s0_ref, rk_ref, lnw_ref, lnb_ref,
                      o_ref, sfin_ref, s_ref, kkn_ref, b_ref, y_ref, *, halved):
    tc = r_ref.shape[1]
    ni = v_ref.shape[2]

    @pl.when(pl.program_id(1) == 0)
    def _():
        s_ref[...] = s0_ref[0]

    kkraw = kk_ref[0]
    nrm = jnp.sqrt(jnp.sum(kkraw * kkraw, axis=1, keepdims=True))
    kkn = kkraw / jnp.maximum(nrm, 1e-12)
    kkn_ref[...] = kkn
    b_ref[...] = kkn * a_ref[0]

    def step(t, carry):
        w_t = w_ref[0, t]
        k_t = k_ref[0, t]
        r_t = r_ref[0, t]
        kk_t = kkn_ref[t]
        b_t = b_ref[t]
        for i in range(ni):
            s = s_ref[i]
            sa = -jnp.sum(s * kk_t, axis=0, keepdims=True)
            v_i = v_ref[0, t, pl.ds(i, 1), :]
            s = s * w_t + sa * b_t + v_i * k_t
            y_ref[t, pl.ds(i, 1), :] = jnp.sum(s * r_t, axis=0, keepdims=True)
            s_ref[i] = s
        return carry

    lax.fori_loop(0, tc, step, 0)

    y = y_ref[...]
    v = v_ref[0]

    def head_sum(z):
        s = jnp.sum(z, axis=1, keepdims=True)
        if halved:
            s = s + pltpu.roll(s, 64, axis=2)
        return s

    mean = head_sum(y) * (1.0 / HEAD_DIM_R)
    d = y - mean
    var = head_sum(d * d) * (1.0 / HEAD_DIM_R)
    yn = d * lax.rsqrt(var + LNX_EPS) * lnw_ref[0] + lnb_ref[0]
    bonus = jnp.sum(r_ref[0] * k_ref[0] * rk_ref[0], axis=1, keepdims=True)
    o_ref[0] = yn + bonus * v

    @pl.when(pl.program_id(1) == pl.num_programs(1) - 1)
    def _():
        sfin_ref[0] = s_ref[...]


def rwkv_scan(r, w, k, kk, a, v, s0, rk, lnw, lnb, *, tc, halved):
    g, t = r.shape[0], r.shape[1]
    ni = v.shape[2]
    col = pl.BlockSpec((1, tc, 64, 128), lambda gi, ti: (gi, ti, 0, 0))
    vspec = pl.BlockSpec((1, tc, ni, 128), lambda gi, ti: (gi, ti, 0, 0))
    sspec = pl.BlockSpec((1, ni, 64, 128), lambda gi, ti: (gi, 0, 0, 0))
    return pl.pallas_call(
        functools.partial(_rwkv_scan_kernel, halved=halved),
        grid=(g, t // tc),
        in_specs=[col, col, col, col, col, vspec, sspec,
                  pl.BlockSpec((1, 64, 128), lambda gi, ti: (gi, 0, 0)),
                  pl.BlockSpec((1, ni, 128), lambda gi, ti: (gi, 0, 0)),
                  pl.BlockSpec((1, ni, 128), lambda gi, ti: (gi, 0, 0))],
        out_specs=[vspec, sspec],
        out_shape=[jax.ShapeDtypeStruct((g, t, ni, 128), jnp.float32),
                   jax.ShapeDtypeStruct((g, ni, 64, 128), jnp.float32)],
        scratch_shapes=[pltpu.VMEM((ni, 64, 128), jnp.float32),
                        pltpu.VMEM((tc, 64, 128), jnp.float32),
                        pltpu.VMEM((tc, 64, 128), jnp.float32),
                        pltpu.VMEM((tc, ni, 128), jnp.float32)],
        compiler_params=_cp(("parallel", "arbitrary")),
        name="rwkv_scan",
    )(r, w, k, kk, a, v, s0, rk, lnw, lnb)


def _to_scan_layout(x, b, t, halved):
    z = x.reshape(b, t, H_R, 64).transpose(1, 3, 0, 2).reshape(t, 64, b * H_R)
    if halved:
        return jnp.concatenate([z, z], axis=-1)[None]
    g = (b * H_R) // 128
    return z.reshape(t, 64, g, 128).transpose(2, 0, 1, 3)


def _v_to_scan_layout(x, b, t, halved):
    if halved:
        return x.reshape(b, t, H_R, 2, 32).transpose(1, 4, 3, 0, 2).reshape(1, t, 32, 128)
    g = (b * H_R) // 128
    z = x.reshape(b, t, H_R, 64).transpose(1, 3, 0, 2).reshape(t, 64, g, 128)
    return z.transpose(2, 0, 1, 3)


def _vecparam_scan_layout(p, b, halved):
    z = jnp.broadcast_to(p.reshape(1, H_R, 64), (b, H_R, 64))
    if halved:
        return z.reshape(b, H_R, 2, 32).transpose(3, 2, 0, 1).reshape(1, 32, 128)
    g = (b * H_R) // 128
    return z.transpose(2, 0, 1).reshape(64, g, 128).transpose(1, 0, 2)


def _keyparam_scan_layout(p, b, halved):
    z = jnp.broadcast_to(p.reshape(1, H_R, 64), (b, H_R, 64)).transpose(2, 0, 1).reshape(64, b * H_R)
    if halved:
        return jnp.concatenate([z, z], axis=-1)[None]
    g = (b * H_R) // 128
    return z.reshape(64, g, 128).transpose(1, 0, 2)


def _from_scan_layout(o, b, t, halved):
    if halved:
        return o.reshape(t, 32, 2, b, H_R).transpose(3, 0, 4, 2, 1).reshape(b, t, D_RWKV)
    g = o.shape[0]
    return o.transpose(1, 2, 0, 3).reshape(t, 64, b, H_R).transpose(2, 0, 3, 1).reshape(b, t, D_RWKV)


def _state_to_scan_layout(s, b, halved):
    if halved:
        return s.reshape(b, H_R, 2, 32, 64).transpose(3, 4, 2, 0, 1).reshape(1, 32, 64, 128)
    g = (b * H_R) // 128
    return s.reshape(g, 128, 64, 64).transpose(0, 2, 3, 1)


def _state_from_scan_layout(s, b, halved):
    if halved:
        return s.reshape(32, 64, 2, b, H_R).transpose(3, 4, 2, 0, 1).reshape(b, H_R, 64, 64)
    return s.transpose(0, 3, 1, 2).reshape(b, H_R, 64, 64)


def rwkv_mix(proj, prev, wts, s0, *, b, t, tm, tc):
    halved = (b * H_R == 64)
    if t == 1:
        outs = rwkv_prep(proj, prev.reshape(1, b, R_COLS), wts["mu_p"], wts["w0"], wts["a0"], wts["k_k"],
                         wts["k_a"], wts["w2p"], wts["a2p"], wts["g2p"], b=1, t=b, tm=b, row_prev=True)
        r, w, k, v, kk, a, g = [z.reshape(b, 1, D_RWKV) for z in outs]
    else:
        r, w, k, v, kk, a, g = rwkv_prep(proj, prev, wts["mu_p"], wts["w0"], wts["a0"], wts["k_k"], wts["k_a"],
                                         wts["w2p"], wts["a2p"], wts["g2p"], b=b, t=t, tm=tm)
    tr = lambda z: _to_scan_layout(z, b, t, halved)
    o, sfin = rwkv_scan(tr(r), tr(w), tr(k), tr(kk), tr(a), _v_to_scan_layout(v, b, t, halved),
                        _state_to_scan_layout(s0, b, halved),
                        _keyparam_scan_layout(wts["r_k"], b, halved),
                        _vecparam_scan_layout(wts["lnx_w"], b, halved),
                        _vecparam_scan_layout(wts["lnx_b"], b, halved),
                        tc=tc, halved=halved)
    return _from_scan_layout(o, b, t, halved), g, _state_from_scan_layout(sfin, b, halved)


def _cmp_part_kernel(*refs, n_in, precise, prefetch):
    refs = refs[prefetch:]
    x_refs, w_ref, o_ref = refs[:n_in], refs[n_in], refs[n_in + 1]
    for e in range(2):
        for hp in range(2):
            acc = None
            for j in range(D_CMP):
                lo = j * 512 + e * 256 + hp * 128
                xs = [xr[0, :, lo:lo + 128] for xr in x_refs]
                x = xs[0] if n_in == 1 else jnp.concatenate(xs, axis=0)
                d = _hdot(x, w_ref[e, j]) if precise else _bdot(x, w_ref[e, j])
                acc = d if acc is None else acc + d
            c0 = e * 256 + hp * 128
            o_ref[0, :, c0:c0 + 128] = acc[:, :128]
            o_ref[0, :, 512 + c0:512 + c0 + 128] = acc[:, 128:]


def cmp_part_prompt(rows, w1x):
    b, nch, _ = rows.shape
    return pl.pallas_call(
        functools.partial(_cmp_part_kernel, n_in=1, precise=True, prefetch=0),
        grid=(b,),
        in_specs=[pl.BlockSpec((1, nch, 8192), lambda bi: (bi, 0, 0)),
                  pl.BlockSpec((2, D_CMP, 128, 256), lambda bi: (0, 0, 0, 0))],
        out_specs=pl.BlockSpec((1, nch, 1024), lambda bi: (bi, 0, 0)),
        out_shape=jax.ShapeDtypeStruct((b, nch, 1024), jnp.float32),
        compiler_params=_cp(("parallel",)),
        name="cmp_part_prompt",
    )(rows, w1x)


def _expand_w1(phi_w1):
    w1 = phi_w1.reshape(2, 2, D_CMP, HEAD_DIM, PHI_HIDDEN)
    eye = jnp.eye(2, dtype=phi_w1.dtype)
    w = jnp.einsum("ab,esjdf->ejadsbf", eye, w1)
    return w.reshape(2, D_CMP, 128, 256)


def _cmp_kv_from_part(part, pe_ref, w1t_ref, w2k_ref, w2vt_ref, kc_ref, vct_ref):
    n = part.shape[0]
    h = part[:, :512] + pltpu.roll(part[:, 512:], n - 1, axis=0)
    for e in range(2):
        pe = _hdot(pe_ref[e], w1t_ref[e])[0:1]
        he = _silu(h[:, e * 256:(e + 1) * 256] + pe)
        for kvh in range(N_KV):
            if e == 0:
                kc_ref[kvh] = _hdot(he, w2k_ref[kvh])
            else:
                vct_ref[kvh] = _dot_nt(w2vt_ref[kvh], he, HI)


def _cmp_sel_kernel(part_ref, pe_ref, w1t_ref, w2k_ref, w2vt_ref, q_ref, ovt_ref, o_ref, sel_ref,
                    kc_ref, vct_ref, *, tq, n_blk):
    ti = pl.program_id(1)

    @pl.when(ti == 0)
    def _():
        _cmp_kv_from_part(part_ref[0], pe_ref, w1t_ref, w2k_ref, w2vt_ref, kc_ref, vct_ref)

    nc = kc_ref.shape[1]
    nbp = -(-n_blk // 8) * 8
    qpos = ti * tq + lax.broadcasted_iota(jnp.int32, (1, tq), 1)
    cidx = lax.broadcasted_iota(jnp.int32, (nc, 1), 0)
    cmask = (cidx * D_CMP + (L_CMP - 1)) <= qpos
    sidx = cidx[:nbp]
    qblk = qpos // L_SLC
    forced = (sidx == 0) | (sidx == qblk) | (sidx == qblk - 1)
    causal = sidx <= qblk
    for kvh in range(N_KV):
        qt = q_ref[0, :, kvh * 256:(kvh + 1) * 256].T
        impt = jnp.zeros((nc, tq), jnp.float32)
        parts = []
        for g in range(GQA):
            st = _hdot(kc_ref[kvh], qt[g * 64:(g + 1) * 64]) * SCALE
            st = jnp.where(cmask, st, NEG)
            m = jnp.max(st, axis=0, keepdims=True)
            ex = jnp.where(cmask, jnp.exp(st - m), 0.0)
            pt = ex / jnp.maximum(jnp.sum(ex, axis=0, keepdims=True), 1e-30)
            impt = impt + pt
            parts.append(_bdot(vct_ref[kvh], pt))
        o_ref[0, :, kvh * 256:(kvh + 1) * 256] = jnp.concatenate(parts, axis=0).T
        imps = _hdot(ovt_ref[...], impt)[:nbp]
        score = jnp.where(forced, -NEG, jnp.where(causal, imps, NEG))
        rank = jnp.zeros((nbp, tq), jnp.float32)
        for s2 in range(n_blk):
            row = score[s2:s2 + 1]
            beats = (row > score) | ((row == score) & (s2 < sidx))
            rank = rank + beats.astype(jnp.float32)
        seln = ((rank < min(N_SELECT, n_blk)) & causal).astype(jnp.float32)
        sel = jnp.concatenate([seln, jnp.zeros((128 - nbp, tq), jnp.float32)], axis=0)
        sel_ref[0, kvh] = sel.T


def cmp_select_prompt(part, pe8, w1t, w2k, w2vt, proj3, ovt, *, tq):
    b, nc, _ = part.shape
    t = proj3.shape[1]
    c3 = lambda bi, ti: (0, 0, 0)
    return pl.pallas_call(
        functools.partial(_cmp_sel_kernel, tq=tq, n_blk=t // L_SLC),
        grid=(b, t // tq),
        in_specs=[pl.BlockSpec((1, nc, 1024), lambda bi, ti: (bi, 0, 0)),
                  pl.BlockSpec((2, 8, 2048), c3),
                  pl.BlockSpec((2, 2048, 256), c3),
                  pl.BlockSpec((N_KV, 256, 64), c3),
                  pl.BlockSpec((N_KV, 64, 256), c3),
                  pl.BlockSpec((1, tq, 1024), lambda bi, ti: (bi, ti, C_Q // 1024)),
                  pl.BlockSpec((128, nc), lambda bi, ti: (0, 0))],
        out_specs=[pl.BlockSpec((1, tq, 1024), lambda bi, ti: (bi, ti, 0)),
                   pl.BlockSpec((1, N_KV, tq, 128), lambda bi, ti: (bi, 0, ti, 0))],
        out_shape=[jax.ShapeDtypeStruct((b, t, 1024), jnp.float32),
                   jax.ShapeDtypeStruct((b, N_KV, t, 128), jnp.float32)],
        scratch_shapes=[pltpu.VMEM((N_KV, nc, 64), jnp.float32),
                        pltpu.VMEM((N_KV, 64, nc), jnp.float32)],
        compiler_params=_cp(("parallel", "arbitrary")),
        name="cmp_select_prompt",
    )(part, pe8, w1t, w2k, w2vt, proj3, ovt)


def _overlap_t(n_blocks_pad, nc, nc_pad):
    c_start = np.arange(nc_pad) * D_CMP
    s_start = np.arange(n_blocks_pad) * L_SLC
    ov = (c_start[None, :] < s_start[:, None] + L_SLC) & (c_start[None, :] + L_CMP > s_start[:, None])
    ov &= (np.arange(nc_pad) < nc)[None, :]
    return jnp.asarray(ov.astype(np.float32))


def _phi_weights(phi_pe, phi_w1, phi_w2):
    w1t = jnp.tile(phi_w1, (1, 1, N_KV))
    pe8 = jnp.broadcast_to(phi_pe.reshape(2, 1, L_CMP * HEAD_DIM), (2, 8, L_CMP * HEAD_DIM))
    z = jnp.zeros((N_KV, N_KV, PHI_HIDDEN, HEAD_DIM), jnp.float32)
    idx = jnp.arange(N_KV)
    w2k = z.at[idx, idx].set(phi_w2[0]).reshape(N_KV, 256, 64)
    w2vt = jnp.transpose(z.at[idx, idx].set(phi_w2[1]), (0, 3, 1, 2)).reshape(N_KV, 64, 256)
    eye = jnp.eye(N_KV, dtype=jnp.float32)
    w2bd = jnp.einsum("ab,efd->eafbd", eye, phi_w2).reshape(2, 256, 256)
    return w1t, pe8, w2k, w2vt, w2bd


def _flash_blocks(q4, k_ref, v_ref, lo, hi, mask_fn, tq):
    def body(kb, carry):
        m, l, acc = carry
        start = pl.multiple_of(kb * 128, 128)
        k = k_ref[0, 0, pl.ds(start, 128), :]
        v = v_ref[0, 0, pl.ds(start, 128), :]
        s = (_dot_nt(q4, k) * SCALE).reshape(GQA, tq, 128)
        kpos = kb * 128 + lax.broadcasted_iota(jnp.int32, (1, 128), 1)
        msk = mask_fn(kb, kpos)[None]
        s = jnp.where(msk, s, NEG)
        m_new = jnp.maximum(m, jnp.max(s, axis=-1, keepdims=True))
        alpha = jnp.exp(m - m_new)
        p = jnp.where(msk, jnp.exp(s - m_new), 0.0)
        l = alpha * l + jnp.sum(p, axis=-1, keepdims=True)
        pv = jnp.dot(p.reshape(GQA * tq, 128).astype(jnp.bfloat16), v, preferred_element_type=jnp.float32)
        acc = alpha * acc + pv.reshape(GQA, tq, 256)
        return m_new, l, acc

    init = (jnp.full((GQA, tq, 1), NEG, jnp.float32), jnp.zeros((GQA, tq, 1), jnp.float32),
            jnp.zeros((GQA, tq, 256), jnp.float32))
    m, l, acc = lax.fori_loop(lo, hi, body, init)
    o = acc / jnp.maximum(l, 1e-30)
    lane_g = lax.broadcasted_iota(jnp.int32, (1, 256), 1) // HEAD_DIM
    out = jnp.zeros((tq, 256), jnp.float32)
    for g in range(GQA):
        out = out + jnp.where(lane_g == g, o[g], 0.0)
    return out


def _slc_win_kernel(q_ref, ks_ref, vs_ref, kw_ref, vw_ref, sel_ref, os_ref, ow_ref, *, tq):
    ti = pl.program_id(2)
    q = q_ref[0]
    lane_g = lax.broadcasted_iota(jnp.int32, (1, 256), 1) // HEAD_DIM
    q4 = jnp.concatenate([jnp.where(lane_g == g, q, 0.0) for g in range(GQA)], axis=0).astype(jnp.bfloat16)
    qpos = ti * tq + lax.broadcasted_iota(jnp.int32, (tq, 1), 0)
    sel = sel_ref[0, 0].astype(jnp.bfloat16)
    srow = lax.broadcasted_iota(jnp.int32, (128, 128), 0)
    khalf = lax.broadcasted_iota(jnp.int32, (128, 128), 1) // L_SLC

    def slc_mask(kb, kpos):
        expand = (srow == 2 * kb + khalf).astype(jnp.bfloat16)
        on = jnp.dot(sel, expand, preferred_element_type=jnp.float32) > 0.5
        return on & (kpos <= qpos)

    def win_mask(kb, kpos):
        diff = qpos - kpos
        return (diff >= 0) & (diff < WINDOW)

    os_ref[0] = _flash_blocks(q4, ks_ref, vs_ref, 0, ti + 1, slc_mask, tq)
    ow_ref[0] = _flash_blocks(q4, kw_ref, vw_ref, jnp.maximum(ti - WINDOW // 128, 0), ti + 1, win_mask, tq)


def slc_win_prompt(proj3, ks, vs, kw, vw, sel, *, tq):
    b, t, _ = proj3.shape
    kv = pl.BlockSpec((1, 1, t, 256), lambda bi, hi, ti: (bi, hi, 0, 0))
    out = pl.BlockSpec((1, tq, 256), lambda bi, hi, ti: (bi, ti, hi))
    shp = jax.ShapeDtypeStruct((b, t, 1024), jnp.float32)
    return pl.pallas_call(
        functools.partial(_slc_win_kernel, tq=tq),
        grid=(b, N_KV, t // tq),
        in_specs=[pl.BlockSpec((1, tq, 256), lambda bi, hi, ti: (bi, ti, hi)),
                  kv, kv, kv, kv,
                  pl.BlockSpec((1, 1, tq, 128), lambda bi, hi, ti: (bi, hi, ti, 0))],
        out_specs=[out, out],
        out_shape=[shp, shp],
        compiler_params=_cp(("parallel", "parallel", "arbitrary")),
        name="slc_win_prompt",
    )(proj3, ks, vs, kw, vw, sel)


def _tile_kv(kv_rows, b, t):
    kv = kv_rows.reshape(b, t, 2, N_KV, HEAD_DIM).transpose(2, 0, 3, 1, 4).astype(jnp.bfloat16)
    kv = jnp.tile(kv, (1, 1, 1, 1, GQA))
    return kv[0], kv[1]


def _out_proj_kernel(ro_ref, rg_ref, oc_ref, os_ref, ow_ref, gt_ref, ex_ref, x_ref, wo_ref, y_ref, a_ref):
    @pl.when(pl.program_id(1) == 0)
    def _():
        gates = _hdot(_sigmoid(gt_ref[...]), ex_ref[...])
        nsa = (gates[:, 0:1024] * oc_ref[...] + gates[:, 1024:2048] * os_ref[...]
               + gates[:, 2048:3072] * ow_ref[...])
        a_ref[:, 0:1024] = (ro_ref[...] * rg_ref[...]).astype(jnp.bfloat16)
        a_ref[:, 1024:2048] = nsa.astype(jnp.bfloat16)

    y_ref[...] = x_ref[...] + jnp.dot(a_ref[...], wo_ref[...], preferred_element_type=jnp.float32)


def out_proj(ro, rg, oc, osl, ow, proj, gate_expand, x, wo_bf16, *, tm, tn):
    m = x.shape[0]
    row = lambda w: pl.BlockSpec((tm, w), lambda i, j: (i, 0))
    return pl.pallas_call(
        _out_proj_kernel,
        grid=(m // tm, D_MODEL // tn),
        in_specs=[row(1024), row(1024), row(1024), row(1024), row(1024),
                  pl.BlockSpec((tm, 256), lambda i, j: (i, C_GATE // 256)),
                  pl.BlockSpec((256, 3072), lambda i, j: (0, 0)),
                  pl.BlockSpec((tm, tn), lambda i, j: (i, j)),
                  pl.BlockSpec((D_MODEL, tn), lambda i, j: (0, j))],
        out_specs=pl.BlockSpec((tm, tn), lambda i, j: (i, j)),
        out_shape=jax.ShapeDtypeStruct((m, D_MODEL), jnp.float32),
        scratch_shapes=[pltpu.VMEM((tm, D_MODEL), jnp.bfloat16)],
        compiler_params=_cp(("parallel", "arbitrary")),
        name="out_proj",
    )(ro, rg, oc, osl, ow, proj, gate_expand, x, wo_bf16)


def _gate_expand_matrix():
    e = np.zeros((256, 3 * D_NSA), np.float32)
    for k in range(3):
        for h in range(N_HEADS):
            e[k * N_HEADS + h, k * D_NSA + h * HEAD_DIM:k * D_NSA + (h + 1) * HEAD_DIM] = 1.0
    return jnp.asarray(e)


def _ffn_kernel(x_ref, gf_ref, wg_ref, wu_ref, wd_ref, gl_ref, y_ref, h_ref, acc_ref):
    f = pl.program_id(1)

    @pl.when(f == 0)
    def _():
        x = x_ref[...]
        ms = jnp.mean(x * x, axis=-1, keepdims=True)
        h_ref[...] = (x * lax.rsqrt(ms + RMS_EPS) * gf_ref[...]).astype(jnp.bfloat16)
        acc_ref[...] = jnp.zeros_like(acc_ref)

    h = h_ref[...]
    gate = jnp.dot(h, wg_ref[...], preferred_element_type=jnp.float32)
    up = jnp.dot(h, wu_ref[...], preferred_element_type=jnp.float32)
    act = (_silu(gate) * up).astype(jnp.bfloat16)
    acc_ref[...] += jnp.dot(act, wd_ref[...], preferred_element_type=jnp.float32)

    @pl.when(f == pl.num_programs(1) - 1)
    def _():
        z = x_ref[...] + acc_ref[...]
        ms = jnp.mean(z * z, axis=-1, keepdims=True)
        y_ref[...] = z * lax.rsqrt(ms + RMS_EPS) * gl_ref[...]


def ffn_final(x, g_ffn, wg, wu, wd, g_final, *, tm, tf):
    m = x.shape[0]
    dff = wg.shape[1]
    return pl.pallas_call(
        _ffn_kernel,
        grid=(m // tm, dff // tf),
        in_specs=[pl.BlockSpec((tm, D_MODEL), lambda i, f: (i, 0)),
                  pl.BlockSpec((1, D_MODEL), lambda i, f: (0, 0)),
                  pl.BlockSpec((D_MODEL, tf), lambda i, f: (0, f)),
                  pl.BlockSpec((D_MODEL, tf), lambda i, f: (0, f)),
                  pl.BlockSpec((tf, D_MODEL), lambda i, f: (f, 0)),
                  pl.BlockSpec((1, D_MODEL), lambda i, f: (0, 0))],
        out_specs=pl.BlockSpec((tm, D_MODEL), lambda i, f: (i, 0)),
        out_shape=jax.ShapeDtypeStruct((m, D_MODEL), jnp.float32),
        scratch_shapes=[pltpu.VMEM((tm, D_MODEL), jnp.bfloat16), pltpu.VMEM((tm, D_MODEL), jnp.float32)],
        compiler_params=_cp(("parallel", "arbitrary")),
        name="ffn_final",
    )(x, g_ffn.reshape(1, -1), wg, wu, wd, g_final.reshape(1, -1))


def cmp_part_sample(pool8, page_table, w1x_bf16, *, pp):
    b, n_pages = page_table.shape

    def xspec(k):
        return pl.BlockSpec((1, 8, 8192), lambda bi, gi, pt: (pt[bi, gi * pp + k], 0, 0))

    grid_spec = pltpu.PrefetchScalarGridSpec(
        num_scalar_prefetch=1,
        grid=(b, n_pages // pp),
        in_specs=[xspec(k) for k in range(pp)]
        + [pl.BlockSpec((2, D_CMP, 128, 256), lambda bi, gi, pt: (0, 0, 0, 0))],
        out_specs=pl.BlockSpec((1, 8 * pp, 1024), lambda bi, gi, pt: (bi, gi, 0)),
    )
    return pl.pallas_call(
        functools.partial(_cmp_part_kernel, n_in=pp, precise=False, prefetch=1),
        grid_spec=grid_spec,
        out_shape=jax.ShapeDtypeStruct((b, n_pages * 8, 1024), jnp.float32),
        compiler_params=_cp(("parallel", "arbitrary")),
        name="cmp_part_sample",
    )(page_table, *([pool8] * pp), w1x_bf16)


def _fold_heads(o_ext):
    row_kvh = lax.broadcasted_iota(jnp.int32, (N_HEADS, 1), 0) // GQA
    out = jnp.zeros((N_HEADS, HEAD_DIM), jnp.float32)
    for kvh in range(N_KV):
        out = out + jnp.where(row_kvh == kvh, o_ext[:, kvh * HEAD_DIM:(kvh + 1) * HEAD_DIM], 0.0)
    return out


def _cmp_attn_sample_kernel(part_ref, pe_ref, w1t_ref, w2bd_ref, q_ref, g8_ref, ov_ref, o_ref, imps_ref, *, q_pos):
    part = part_ref[0]
    n = part.shape[0]
    h = part[:, :512] + pltpu.roll(part[:, 512:], n - 1, axis=0)
    kv = []
    for e in range(2):
        pe = _hdot(pe_ref[e], w1t_ref[e])[0:1]
        he = _silu(h[:, e * 256:(e + 1) * 256] + pe)
        kv.append(_hdot(he, w2bd_ref[e]))
    kc, vc = kv
    s = _dot_nt(q_ref[0], kc, HI) * SCALE
    cidx = lax.broadcasted_iota(jnp.int32, (1, n), 1)
    cmask = (cidx * D_CMP + (L_CMP - 1)) <= q_pos
    s = jnp.where(cmask, s, NEG)
    m = jnp.max(s, axis=-1, keepdims=True)
    ex = jnp.where(cmask, jnp.exp(s - m), 0.0)
    p = ex / jnp.maximum(jnp.sum(ex, axis=-1, keepdims=True), 1e-30)
    o_ref[0] = _fold_heads(_bdot(p, vc))
    imps_ref[0] = _hdot(_hdot(g8_ref[...], p), ov_ref[...])


def cmp_attn_sample(part, pe8, w1t, w2bd, q16ext, g8, ov, *, q_pos):
    b, n, _ = part.shape
    nbp = ov.shape[1]
    c3 = lambda bi: (0, 0, 0)
    return pl.pallas_call(
        functools.partial(_cmp_attn_sample_kernel, q_pos=q_pos),
        grid=(b,),
        in_specs=[pl.BlockSpec((1, n, 1024), lambda bi: (bi, 0, 0)),
                  pl.BlockSpec((2, 8, 2048), c3),
                  pl.BlockSpec((2, 2048, 256), c3),
                  pl.BlockSpec((2, 256, 256), c3),
                  pl.BlockSpec((1, N_HEADS, 256), lambda bi: (bi, 0, 0)),
                  pl.BlockSpec((8, N_HEADS), lambda bi: (0, 0)),
                  pl.BlockSpec((n, nbp), lambda bi: (0, 0))],
        out_specs=[pl.BlockSpec((1, N_HEADS, HEAD_DIM), lambda bi: (bi, 0, 0)),
                   pl.BlockSpec((1, 8, nbp), lambda bi: (bi, 0, 0))],
        out_shape=[jax.ShapeDtypeStruct((b, N_HEADS, HEAD_DIM), jnp.float32),
                   jax.ShapeDtypeStruct((b, 8, nbp), jnp.float32)],
        compiler_params=_cp(("parallel",)),
        name="cmp_attn_sample",
    )(part, pe8, w1t, w2bd, q16ext, g8, ov)


def _topk_kernel(sc_ref, idx_ref, *, n_blk, q_blk, k_sel):
    imps = sc_ref[...]
    nbp = imps.shape[0]
    sidx = lax.broadcasted_iota(jnp.int32, (nbp, 1), 0)
    sf = sidx.astype(jnp.float32)
    forced = (sidx == 0) | (sidx == q_blk) | (sidx == q_blk - 1)
    causal = (sidx <= q_blk) & (sidx < n_blk)
    score = jnp.where(forced, -NEG, jnp.where(causal, imps, NEG))
    rows = []
    for _ in range(k_sel):
        m = jnp.max(score, axis=0, keepdims=True)
        pick = jnp.min(jnp.where(score == m, sf, float(nbp)), axis=0, keepdims=True)
        rows.append(jnp.where(m > 0.5 * NEG, pick, -1.0))
        score = jnp.where(sf == pick, 2.0 * NEG, score)
    idx_ref[...] = jnp.concatenate(rows, axis=0).astype(jnp.int32)


def topk_blocks(scores, *, n_blk, q_blk, k_sel):
    nbp, lanes = scores.shape
    return pl.pallas_call(
        functools.partial(_topk_kernel, n_blk=n_blk, q_blk=q_blk, k_sel=k_sel),
        out_shape=jax.ShapeDtypeStruct((k_sel, lanes), jnp.int32),
        name="topk_blocks",
    )(scores)


def _slc_sample_kernel(idx_ref, pt_ref, *refs, n_sel, new_blk):
    blk_refs = refs[:N_KV * n_sel]
    q_ref, new_ref, o_ref = refs[N_KV * n_sel:]
    b = pl.program_id(0)
    new = new_ref[0]
    knew, vnew = new[:, :KV_W], new[:, KV_W:]
    lane_blk = lax.broadcasted_iota(jnp.int32, (1, n_sel * L_SLC), 1) // L_SLC
    for kvh in range(N_KV):
        ks, vs = [], []
        dead = lane_blk < 0
        has_new = False
        for n in range(n_sel):
            blk = blk_refs[kvh * n_sel + n][0]
            ks.append(blk[:, :KV_W])
            vs.append(blk[:, KV_W:])
            s_n = idx_ref[(b * N_KV + kvh) * n_sel + n]
            dead = dead | ((lane_blk == n) & ((s_n == new_blk) | (s_n < 0)))
            has_new = jnp.logical_or(has_new, s_n == new_blk)
        kcat = jnp.concatenate(ks, axis=0)
        vcat = jnp.concatenate(vs, axis=0)
        q = q_ref[0, kvh]
        s = _dot_nt(q.astype(jnp.bfloat16), kcat.astype(jnp.bfloat16)) * SCALE
        s = jnp.where(dead, NEG, s)
        s_new = jnp.where(has_new, jnp.sum(q * knew, axis=-1, keepdims=True) * SCALE, NEG)
        m = jnp.maximum(jnp.max(s, axis=-1, keepdims=True), s_new)
        ex = jnp.where(dead, 0.0, jnp.exp(s - m))
        ex_new = jnp.where(has_new, jnp.exp(s_new - m), 0.0)
        denom = jnp.maximum(jnp.sum(ex, axis=-1, keepdims=True) + ex_new, 1e-30)
        o_ext = (_bdot(ex, vcat) + ex_new * vnew) / denom
        o_ref[0, kvh] = o_ext[:, kvh * HEAD_DIM:(kvh + 1) * HEAD_DIM]


def slc_sample(idx_flat, page_table, pool2, qext4, new_rows, *, n_sel, new_blk):
    b, n_pages = page_table.shape

    def bspec(kvh, n):
        def imap(bi, idx, pt):
            s = jnp.clip(idx[(bi * N_KV + kvh) * n_sel + n], 0, 2 * n_pages - 1)
            return (pt[bi, s // 2] * 2 + s % 2, 0, 0)
        return pl.BlockSpec((1, L_SLC, 2 * KV_W), imap)

    grid_spec = pltpu.PrefetchScalarGridSpec(
        num_scalar_prefetch=2,
        grid=(b,),
        in_specs=[bspec(kvh, n) for kvh in range(N_KV) for n in range(n_sel)]
        + [pl.BlockSpec((1, N_KV, 8, KV_W), lambda bi, idx, pt: (bi, 0, 0, 0)),
           pl.BlockSpec((1, 1, 2 * KV_W), lambda bi, idx, pt: (bi, 0, 0))],
        out_specs=pl.BlockSpec((1, N_KV, 8, HEAD_DIM), lambda bi, idx, pt: (bi, 0, 0, 0)),
    )
    return pl.pallas_call(
        functools.partial(_slc_sample_kernel, n_sel=n_sel, new_blk=new_blk),
        grid_spec=grid_spec,
        out_shape=jax.ShapeDtypeStruct((b, N_KV, 8, HEAD_DIM), jnp.float32),
        compiler_params=_cp(("arbitrary",)),
        name="slc_sample",
    )(idx_flat, page_table, *([pool2] * (N_KV * n_sel)), qext4, new_rows)


def _win_sample_kernel(w_ref, q_ref, o_ref):
    w = w_ref[0]
    s = _dot_nt(q_ref[0].astype(jnp.bfloat16), w[:, :KV_W].astype(jnp.bfloat16)) * SCALE
    m = jnp.max(s, axis=-1, keepdims=True)
    ex = jnp.exp(s - m)
    p = ex / jnp.maximum(jnp.sum(ex, axis=-1, keepdims=True), 1e-30)
    o_ref[0] = _fold_heads(_bdot(p, w[:, KV_W:]))


def win_sample(win_rows, q16ext):
    b, nk, _ = win_rows.shape
    return pl.pallas_call(
        _win_sample_kernel,
        grid=(b,),
        in_specs=[pl.BlockSpec((1, nk, 2 * KV_W), lambda bi: (bi, 0, 0)),
                  pl.BlockSpec((1, N_HEADS, KV_W), lambda bi: (bi, 0, 0))],
        out_specs=pl.BlockSpec((1, N_HEADS, HEAD_DIM), lambda bi: (bi, 0, 0)),
        out_shape=jax.ShapeDtypeStruct((b, N_HEADS, HEAD_DIM), jnp.float32),
        compiler_params=_cp(("parallel",)),
        name="win_sample",
    )(win_rows, q16ext)


def _prep_weights(w_in, mu, w2, a2, g2):
    o3 = 3 * D_RWKV
    nsa0 = R_COLS
    cols = [w_in[:, nsa0:nsa0 + D_NSA],
            w_in[:, 0:o3],
            w_in[:, nsa0 + D_NSA:nsa0 + D_NSA + 6 * KV_W],
            w_in[:, o3:R_COLS],
            w_in[:, nsa0 + D_NSA + 6 * KV_W:],
            jnp.zeros((D_MODEL, D_IN_PAD - C_GATE - 3 * N_HEADS), w_in.dtype)]
    w_in_p = jnp.concatenate(cols, axis=1).astype(jnp.bfloat16)
    z = jnp.zeros((LORA_ALL, D_RWKV), jnp.float32)
    return dict(w_in_p=w_in_p, mu_p=mu,
                w2p=z.at[0:LORA_W].set(w2), a2p=z.at[LORA_W:LORA_W + LORA_A].set(a2),
                g2p=z.at[LORA_W + LORA_A:].set(g2))


def _layer_weights(layer, g_mix, w_in, mu, w0, w2, a0, a2, g2, k_k, k_a, r_k, lnx_w, lnx_b, phi_pe, phi_w1,
                   phi_w2, w_o, g_ffn, w_gate, w_up, w_down):
    w = _prep_weights(w_in[layer], mu[layer], w2[layer], a2[layer], g2[layer])
    w1t, pe8, w2k, w2vt, w2bd = _phi_weights(phi_pe[layer], phi_w1[layer], phi_w2[layer])
    w1x = _expand_w1(phi_w1[layer])
    w.update(g_mix=g_mix[layer], w0=w0[layer], a0=a0[layer], k_k=k_k[layer], k_a=k_a[layer], r_k=r_k[layer],
             lnx_w=lnx_w[layer], lnx_b=lnx_b[layer], w1x=w1x, w1x_bf16=w1x.astype(jnp.bfloat16),
             w1t=w1t, pe8=pe8, w2k=w2k, w2vt=w2vt, w2bd=w2bd,
             gate_expand=_gate_expand_matrix(), wo=w_o[layer].astype(jnp.bfloat16), g_ffn=g_ffn[layer],
             wg=w_gate[layer].astype(jnp.bfloat16), wu=w_up[layer].astype(jnp.bfloat16),
             wd=w_down[layer].astype(jnp.bfloat16))
    return w


def _pick(n, pref):
    while n % pref:
        pref //= 2
    return pref


def _prompt_layer(x_prompt, W, g_final):
    B, T, _ = x_prompt.shape
    M = B * T
    x2 = x_prompt.reshape(M, D_MODEL)
    proj = proj_matmul(x2, W["g_mix"], W["w_in_p"], apply_norm=True, tm=_pick(M, 512), tn=1024)
    proj3 = proj.reshape(B, T, D_IN_PAD)
    ro, rg, wkv = rwkv_mix(proj, jnp.zeros((B, 1, R_COLS), jnp.float32), W,
                           jnp.zeros((B, H_R, 64, 64), jnp.float32), b=B, t=T, tm=_pick(T, 256), tc=_pick(T, 64))
    cmp_kv = proj3[..., C_CMP:C_CMP + 2 * KV_W]
    slc_kv = proj3[..., C_SLC:C_SLC + 2 * KV_W]
    win_kv = proj3[..., C_WIN:C_WIN + 2 * KV_W]
    nch = T // D_CMP
    part = cmp_part_prompt(cmp_kv.reshape(B, nch, D_CMP * 2 * KV_W), W["w1x"])
    ovt = _overlap_t(128, nch - 1, nch)
    o_cmp, sel = cmp_select_prompt(part, W["pe8"], W["w1t"], W["w2k"], W["w2vt"], proj3, ovt, tq=128)
    ks, vs = _tile_kv(slc_kv, B, T)
    kw, vw = _tile_kv(win_kv, B, T)
    o_slc, o_win = slc_win_prompt(proj3, ks, vs, kw, vw, sel, tq=128)
    r2 = lambda z: z.reshape(M, 1024)
    x1 = out_proj(r2(ro), r2(rg), r2(o_cmp), r2(o_slc), r2(o_win), proj, W["gate_expand"], x2, W["wo"],
                  tm=_pick(M, 512), tn=1024)
    y = ffn_final(x1, W["g_ffn"], W["wg"], W["wu"], W["wd"], g_final, tm=_pick(M, 512), tf=512)
    kv5 = lambda z: z.reshape(B, T, 2, N_KV, HEAD_DIM)
    nwin = min(WINDOW, T)
    return y.reshape(B, T, D_MODEL), kv5(cmp_kv), kv5(slc_kv), kv5(win_kv)[:, T - nwin:], wkv


def _sample_layer(x_sample, xn_s, cache_cmp, cache_slc, cache_win, wkv0, shift, page_table, W, g_final):
    b = x_sample.shape[0]
    n_pool, page = cache_cmp.shape[0], cache_cmp.shape[1]
    n_pages = page_table.shape[1]
    past_len = n_pages * page
    n_rows = -(-(past_len + 1) // L_SLC) * L_SLC
    n_blk = n_rows // L_SLC
    q_blk = past_len // L_SLC
    nbp = -(-n_blk // 128) * 128

    rows = jnp.concatenate([xn_s, shift], axis=0)
    proj = proj_matmul(rows, W["g_mix"], W["w_in_p"], apply_norm=False, tm=2 * b, tn=1024)
    prev = jnp.concatenate([proj[b:, C_R:C_R + 3 * D_RWKV], proj[b:, C_LORA:C_LORA + LORA_ALL]], axis=1)
    ro, rg, wkv = rwkv_mix(proj[:b], prev, W, wkv0, b=b, t=1, tm=1, tc=1)

    q = proj[:b, C_Q:C_Q + D_NSA].reshape(b, N_KV, GQA, 1, HEAD_DIM)
    eye = jnp.eye(N_KV, dtype=jnp.float32).reshape(1, N_KV, 1, N_KV, 1)
    qext = (q * eye).reshape(b, N_KV, GQA, KV_W)
    q16ext = qext.reshape(b, N_HEADS, KV_W)
    qext4 = jnp.pad(qext, ((0, 0), (0, 0), (0, 8 - GQA), (0, 0)))
    cmp_new = proj[:b, C_CMP:C_CMP + 2 * KV_W]
    slc_new = proj[:b, C_SLC:C_SLC + 2 * KV_W]
    win_new = proj[:b, C_WIN:C_WIN + 2 * KV_W]

    chunks_per_page = page // D_CMP
    part = cmp_part_sample(cache_cmp.reshape(n_pool, chunks_per_page, D_CMP * 2 * KV_W), page_table,
                           W["w1x_bf16"], pp=_pick(n_pages, 32))
    nc_rows = part.shape[1]
    ov = _overlap_t(nbp, n_rows // D_CMP - L_CMP // D_CMP + 1, nc_rows)[:n_blk].T
    ov = jnp.pad(ov, ((0, 0), (0, nbp - n_blk)))
    g8 = jnp.asarray((np.arange(8)[:, None] == np.arange(N_HEADS)[None, :] // GQA).astype(np.float32))
    o_cmp, imps = cmp_attn_sample(part, W["pe8"], W["w1t"], W["w2bd"], q16ext, g8, ov, q_pos=past_len)
    scores = imps[:, :N_KV].transpose(2, 0, 1).reshape(nbp, b * N_KV)
    k_sel = min(N_SELECT, n_blk)
    idx = topk_blocks(scores, n_blk=n_blk, q_blk=q_blk, k_sel=k_sel)
    o_slc = slc_sample(idx.T.reshape(-1), page_table, cache_slc.reshape(n_pool * (page // L_SLC), L_SLC, 2 * KV_W),
                       qext4, slc_new.reshape(b, 1, 2 * KV_W), n_sel=k_sel, new_blk=q_blk)
    o_slc = o_slc[:, :, :GQA].reshape(b, D_NSA)

    n_buf = cache_win.shape[1]
    win_all = jnp.concatenate([cache_win, win_new.reshape(b, 1, 2, N_KV, HEAD_DIM)], axis=1)
    win_keep = win_all[:, n_buf + 1 - min(WINDOW, n_buf + 1):]
    o_win = win_sample(win_keep.reshape(b, -1, 2 * KV_W), q16ext)

    x2 = x_sample.reshape(b, D_MODEL)
    x1 = out_proj(ro.reshape(b, -1), rg.reshape(b, -1), o_cmp.reshape(b, D_NSA), o_slc, o_win.reshape(b, D_NSA),
                  proj, W["gate_expand"], x2, W["wo"], tm=b, tn=1024)
    y = ffn_final(x1, W["g_ffn"], W["wg"], W["wu"], W["wd"], g_final, tm=b, tf=512)
    kv5 = lambda z: z.reshape(b, 1, 2, N_KV, HEAD_DIM)
    return y.reshape(b, 1, D_MODEL), kv5(cmp_new), kv5(slc_new), win_keep, wkv


def kernel(x_prompt, x_sample, cache_cmp_kv, cache_slc_kv, cache_win_kv, state_wkv, state_shift, page_table, g_mix, w_in, mu, w0, w2, a0, a2, g2, k_k, k_a, r_k, lnx_w, lnx_b, phi_pe, phi_w1, phi_w2, w_o, g_ffn, w_gate, w_up, w_down, g_final):
    W = _layer_weights(0, g_mix, w_in, mu, w0, w2, a0, a2, g2, k_k, k_a, r_k, lnx_w, lnx_b, phi_pe, phi_w1,
                       phi_w2, w_o, g_ffn, w_gate, w_up, w_down)
    bp, bs = x_prompt.shape[0], x_sample.shape[0]
    small = jnp.concatenate([x_sample.reshape(bs, D_MODEL), x_prompt[:, -1]], axis=0)
    pad = (-small.shape[0]) % 8
    xn_small = rmsnorm_rows(jnp.pad(small, ((0, pad), (0, 0))), W["g_mix"])
    xn_s, shift_p = xn_small[:bs], xn_small[bs:bs + bp]

    y_p, cmp_p, slc_p, win_p, wkv_p = _prompt_layer(x_prompt, W, g_final)
    y_s, cmp_s, slc_s, win_s, wkv_s = _sample_layer(x_sample, xn_s, cache_cmp_kv[0], cache_slc_kv[0],
                                                    cache_win_kv[0], state_wkv[0], state_shift[0], page_table,
                                                    W, g_final)
    return (y_p, y_s, cmp_p[None], slc_p[None], win_p[None], wkv_p[None], shift_p[None],
            cmp_s[None], slc_s[None], win_s[None], wkv_s[None], xn_s[None])
```

```python
import functools

import jax
import jax.numpy as jnp
import numpy as np
from jax import lax
from jax.experimental import pallas as pl
from jax.experimental.pallas import tpu as pltpu

D_MODEL = 2048
D_RWKV = 1024
D_NSA = 1024
HEAD_DIM_R = 64
H_R = 16
LORA_W = 96
LORA_A = 96
LORA_G = 64
LORA_ALL = LORA_W + LORA_A + LORA_G
HEAD_DIM = 64
N_HEADS = 16
N_KV = 4
GQA = 4
KV_W = 256
L_CMP = 32
D_CMP = 16
L_SLC = 64
N_SELECT = 16
WINDOW = 512
PHI_HIDDEN = 64
R_COLS = 3 * D_RWKV + LORA_ALL
RMS_EPS = 1e-6
LNX_EPS = 64e-5
SCALE = HEAD_DIM ** -0.5
NEG = -1e30

C_Q, C_R, C_K, C_V = 0, 1024, 2048, 3072
C_CMP, C_SLC, C_WIN = 4096, 4608, 5120
C_LORA, C_GATE = 5632, 5888
D_IN_PAD = 6144

VMEM_LIMIT = 48 * 1024 * 1024
HI = lax.Precision.HIGHEST


def _cp(sem, limit=VMEM_LIMIT):
    return pltpu.CompilerParams(dimension_semantics=sem, vmem_limit_bytes=limit)


def _bdot(a, b):
    return jnp.dot(a.astype(jnp.bfloat16), b.astype(jnp.bfloat16), preferred_element_type=jnp.float32)


def _hdot(a, b):
    return jnp.dot(a, b, precision=HI, preferred_element_type=jnp.float32)


def _dot_nt(a, b, precision=None):
    return lax.dot_general(a, b, (((1,), (1,)), ((), ())), precision=precision,
                           preferred_element_type=jnp.float32)


def _sigmoid(x):
    return 1.0 / (1.0 + jnp.exp(-x))


def _silu(x):
    return x * _sigmoid(x)


def _rmsnorm_rows_kernel(x_ref, g_ref, o_ref):
    x = x_ref[...]
    ms = jnp.mean(x * x, axis=-1, keepdims=True)
    o_ref[...] = x * lax.rsqrt(ms + RMS_EPS) * g_ref[...]


def rmsnorm_rows(x, g):
    m, d = x.shape
    return pl.pallas_call(
        _rmsnorm_rows_kernel,
        out_shape=jax.ShapeDtypeStruct((m, d), jnp.float32),
        name="rmsnorm_rows",
    )(x, g.reshape(1, d))


def _proj_kernel(x_ref, g_ref, w_ref, o_ref, xn_ref, *, apply_norm):
    @pl.when(pl.program_id(1) == 0)
    def _():
        x = x_ref[...]
        if apply_norm:
            ms = jnp.mean(x * x, axis=-1, keepdims=True)
            x = x * lax.rsqrt(ms + RMS_EPS) * g_ref[...]
        xn_ref[...] = x.astype(jnp.bfloat16)

    o_ref[...] = jnp.dot(xn_ref[...], w_ref[...], preferred_element_type=jnp.float32)


def proj_matmul(x, g, w_bf16, *, apply_norm, tm, tn):
    m, d = x.shape
    n = w_bf16.shape[1]
    return pl.pallas_call(
        functools.partial(_proj_kernel, apply_norm=apply_norm),
        grid=(m // tm, n // tn),
        in_specs=[
            pl.BlockSpec((tm, d), lambda i, j: (i, 0)),
            pl.BlockSpec((1, d), lambda i, j: (0, 0)),
            pl.BlockSpec((d, tn), lambda i, j: (0, j)),
        ],
        out_specs=pl.BlockSpec((tm, tn), lambda i, j: (i, j)),
        out_shape=jax.ShapeDtypeStruct((m, n), jnp.float32),
        scratch_shapes=[pltpu.VMEM((tm, d), jnp.bfloat16)],
        compiler_params=_cp(("parallel", "arbitrary")),
        name="proj_matmul",
    )(x, g.reshape(1, d), w_bf16)


def _rwkv_prep_kernel(pr_ref, pk_ref, pv_ref, pl_ref, prev_ref, mu_ref, w0_ref, a0_ref, kk_ref, ka_ref,
                      w2_ref, a2_ref, g2_ref,
                      r_out, w_out, k_out, v_out, kk_out, a_out, g_out, carry_ref, *, row_prev):
    tm = pr_ref.shape[1]
    pr, pk, pv, plo = pr_ref[0], pk_ref[0], pv_ref[0], pl_ref[0]
    if row_prev:
        def shift_mix(p, lo, hi):
            return p + (prev_ref[0, :, lo:hi] - p) * mu_ref[:, lo:hi]
    else:
        @pl.when(pl.program_id(1) == 0)
        def _():
            carry_ref[...] = prev_ref[0]

        row0 = lax.broadcasted_iota(jnp.int32, (tm, 1), 0) == 0

        def shift_mix(p, lo, hi):
            prev = pltpu.roll(p, 1, axis=0)
            prev = jnp.where(row0, carry_ref[:, lo:hi], prev)
            return p + (prev - p) * mu_ref[:, lo:hi]

    xr = shift_mix(pr, 0, 1024)
    xk = shift_mix(pk, 1024, 2048)
    xv = shift_mix(pv, 2048, 3072)
    xl = shift_mix(plo, 3072, 3328)
    if not row_prev:
        last = tm - 1
        carry_ref[:, 0:1024] = pr[last:last + 1]
        carry_ref[:, 1024:2048] = pk[last:last + 1]
        carry_ref[:, 2048:3072] = pv[last:last + 1]
        carry_ref[:, 3072:3328] = plo[last:last + 1]

    wl = _hdot(jnp.tanh(xl), w2_ref[...])
    al = _hdot(xl, a2_ref[...])
    gl = _hdot(_sigmoid(xl), g2_ref[...])
    z = -(w0_ref[...] + wl)
    softplus = jnp.maximum(z, 0.0) + jnp.log(1.0 + jnp.exp(-jnp.abs(z)))
    w_log = -softplus - 0.5
    decay = jnp.exp(-jnp.exp(w_log))
    a = _sigmoid(a0_ref[...] + al)
    r_out[0] = xr
    w_out[0] = decay
    k_out[0] = xk * (1.0 + (a - 1.0) * ka_ref[...])
    v_out[0] = xv
    kk_out[0] = xk * kk_ref[...]
    a_out[0] = a
    g_out[0] = gl


def rwkv_prep(proj, prev, mu_p, w0, a0, k_k, k_a, w2p, a2p, g2p, *, b, t, tm, row_prev=False):
    proj3 = proj.reshape(b, t, D_IN_PAD)
    nt = t // tm
    prev_spec = (pl.BlockSpec((1, tm, R_COLS), lambda bi, ti: (bi, ti, 0)) if row_prev
                 else pl.BlockSpec((1, 1, R_COLS), lambda bi, ti: (bi, 0, 0)))
    row = lambda blk, w: pl.BlockSpec((1, tm, w), lambda bi, ti: (bi, ti, blk))
    vec = lambda w: pl.BlockSpec((1, w), lambda bi, ti: (0, 0))
    mat = pl.BlockSpec((LORA_ALL, D_RWKV), lambda bi, ti: (0, 0))
    out = pl.BlockSpec((1, tm, D_RWKV), lambda bi, ti: (bi, ti, 0))
    shp = jax.ShapeDtypeStruct((b, t, D_RWKV), jnp.float32)
    return pl.pallas_call(
        functools.partial(_rwkv_prep_kernel, row_prev=row_prev),
        grid=(b, nt),
        in_specs=[row(C_R // 1024, 1024), row(C_K // 1024, 1024), row(C_V // 1024, 1024),
                  row(C_LORA // 256, 256),
                  prev_spec,
                  vec(R_COLS), vec(D_RWKV), vec(D_RWKV), vec(D_RWKV), vec(D_RWKV),
                  mat, mat, mat],
        out_specs=[out] * 7,
        out_shape=[shp] * 7,
        scratch_shapes=[pltpu.VMEM((1, R_COLS), jnp.float32)],
        compiler_params=_cp(("parallel", "arbitrary")),
        name="rwkv_prep",
    )(proj3, proj3, proj3, proj3, prev, mu_p.reshape(1, -1), w0.reshape(1, -1), a0.reshape(1, -1),
      k_k.reshape(1, -1), k_a.reshape(1, -1), w2p, a2p, g2p)


def _rwkv_scan_kernel(r_ref, w_ref, k_ref, kk_ref, a_ref, v_ref, s0_ref, rk_ref, lnw_ref, lnb_ref,
                      o_ref, sfin_ref, s_ref, kkn_ref, b_ref, y_ref, *, halved):
    tc = r_ref.shape[1]
    ni = v_ref.shape[2]

    @pl.when(pl.program_id(1) == 0)
    def _():
        s_ref[...] = s0_ref[0]

    kkraw = kk_ref[0]
    nrm = jnp.sqrt(jnp.sum(kkraw * kkraw, axis=1, keepdims=True))
    kkn = kkraw / jnp.maximum(nrm, 1e-12)
    kkn_ref[...] = kkn
    b_ref[...] = kkn * a_ref[0]

    def step(t, carry):
        w_t = w_ref[0, t]
        k_t = k_ref[0, t]
        r_t = r_ref[0, t]
        kk_t = kkn_ref[t]
        b_t = b_ref[t]
        for i in range(ni):
            s = s_ref[i]
            sa = -jnp.sum(s * kk_t, axis=0, keepdims=True)
            v_i = v_ref[0, t, pl.ds(i, 1), :]
            s = s * w_t + sa * b_t + v_i * k_t
            y_ref[t, pl.ds(i, 1), :] = jnp.sum(s * r_t, axis=0, keepdims=True)
            s_ref[i] = s
        return carry

    lax.fori_loop(0, tc, step, 0)

    y = y_ref[...]
    v = v_ref[0]

    def head_sum(z):
        s = jnp.sum(z, axis=1, keepdims=True)
        if halved:
            s = s + pltpu.roll(s, 64, axis=2)
        return s

    mean = head_sum(y) * (1.0 / HEAD_DIM_R)
    d = y - mean
    var = head_sum(d * d) * (1.0 / HEAD_DIM_R)
    yn = d * lax.rsqrt(var + LNX_EPS) * lnw_ref[0] + lnb_ref[0]
    bonus = jnp.sum(r_ref[0] * k_ref[0] * rk_ref[0], axis=1, keepdims=True)
    o_ref[0] = yn + bonus * v

    @pl.when(pl.program_id(1) == pl.num_programs(1) - 1)
    def _():
        sfin_ref[0] = s_ref[...]


def rwkv_scan(r, w, k, kk, a, v, s0, rk, lnw, lnb, *, tc, halved):
    g, t = r.shape[0], r.shape[1]
    ni = v.shape[2]
    col = pl.BlockSpec((1, tc, 64, 128), lambda gi, ti: (gi, ti, 0, 0))
    vspec = pl.BlockSpec((1, tc, ni, 128), lambda gi, ti: (gi, ti, 0, 0))
    sspec = pl.BlockSpec((1, ni, 64, 128), lambda gi, ti: (gi, 0, 0, 0))
    return pl.pallas_call(
        functools.partial(_rwkv_scan_kernel, halved=halved),
        grid=(g, t // tc),
        in_specs=[col, col, col, col, col, vspec, sspec,
                  pl.BlockSpec((1, 64, 128), lambda gi, ti: (gi, 0, 0)),
                  pl.BlockSpec((1, ni, 128), lambda gi, ti: (gi, 0, 0)),
                  pl.BlockSpec((1, ni, 128), lambda gi, ti: (gi, 0, 0))],
        out_specs=[vspec, sspec],
        out_shape=[jax.ShapeDtypeStruct((g, t, ni, 128), jnp.float32),
                   jax.ShapeDtypeStruct((g, ni, 64, 128), jnp.float32)],
        scratch_shapes=[pltpu.VMEM((ni, 64, 128), jnp.float32),
                        pltpu.VMEM((tc, 64, 128), jnp.float32),
                        pltpu.VMEM((tc, 64, 128), jnp.float32),
                        pltpu.VMEM((tc, ni, 128), jnp.float32)],
        compiler_params=_cp(("parallel", "arbitrary")),
        name="rwkv_scan",
    )(r, w, k, kk, a, v, s0, rk, lnw, lnb)


def _to_scan_layout(x, b, t, halved):
    z = x.reshape(b, t, H_R, 64).transpose(1, 3, 0, 2).reshape(t, 64, b * H_R)
    if halved:
        return jnp.concatenate([z, z], axis=-1)[None]
    g = (b * H_R) // 128
    return z.reshape(t, 64, g, 128).transpose(2, 0, 1, 3)


def _v_to_scan_layout(x, b, t, halved):
    if halved:
        return x.reshape(b, t, H_R, 2, 32).transpose(1, 4, 3, 0, 2).reshape(1, t, 32, 128)
    g = (b * H_R) // 128
    z = x.reshape(b, t, H_R, 64).transpose(1, 3, 0, 2).reshape(t, 64, g, 128)
    return z.transpose(2, 0, 1, 3)


def _vecparam_scan_layout(p, b, halved):
    z = jnp.broadcast_to(p.reshape(1, H_R, 64), (b, H_R, 64))
    if halved:
        return z.reshape(b, H_R, 2, 32).transpose(3, 2, 0, 1).reshape(1, 32, 128)
    g = (b * H_R) // 128
    return z.transpose(2, 0, 1).reshape(64, g, 128).transpose(1, 0, 2)


def _keyparam_scan_layout(p, b, halved):
    z = jnp.broadcast_to(p.reshape(1, H_R, 64), (b, H_R, 64)).transpose(2, 0, 1).reshape(64, b * H_R)
    if halved:
        return jnp.concatenate([z, z], axis=-1)[None]
    g = (b * H_R) // 128
    return z.reshape(64, g, 128).transpose(1, 0, 2)


def _from_scan_layout(o, b, t, halved):
    if halved:
        return o.reshape(t, 32, 2, b, H_R).transpose(3, 0, 4, 2, 1).reshape(b, t, D_RWKV)
    g = o.shape[0]
    return o.transpose(1, 2, 0, 3).reshape(t, 64, b, H_R).transpose(2, 0, 3, 1).reshape(b, t, D_RWKV)


def _state_to_scan_layout(s, b, halved):
    if halved:
        return s.reshape(b, H_R, 2, 32, 64).transpose(3, 4, 2, 0, 1).reshape(1, 32, 64, 128)
    g = (b * H_R) // 128
    return s.reshape(g, 128, 64, 64).transpose(0, 2, 3, 1)


def _state_from_scan_layout(s, b, halved):
    if halved:
        return s.reshape(32, 64, 2, b, H_R).transpose(3, 4, 2, 0, 1).reshape(b, H_R, 64, 64)
    return s.transpose(0, 3, 1, 2).reshape(b, H_R, 64, 64)


def rwkv_mix(proj, prev, wts, s0, *, b, t, tm, tc):
    halved = (b * H_R == 64)
    if t == 1:
        outs = rwkv_prep(proj, prev.reshape(1, b, R_COLS), wts["mu_p"], wts["w0"], wts["a0"], wts["k_k"],
                         wts["k_a"], wts["w2p"], wts["a2p"], wts["g2p"], b=1, t=b, tm=b, row_prev=True)
        r, w, k, v, kk, a, g = [z.reshape(b, 1, D_RWKV) for z in outs]
    else:
        r, w, k, v, kk, a, g = rwkv_prep(proj, prev, wts["mu_p"], wts["w0"], wts["a0"], wts["k_k"], wts["k_a"],
                                         wts["w2p"], wts["a2p"], wts["g2p"], b=b, t=t, tm=tm)
    tr = lambda z: _to_scan_layout(z, b, t, halved)
    o, sfin = rwkv_scan(tr(r), tr(w), tr(k), tr(kk), tr(a), _v_to_scan_layout(v, b, t, halved),
                        _state_to_scan_layout(s0, b, halved),
                        _keyparam_scan_layout(wts["r_k"], b, halved),
                        _vecparam_scan_layout(wts["lnx_w"], b, halved),
                        _vecparam_scan_layout(wts["lnx_b"], b, halved),
                        tc=tc, halved=halved)
    return _from_scan_layout(o, b, t, halved), g, _state_from_scan_layout(sfin, b, halved)


def _cmp_part_kernel(*refs, n_in, precise, prefetch):
    refs = refs[prefetch:]
    x_refs, w_ref, o_ref = refs[:n_in], refs[n_in], refs[n_in + 1]
    for e in range(2):
        for hp in range(2):
            acc = None
            for j in range(D_CMP):
                lo = j * 512 + e * 256 + hp * 128
                xs = [xr[0, :, lo:lo + 128] for xr in x_refs]
                x = xs[0] if n_in == 1 else jnp.concatenate(xs, axis=0)
                d = _hdot(x, w_ref[e, j]) if precise else _bdot(x, w_ref[e, j])
                acc = d if acc is None else acc + d
            c0 = e * 256 + hp * 128
            o_ref[0, :, c0:c0 + 128] = acc[:, :128]
            o_ref[0, :, 512 + c0:512 + c0 + 128] = acc[:, 128:]


def cmp_part_prompt(rows, w1x):
    b, nch, _ = rows.shape
    return pl.pallas_call(
        functools.partial(_cmp_part_kernel, n_in=1, precise=True, prefetch=0),
        grid=(b,),
        in_specs=[pl.BlockSpec((1, nch, 8192), lambda bi: (bi, 0, 0)),
                  pl.BlockSpec((2, D_CMP, 128, 256), lambda bi: (0, 0, 0, 0))],
        out_specs=pl.BlockSpec((1, nch, 1024), lambda bi: (bi, 0, 0)),
        out_shape=jax.ShapeDtypeStruct((b, nch, 1024), jnp.float32),
        compiler_params=_cp(("parallel",)),
        name="cmp_part_prompt",
    )(rows, w1x)


def _expand_w1(phi_w1):
    w1 = phi_w1.reshape(2, 2, D_CMP, HEAD_DIM, PHI_HIDDEN)
    eye = jnp.eye(2, dtype=phi_w1.dtype)
    w = jnp.einsum("ab,esjdf->ejadsbf", eye, w1)
    return w.reshape(2, D_CMP, 128, 256)


def _cmp_kv_from_part(part, pe_ref, w1t_ref, w2k_ref, w2vt_ref, kc_ref, vct_ref):
    n = part.shape[0]
    h = part[:, :512] + pltpu.roll(part[:, 512:], n - 1, axis=0)
    for e in range(2):
        pe = _hdot(pe_ref[e], w1t_ref[e])[0:1]
        he = _silu(h[:, e * 256:(e + 1) * 256] + pe)
        for kvh in range(N_KV):
            if e == 0:
                kc_ref[kvh] = _hdot(he, w2k_ref[kvh])
            else:
                vct_ref[kvh] = _dot_nt(w2vt_ref[kvh], he, HI)


def _cmp_sel_kernel(part_ref, pe_ref, w1t_ref, w2k_ref, w2vt_ref, q_ref, ovt_ref, o_ref, sel_ref,
                    kc_ref, vct_ref, *, tq, n_blk):
    ti = pl.program_id(1)

    @pl.when(ti == 0)
    def _():
        _cmp_kv_from_part(part_ref[0], pe_ref, w1t_ref, w2k_ref, w2vt_ref, kc_ref, vct_ref)

    nc = kc_ref.shape[1]
    nbp = -(-n_blk // 8) * 8
    qpos = ti * tq + lax.broadcasted_iota(jnp.int32, (1, tq), 1)
    cidx = lax.broadcasted_iota(jnp.int32, (nc, 1), 0)
    cmask = (cidx * D_CMP + (L_CMP - 1)) <= qpos
    sidx = cidx[:nbp]
    qblk = qpos // L_SLC
    forced = (sidx == 0) | (sidx == qblk) | (sidx == qblk - 1)
    causal = sidx <= qblk
    for kvh in range(N_KV):
        qt = q_ref[0, :, kvh * 256:(kvh + 1) * 256].T
        impt = jnp.zeros((nc, tq), jnp.float32)
        parts = []
        for g in range(GQA):
            st = _hdot(kc_ref[kvh], qt[g * 64:(g + 1) * 64]) * SCALE
            st = jnp.where(cmask, st, NEG)
            m = jnp.max(st, axis=0, keepdims=True)
            ex = jnp.where(cmask, jnp.exp(st - m), 0.0)
            pt = ex / jnp.maximum(jnp.sum(ex, axis=0, keepdims=True), 1e-30)
            impt = impt + pt
            parts.append(_bdot(vct_ref[kvh], pt))
        o_ref[0, :, kvh * 256:(kvh + 1) * 256] = jnp.concatenate(parts, axis=0).T
        imps = _hdot(ovt_ref[...], impt)[:nbp]
        score = jnp.where(forced, -NEG, jnp.where(causal, imps, NEG))
        rank = jnp.zeros((nbp, tq), jnp.float32)
        for s2 in range(n_blk):
            row = score[s2:s2 + 1]
            beats = (row > score) | ((row == score) & (s2 < sidx))
            rank = rank + beats.astype(jnp.float32)
        seln = ((rank < min(N_SELECT, n_blk)) & causal).astype(jnp.float32)
        sel_ref[0, kvh] = jnp.concatenate([seln, jnp.zeros((128 - nbp, tq), jnp.float32)], axis=0)


def cmp_select_prompt(part, pe8, w1t, w2k, w2vt, proj3, ovt, *, tq):
    b, nc, _ = part.shape
    t = proj3.shape[1]
    c3 = lambda bi, ti: (0, 0, 0)
    return pl.pallas_call(
        functools.partial(_cmp_sel_kernel, tq=tq, n_blk=t // L_SLC),
        grid=(b, t // tq),
        in_specs=[pl.BlockSpec((1, nc, 1024), lambda bi, ti: (bi, 0, 0)),
                  pl.BlockSpec((2, 8, 2048), c3),
                  pl.BlockSpec((2, 2048, 256), c3),
                  pl.BlockSpec((N_KV, 256, 64), c3),
                  pl.BlockSpec((N_KV, 64, 256), c3),
                  pl.BlockSpec((1, tq, 1024), lambda bi, ti: (bi, ti, C_Q // 1024)),
                  pl.BlockSpec((128, nc), lambda bi, ti: (0, 0))],
        out_specs=[pl.BlockSpec((1, tq, 1024), lambda bi, ti: (bi, ti, 0)),
                   pl.BlockSpec((1, N_KV, 128, tq), lambda bi, ti: (bi, 0, 0, ti))],
        out_shape=[jax.ShapeDtypeStruct((b, t, 1024), jnp.float32),
                   jax.ShapeDtypeStruct((b, N_KV, 128, t), jnp.float32)],
        scratch_shapes=[pltpu.VMEM((N_KV, nc, 64), jnp.float32),
                        pltpu.VMEM((N_KV, 64, nc), jnp.float32)],
        compiler_params=_cp(("parallel", "arbitrary")),
        name="cmp_select_prompt",
    )(part, pe8, w1t, w2k, w2vt, proj3, ovt)


def _overlap_t(n_blocks_pad, nc, nc_pad):
    c_start = np.arange(nc_pad) * D_CMP
    s_start = np.arange(n_blocks_pad) * L_SLC
    ov = (c_start[None, :] < s_start[:, None] + L_SLC) & (c_start[None, :] + L_CMP > s_start[:, None])
    ov &= (np.arange(nc_pad) < nc)[None, :]
    return jnp.asarray(ov.astype(np.float32))


def _phi_weights(phi_pe, phi_w1, phi_w2):
    w1t = jnp.tile(phi_w1, (1, 1, N_KV))
    pe8 = jnp.broadcast_to(phi_pe.reshape(2, 1, L_CMP * HEAD_DIM), (2, 8, L_CMP * HEAD_DIM))
    z = jnp.zeros((N_KV, N_KV, PHI_HIDDEN, HEAD_DIM), jnp.float32)
    idx = jnp.arange(N_KV)
    w2k = z.at[idx, idx].set(phi_w2[0]).reshape(N_KV, 256, 64)
    w2vt = jnp.transpose(z.at[idx, idx].set(phi_w2[1]), (0, 3, 1, 2)).reshape(N_KV, 64, 256)
    eye = jnp.eye(N_KV, dtype=jnp.float32)
    w2bd = jnp.einsum("ab,efd->eafbd", eye, phi_w2).reshape(2, 256, 256)
    return w1t, pe8, w2k, w2vt, w2bd


def _flash_t(qt, k_ref, vt_ref, lo, hi, mask_fn, tq):
    def body(kb, carry):
        k = k_ref[0, 0, kb]
        vt = vt_ref[0, 0, kb]
        msk = mask_fn(kb)
        out = []
        for g in range(GQA):
            m, l, acc = carry[g]
            s = jnp.dot(k, qt[g * HEAD_DIM:(g + 1) * HEAD_DIM], preferred_element_type=jnp.float32) * SCALE
            s = jnp.where(msk, s, NEG)
            m_new = jnp.maximum(m, jnp.max(s, axis=0, keepdims=True))
            alpha = jnp.exp(m - m_new)
            p = jnp.where(msk, jnp.exp(s - m_new), 0.0)
            l = alpha * l + jnp.sum(p, axis=0, keepdims=True)
            acc = alpha * acc + jnp.dot(vt, p.astype(jnp.bfloat16), preferred_element_type=jnp.float32)
            out.append((m_new, l, acc))
        return tuple(out)

    init = tuple((jnp.full((1, tq), NEG, jnp.float32), jnp.zeros((1, tq), jnp.float32),
                  jnp.zeros((HEAD_DIM, tq), jnp.float32)) for _ in range(GQA))
    fin = lax.fori_loop(lo, hi, body, init)
    return jnp.concatenate([acc / jnp.maximum(l, 1e-30) for (_, l, acc) in fin], axis=0)


def _slc_win_kernel(q_ref, ks_ref, vs_ref, kw_ref, vw_ref, sel_ref, os_ref, ow_ref, *, tq):
    ti = pl.program_id(2)
    qt = q_ref[0].T.astype(jnp.bfloat16)
    qpos = ti * tq + lax.broadcasted_iota(jnp.int32, (1, tq), 1)
    krow = lax.broadcasted_iota(jnp.int32, (128, 1), 0)
    selt = sel_ref[0, 0].astype(jnp.bfloat16)
    khalf = lax.broadcasted_iota(jnp.int32, (128, 128), 0) // L_SLC
    scol = lax.broadcasted_iota(jnp.int32, (128, 128), 1)

    def slc_mask(kb):
        expand = (scol == 2 * kb + khalf).astype(jnp.bfloat16)
        on = jnp.dot(expand, selt, preferred_element_type=jnp.float32) > 0.5
        return on & ((kb * 128 + krow) <= qpos)

    def win_mask(kb):
        diff = qpos - (kb * 128 + krow)
        return (diff >= 0) & (diff < WINDOW)

    os_ref[0] = _flash_t(qt, ks_ref, vs_ref, 0, ti + 1, slc_mask, tq).T
    ow_ref[0] = _flash_t(qt, kw_ref, vw_ref, jnp.maximum(ti - WINDOW // 128, 0), ti + 1, win_mask, tq).T


def slc_win_prompt(proj3, ks, vs, kw, vw, selt, *, tq):
    b, t, _ = proj3.shape
    nkb = t // 128
    kspec = pl.BlockSpec((1, 1, nkb, 128, HEAD_DIM), lambda bi, hi, ti: (bi, hi, 0, 0, 0))
    vspec = pl.BlockSpec((1, 1, nkb, HEAD_DIM, 128), lambda bi, hi, ti: (bi, hi, 0, 0, 0))
    out = pl.BlockSpec((1, tq, 256), lambda bi, hi, ti: (bi, ti, hi))
    shp = jax.ShapeDtypeStruct((b, t, 1024), jnp.float32)
    return pl.pallas_call(
        functools.partial(_slc_win_kernel, tq=tq),
        grid=(b, N_KV, t // tq),
        in_specs=[pl.BlockSpec((1, tq, 256), lambda bi, hi, ti: (bi, ti, hi)),
                  kspec, vspec, kspec, vspec,
                  pl.BlockSpec((1, 1, 128, tq), lambda bi, hi, ti: (bi, hi, 0, ti))],
        out_specs=[out, out],
        out_shape=[shp, shp],
        compiler_params=_cp(("parallel", "parallel", "arbitrary")),
        name="slc_win_prompt",
    )(proj3, ks, vs, kw, vw, selt)


def _block_kv(kv_rows, b, t):
    kv = kv_rows.reshape(b, t // 128, 128, 2, N_KV, HEAD_DIM).astype(jnp.bfloat16)
    return kv[:, :, :, 0].transpose(0, 3, 1, 2, 4), kv[:, :, :, 1].transpose(0, 3, 1, 4, 2)


def _out_proj_kernel(ro_ref, rg_ref, oc_ref, os_ref, ow_ref, gt_ref, ex_ref, x_ref, wo_ref, y_ref, a_ref):
    @pl.when(pl.program_id(1) == 0)
    def _():
        gates = _hdot(_sigmoid(gt_ref[...]), ex_ref[...])
        nsa = (gates[:, 0:1024] * oc_ref[...] + gates[:, 1024:2048] * os_ref[...]
               + gates[:, 2048:3072] * ow_ref[...])
        a_ref[:, 0:1024] = (ro_ref[...] * rg_ref[...]).astype(jnp.bfloat16)
        a_ref[:, 1024:2048] = nsa.astype(jnp.bfloat16)

    y_ref[...] = x_ref[...] + jnp.dot(a_ref[...], wo_ref[...], preferred_element_type=jnp.float32)


def out_proj(ro, rg, oc, osl, ow, proj, gate_expand, x, wo_bf16, *, tm, tn):
    m = x.shape[0]
    row = lambda w: pl.BlockSpec((tm, w), lambda i, j: (i, 0))
    return pl.pallas_call(
        _out_proj_kernel,
        grid=(m // tm, D_MODEL // tn),
        in_specs=[row(1024), row(1024), row(1024), row(1024), row(1024),
                  pl.BlockSpec((tm, 256), lambda i, j: (i, C_GATE // 256)),
                  pl.BlockSpec((256, 3072), lambda i, j: (0, 0)),
                  pl.BlockSpec((tm, tn), lambda i, j: (i, j)),
                  pl.BlockSpec((D_MODEL, tn), lambda i, j: (0, j))],
        out_specs=pl.BlockSpec((tm, tn), lambda i, j: (i, j)),
        out_shape=jax.ShapeDtypeStruct((m, D_MODEL), jnp.float32),
        scratch_shapes=[pltpu.VMEM((tm, D_MODEL), jnp.bfloat16)],
        compiler_params=_cp(("parallel", "arbitrary")),
        name="out_proj",
    )(ro, rg, oc, osl, ow, proj, gate_expand, x, wo_bf16)


def _gate_expand_matrix():
    e = np.zeros((256, 3 * D_NSA), np.float32)
    for k in range(3):
        for h in range(N_HEADS):
            e[k * N_HEADS + h, k * D_NSA + h * HEAD_DIM:k * D_NSA + (h + 1) * HEAD_DIM] = 1.0
    return jnp.asarray(e)


def _ffn_kernel(x_ref, gf_ref, wg_ref, wu_ref, wd_ref, gl_ref, y_ref, h_ref, acc_ref):
    f = pl.program_id(1)

    @pl.when(f == 0)
    def _():
        x = x_ref[...]
        ms = jnp.mean(x * x, axis=-1, keepdims=True)
        h_ref[...] = (x * lax.rsqrt(ms + RMS_EPS) * gf_ref[...]).astype(jnp.bfloat16)
        acc_ref[...] = jnp.zeros_like(acc_ref)

    h = h_ref[...]
    gate = jnp.dot(h, wg_ref[...], preferred_element_type=jnp.float32)
    up = jnp.dot(h, wu_ref[...], preferred_element_type=jnp.float32)
    act = (_silu(gate) * up).astype(jnp.bfloat16)
    acc_ref[...] += jnp.dot(act, wd_ref[...], preferred_element_type=jnp.float32)

    @pl.when(f == pl.num_programs(1) - 1)
    def _():
        z = x_ref[...] + acc_ref[...]
        ms = jnp.mean(z * z, axis=-1, keepdims=True)
        y_ref[...] = z * lax.rsqrt(ms + RMS_EPS) * gl_ref[...]


def ffn_final(x, g_ffn, wg, wu, wd, g_final, *, tm, tf):
    m = x.shape[0]
    dff = wg.shape[1]
    return pl.pallas_call(
        _ffn_kernel,
        grid=(m // tm, dff // tf),
        in_specs=[pl.BlockSpec((tm, D_MODEL), lambda i, f: (i, 0)),
                  pl.BlockSpec((1, D_MODEL), lambda i, f: (0, 0)),
                  pl.BlockSpec((D_MODEL, tf), lambda i, f: (0, f)),
                  pl.BlockSpec((D_MODEL, tf), lambda i, f: (0, f)),
                  pl.BlockSpec((tf, D_MODEL), lambda i, f: (f, 0)),
                  pl.BlockSpec((1, D_MODEL), lambda i, f: (0, 0))],
        out_specs=pl.BlockSpec((tm, D_MODEL), lambda i, f: (i, 0)),
        out_shape=jax.ShapeDtypeStruct((m, D_MODEL), jnp.float32),
        scratch_shapes=[pltpu.VMEM((tm, D_MODEL), jnp.bfloat16), pltpu.VMEM((tm, D_MODEL), jnp.float32)],
        compiler_params=_cp(("parallel", "arbitrary")),
        name="ffn_final",
    )(x, g_ffn.reshape(1, -1), wg, wu, wd, g_final.reshape(1, -1))


def _cmp_part_paged_kernel(pt_ref, *refs, n_in):
    x_refs = refs[:n_in]
    perm_ref, w_ref, o_ref, xs_ref = refs[n_in:]
    cpp = x_refs[0].shape[-1] // D_CMP
    n_eh = 2 * N_KV
    for p in range(n_in):
        x = x_refs[p][0].reshape(n_eh * HEAD_DIM, x_refs[p].shape[-1])
        xp_all = jnp.dot(x.astype(jnp.bfloat16), perm_ref[...], preferred_element_type=jnp.float32)
        for eh in range(n_eh):
            xp = xp_all[eh * HEAD_DIM:(eh + 1) * HEAD_DIM].T
            for j in range(D_CMP):
                xs_ref[eh, j, p * cpp:(p + 1) * cpp, :] = xp[j * cpp:(j + 1) * cpp]
    for e in range(2):
        for h in range(N_KV):
            acc = None
            for j in range(D_CMP):
                xj = xs_ref[e * N_KV + h, j]
                d = _bdot(xj, w_ref[e, j])
                acc = d if acc is None else acc + d
            c0 = (e * N_KV + h) * PHI_HIDDEN
            o_ref[0, :, c0:c0 + PHI_HIDDEN] = acc[:, :PHI_HIDDEN]
            o_ref[0, :, 512 + c0:512 + c0 + PHI_HIDDEN] = acc[:, PHI_HIDDEN:]


def cmp_part_sample(pool_t, page_table, w1j_bf16, *, pp):
    b, n_pages = page_table.shape
    page = pool_t.shape[-1]
    cpp = page // D_CMP

    def xspec(k):
        return pl.BlockSpec((1, 2, N_KV, HEAD_DIM, page), lambda bi, gi, pt: (pt[bi, gi * pp + k], 0, 0, 0, 0))

    r = np.arange(page)
    perm = np.zeros((page, page), np.float32)
    perm[r, (r % D_CMP) * cpp + r // D_CMP] = 1.0
    grid_spec = pltpu.PrefetchScalarGridSpec(
        num_scalar_prefetch=1,
        grid=(b, n_pages // pp),
        in_specs=[xspec(k) for k in range(pp)]
        + [pl.BlockSpec((page, page), lambda bi, gi, pt: (0, 0)),
           pl.BlockSpec((2, D_CMP, HEAD_DIM, 2 * PHI_HIDDEN), lambda bi, gi, pt: (0, 0, 0, 0))],
        out_specs=pl.BlockSpec((1, cpp * pp, 1024), lambda bi, gi, pt: (bi, gi, 0)),
        scratch_shapes=[pltpu.VMEM((2 * N_KV, D_CMP, pp * cpp, HEAD_DIM), jnp.float32)],
    )
    return pl.pallas_call(
        functools.partial(_cmp_part_paged_kernel, n_in=pp),
        grid_spec=grid_spec,
        out_shape=jax.ShapeDtypeStruct((b, n_pages * cpp, 1024), jnp.float32),
        compiler_params=_cp(("parallel", "arbitrary")),
        name="cmp_part_sample",
    )(page_table, *([pool_t] * pp), jnp.asarray(perm, jnp.bfloat16), w1j_bf16)


def _fold_heads(o_ext):
    row_kvh = lax.broadcasted_iota(jnp.int32, (N_HEADS, 1), 0) // GQA
    out = jnp.zeros((N_HEADS, HEAD_DIM), jnp.float32)
    for kvh in range(N_KV):
        out = out + jnp.where(row_kvh == kvh, o_ext[:, kvh * HEAD_DIM:(kvh + 1) * HEAD_DIM], 0.0)
    return out


def _cmp_attn_sample_kernel(part_ref, pe_ref, w1t_ref, w2bd_ref, q_ref, g8_ref, ov_ref, o_ref, imps_ref, *, q_pos):
    part = part_ref[0]
    n = part.shape[0]
    h = part[:, :512] + pltpu.roll(part[:, 512:], n - 1, axis=0)
    kv = []
    for e in range(2):
        pe = _hdot(pe_ref[e], w1t_ref[e])[0:1]
        he = _silu(h[:, e * 256:(e + 1) * 256] + pe)
        kv.append(_hdot(he, w2bd_ref[e]))
    kc, vc = kv
    s = _dot_nt(q_ref[0], kc, HI) * SCALE
    cidx = lax.broadcasted_iota(jnp.int32, (1, n), 1)
    cmask = (cidx * D_CMP + (L_CMP - 1)) <= q_pos
    s = jnp.where(cmask, s, NEG)
    m = jnp.max(s, axis=-1, keepdims=True)
    ex = jnp.where(cmask, jnp.exp(s - m), 0.0)
    p = ex / jnp.maximum(jnp.sum(ex, axis=-1, keepdims=True), 1e-30)
    o_ref[0] = _fold_heads(_bdot(p, vc))
    imps_ref[0] = _hdot(_hdot(g8_ref[...], p), ov_ref[...])


def cmp_attn_sample(part, pe8, w1t, w2bd, q16ext, g8, ov, *, q_pos):
    b, n, _ = part.shape
    nbp = ov.shape[1]
    c3 = lambda bi: (0, 0, 0)
    return pl.pallas_call(
        functools.partial(_cmp_attn_sample_kernel, q_pos=q_pos),
        grid=(b,),
        in_specs=[pl.BlockSpec((1, n, 1024), lambda bi: (bi, 0, 0)),
                  pl.BlockSpec((2, 8, 2048), c3),
                  pl.BlockSpec((2, 2048, 256), c3),
                  pl.BlockSpec((2, 256, 256), c3),
                  pl.BlockSpec((1, N_HEADS, 256), lambda bi: (bi, 0, 0)),
                  pl.BlockSpec((8, N_HEADS), lambda bi: (0, 0)),
                  pl.BlockSpec((n, nbp), lambda bi: (0, 0))],
        out_specs=[pl.BlockSpec((1, N_HEADS, HEAD_DIM), lambda bi: (bi, 0, 0)),
                   pl.BlockSpec((1, 8, nbp), lambda bi: (bi, 0, 0))],
        out_shape=[jax.ShapeDtypeStruct((b, N_HEADS, HEAD_DIM), jnp.float32),
                   jax.ShapeDtypeStruct((b, 8, nbp), jnp.float32)],
        compiler_params=_cp(("parallel",)),
        name="cmp_attn_sample",
    )(part, pe8, w1t, w2bd, q16ext, g8, ov)


def _topk_kernel(sc_ref, idx_ref, *, n_blk, q_blk, k_sel):
    imps = sc_ref[...]
    nbp = imps.shape[0]
    sidx = lax.broadcasted_iota(jnp.int32, (nbp, 1), 0)
    sf = sidx.astype(jnp.float32)
    forced = (sidx == 0) | (sidx == q_blk) | (sidx == q_blk - 1)
    causal = (sidx <= q_blk) & (sidx < n_blk)
    score = jnp.where(forced, -NEG, jnp.where(causal, imps, NEG))
    rows = []
    for _ in range(k_sel):
        m = jnp.max(score, axis=0, keepdims=True)
        pick = jnp.min(jnp.where(score == m, sf, float(nbp)), axis=0, keepdims=True)
        rows.append(jnp.where(m > 0.5 * NEG, pick, -1.0))
        score = jnp.where(sf == pick, 2.0 * NEG, score)
    idx_ref[...] = jnp.concatenate(rows, axis=0).astype(jnp.int32)


def topk_blocks(scores, *, n_blk, q_blk, k_sel):
    nbp, lanes = scores.shape
    return pl.pallas_call(
        functools.partial(_topk_kernel, n_blk=n_blk, q_blk=q_blk, k_sel=k_sel),
        out_shape=jax.ShapeDtypeStruct((k_sel, lanes), jnp.int32),
        name="topk_blocks",
    )(scores)


def _slc_sample_kernel(idx_ref, pt_ref, *refs, n_sel, new_blk):
    blk_refs = refs[:N_KV * n_sel]
    q_ref, knew_ref, vnew_ref, o_ref = refs[N_KV * n_sel:]
    b = pl.program_id(0)
    page = blk_refs[0].shape[-1]
    lane = lax.broadcasted_iota(jnp.int32, (1, n_sel * page), 1)
    for kvh in range(N_KV):
        kts, vts = [], []
        live = lane < 0
        has_new = False
        for n in range(n_sel):
            blk = blk_refs[kvh * n_sel + n]
            kts.append(blk[0, 0, 0])
            vts.append(blk[0, 1, 0])
            s_n = idx_ref[(b * N_KV + kvh) * n_sel + n]
            lo = n * page + (s_n % (page // L_SLC)) * L_SLC
            live = live | ((lane >= lo) & (lane < lo + L_SLC) & (s_n >= 0) & (s_n != new_blk))
            has_new = jnp.logical_or(has_new, s_n == new_blk)
        kt = jnp.concatenate(kts, axis=1).astype(jnp.bfloat16)
        vt = jnp.concatenate(vts, axis=1).astype(jnp.bfloat16)
        q = q_ref[0, kvh]
        s = jnp.dot(q.astype(jnp.bfloat16), kt, preferred_element_type=jnp.float32) * SCALE
        s = jnp.where(live, s, NEG)
        s_new = jnp.where(has_new, jnp.sum(q * knew_ref[0, kvh], axis=-1, keepdims=True) * SCALE, NEG)
        m = jnp.maximum(jnp.max(s, axis=-1, keepdims=True), s_new)
        ex = jnp.where(live, jnp.exp(s - m), 0.0)
        ex_new = jnp.where(has_new, jnp.exp(s_new - m), 0.0)
        denom = jnp.maximum(jnp.sum(ex, axis=-1, keepdims=True) + ex_new, 1e-30)
        o_ref[0, kvh] = (_dot_nt(ex.astype(jnp.bfloat16), vt) + ex_new * vnew_ref[0, kvh]) / denom


def slc_sample(idx_flat, page_table, pool_t, q4, knew, vnew, *, n_sel, new_blk):
    b, n_pages = page_table.shape
    page = pool_t.shape[-1]
    bpp = page // L_SLC

    def bspec(kvh, n):
        def imap(bi, idx, pt):
            s = jnp.clip(idx[(bi * N_KV + kvh) * n_sel + n], 0, bpp * n_pages - 1)
            return (pt[bi, s // bpp], 0, kvh, 0, 0)
        return pl.BlockSpec((1, 2, 1, HEAD_DIM, page), imap)

    small = lambda r: pl.BlockSpec((1, N_KV, r, HEAD_DIM), lambda bi, idx, pt: (bi, 0, 0, 0))
    grid_spec = pltpu.PrefetchScalarGridSpec(
        num_scalar_prefetch=2,
        grid=(b,),
        in_specs=[bspec(kvh, n) for kvh in range(N_KV) for n in range(n_sel)] + [small(8), small(1), small(1)],
        out_specs=small(8),
    )
    return pl.pallas_call(
        functools.partial(_slc_sample_kernel, n_sel=n_sel, new_blk=new_blk),
        grid_spec=grid_spec,
        out_shape=jax.ShapeDtypeStruct((b, N_KV, 8, HEAD_DIM), jnp.float32),
        compiler_params=_cp(("arbitrary",)),
        name="slc_sample",
    )(idx_flat, page_table, *([pool_t] * (N_KV * n_sel)), q4, knew, vnew)


def _win_sample_kernel(w_ref, q_ref, o_ref):
    w = w_ref[0]
    s = _dot_nt(q_ref[0].astype(jnp.bfloat16), w[:, :KV_W].astype(jnp.bfloat16)) * SCALE
    m = jnp.max(s, axis=-1, keepdims=True)
    ex = jnp.exp(s - m)
    p = ex / jnp.maximum(jnp.sum(ex, axis=-1, keepdims=True), 1e-30)
    o_ref[0] = _fold_heads(_bdot(p, w[:, KV_W:]))


def win_sample(win_rows, q16ext):
    b, nk, _ = win_rows.shape
    return pl.pallas_call(
        _win_sample_kernel,
        grid=(b,),
        in_specs=[pl.BlockSpec((1, nk, 2 * KV_W), lambda bi: (bi, 0, 0)),
                  pl.BlockSpec((1, N_HEADS, KV_W), lambda bi: (bi, 0, 0))],
        out_specs=pl.BlockSpec((1, N_HEADS, HEAD_DIM), lambda bi: (bi, 0, 0)),
        out_shape=jax.ShapeDtypeStruct((b, N_HEADS, HEAD_DIM), jnp.float32),
        compiler_params=_cp(("parallel",)),
        name="win_sample",
    )(win_rows, q16ext)


def _prep_weights(w_in, mu, w2, a2, g2):
    o3 = 3 * D_RWKV
    nsa0 = R_COLS
    cols = [w_in[:, nsa0:nsa0 + D_NSA],
            w_in[:, 0:o3],
            w_in[:, nsa0 + D_NSA:nsa0 + D_NSA + 6 * KV_W],
            w_in[:, o3:R_COLS],
            w_in[:, nsa0 + D_NSA + 6 * KV_W:],
            jnp.zeros((D_MODEL, D_IN_PAD - C_GATE - 3 * N_HEADS), w_in.dtype)]
    w_in_p = jnp.concatenate(cols, axis=1).astype(jnp.bfloat16)
    z = jnp.zeros((LORA_ALL, D_RWKV), jnp.float32)
    return dict(w_in_p=w_in_p, mu_p=mu,
                w2p=z.at[0:LORA_W].set(w2), a2p=z.at[LORA_W:LORA_W + LORA_A].set(a2),
                g2p=z.at[LORA_W + LORA_A:].set(g2))


def _layer_weights(layer, g_mix, w_in, mu, w0, w2, a0, a2, g2, k_k, k_a, r_k, lnx_w, lnx_b, phi_pe, phi_w1,
                   phi_w2, w_o, g_ffn, w_gate, w_up, w_down):
    w = _prep_weights(w_in[layer], mu[layer], w2[layer], a2[layer], g2[layer])
    w1t, pe8, w2k, w2vt, w2bd = _phi_weights(phi_pe[layer], phi_w1[layer], phi_w2[layer])
    w1x = _expand_w1(phi_w1[layer])
    w1j = phi_w1[layer].reshape(2, 2, D_CMP, HEAD_DIM, PHI_HIDDEN).transpose(0, 2, 3, 1, 4)
    w1j = w1j.reshape(2, D_CMP, HEAD_DIM, 2 * PHI_HIDDEN)
    w.update(g_mix=g_mix[layer], w0=w0[layer], a0=a0[layer], k_k=k_k[layer], k_a=k_a[layer], r_k=r_k[layer],
             lnx_w=lnx_w[layer], lnx_b=lnx_b[layer], w1x=w1x, w1j_bf16=w1j.astype(jnp.bfloat16),
             w1t=w1t, pe8=pe8, w2k=w2k, w2vt=w2vt, w2bd=w2bd,
             gate_expand=_gate_expand_matrix(), wo=w_o[layer].astype(jnp.bfloat16), g_ffn=g_ffn[layer],
             wg=w_gate[layer].astype(jnp.bfloat16), wu=w_up[layer].astype(jnp.bfloat16),
             wd=w_down[layer].astype(jnp.bfloat16))
    return w


def _pick(n, pref):
    while n % pref:
        pref //= 2
    return pref


def _prompt_layer(x_prompt, W, g_final):
    B, T, _ = x_prompt.shape
    M = B * T
    x2 = x_prompt.reshape(M, D_MODEL)
    proj = proj_matmul(x2, W["g_mix"], W["w_in_p"], apply_norm=True, tm=_pick(M, 512), tn=1024)
    proj3 = proj.reshape(B, T, D_IN_PAD)
    ro, rg, wkv = rwkv_mix(proj, jnp.zeros((B, 1, R_COLS), jnp.float32), W,
                           jnp.zeros((B, H_R, 64, 64), jnp.float32), b=B, t=T, tm=_pick(T, 256), tc=_pick(T, 64))
    cmp_kv = proj3[..., C_CMP:C_CMP + 2 * KV_W]
    slc_kv = proj3[..., C_SLC:C_SLC + 2 * KV_W]
    win_kv = proj3[..., C_WIN:C_WIN + 2 * KV_W]
    nch = T // D_CMP
    part = cmp_part_prompt(cmp_kv.reshape(B, nch, D_CMP * 2 * KV_W), W["w1x"])
    ovt = _overlap_t(128, nch - 1, nch)
    o_cmp, sel = cmp_select_prompt(part, W["pe8"], W["w1t"], W["w2k"], W["w2vt"], proj3, ovt, tq=128)
    ks, vs = _block_kv(slc_kv, B, T)
    kw, vw = _block_kv(win_kv, B, T)
    o_slc, o_win = slc_win_prompt(proj3, ks, vs, kw, vw, sel, tq=128)
    r2 = lambda z: z.reshape(M, 1024)
    x1 = out_proj(r2(ro), r2(rg), r2(o_cmp), r2(o_slc), r2(o_win), proj, W["gate_expand"], x2, W["wo"],
                  tm=_pick(M, 512), tn=1024)
    y = ffn_final(x1, W["g_ffn"], W["wg"], W["wu"], W["wd"], g_final, tm=_pick(M, 512), tf=512)
    kv5 = lambda z: z.reshape(B, T, 2, N_KV, HEAD_DIM)
    nwin = min(WINDOW, T)
    return y.reshape(B, T, D_MODEL), kv5(cmp_kv), kv5(slc_kv), kv5(win_kv)[:, T - nwin:], wkv


def _sample_layer(x_sample, xn_s, cache_cmp, cache_slc, cache_win, wkv0, shift, page_table, W, g_final):
    b = x_sample.shape[0]
    n_pool, page = cache_cmp.shape[0], cache_cmp.shape[1]
    n_pages = page_table.shape[1]
    past_len = n_pages * page
    n_rows = -(-(past_len + 1) // L_SLC) * L_SLC
    n_blk = n_rows // L_SLC
    q_blk = past_len // L_SLC
    nbp = -(-n_blk // 128) * 128

    rows = jnp.concatenate([xn_s, shift], axis=0)
    proj = proj_matmul(rows, W["g_mix"], W["w_in_p"], apply_norm=False, tm=2 * b, tn=1024)
    prev = jnp.concatenate([proj[b:, C_R:C_R + 3 * D_RWKV], proj[b:, C_LORA:C_LORA + LORA_ALL]], axis=1)
    ro, rg, wkv = rwkv_mix(proj[:b], prev, W, wkv0, b=b, t=1, tm=1, tc=1)

    q = proj[:b, C_Q:C_Q + D_NSA].reshape(b, N_KV, GQA, 1, HEAD_DIM)
    eye = jnp.eye(N_KV, dtype=jnp.float32).reshape(1, N_KV, 1, N_KV, 1)
    qext = (q * eye).reshape(b, N_KV, GQA, KV_W)
    q16ext = qext.reshape(b, N_HEADS, KV_W)
    cmp_new = proj[:b, C_CMP:C_CMP + 2 * KV_W]
    slc_new = proj[:b, C_SLC:C_SLC + 2 * KV_W]
    win_new = proj[:b, C_WIN:C_WIN + 2 * KV_W]

    part = cmp_part_sample(cache_cmp.transpose(0, 2, 3, 4, 1), page_table, W["w1j_bf16"], pp=_pick(n_pages, 32))
    nc_rows = part.shape[1]
    ov = _overlap_t(nbp, n_rows // D_CMP - L_CMP // D_CMP + 1, nc_rows)[:n_blk].T
    ov = jnp.pad(ov, ((0, 0), (0, nbp - n_blk)))
    g8 = jnp.asarray((np.arange(8)[:, None] == np.arange(N_HEADS)[None, :] // GQA).astype(np.float32))
    o_cmp, imps = cmp_attn_sample(part, W["pe8"], W["w1t"], W["w2bd"], q16ext, g8, ov, q_pos=past_len)
    scores = imps[:, :N_KV].transpose(2, 0, 1).reshape(nbp, b * N_KV)
    k_sel = min(N_SELECT, n_blk)
    idx = topk_blocks(scores, n_blk=n_blk, q_blk=q_blk, k_sel=k_sel)
    q4 = jnp.pad(q.reshape(b, N_KV, GQA, HEAD_DIM), ((0, 0), (0, 0), (0, 8 - GQA), (0, 0)))
    slc_new5 = slc_new.reshape(b, 2, N_KV, 1, HEAD_DIM)
    o_slc = slc_sample(idx.T.reshape(-1), page_table, cache_slc.transpose(0, 2, 3, 4, 1), q4,
                       slc_new5[:, 0], slc_new5[:, 1], n_sel=k_sel, new_blk=q_blk)
    o_slc = o_slc[:, :, :GQA].reshape(b, D_NSA)

    n_buf = cache_win.shape[1]
    win_all = jnp.concatenate([cache_win, win_new.reshape(b, 1, 2, N_KV, HEAD_DIM)], axis=1)
    win_keep = win_all[:, n_buf + 1 - min(WINDOW, n_buf + 1):]
    o_win = win_sample(win_keep.reshape(b, -1, 2 * KV_W), q16ext)

    x2 = x_sample.reshape(b, D_MODEL)
    x1 = out_proj(ro.reshape(b, -1), rg.reshape(b, -1), o_cmp.reshape(b, D_NSA), o_slc, o_win.reshape(b, D_NSA),
                  proj, W["gate_expand"], x2, W["wo"], tm=b, tn=1024)
    y = ffn_final(x1, W["g_ffn"], W["wg"], W["wu"], W["wd"], g_final, tm=b, tf=512)
    kv5 = lambda z: z.reshape(b, 1, 2, N_KV, HEAD_DIM)
    return y.reshape(b, 1, D_MODEL), kv5(cmp_new), kv5(slc_new), win_keep, wkv


def kernel(x_prompt, x_sample, cache_cmp_kv, cache_slc_kv, cache_win_kv, state_wkv, state_shift, page_table, g_mix, w_in, mu, w0, w2, a0, a2, g2, k_k, k_a, r_k, lnx_w, lnx_b, phi_pe, phi_w1, phi_w2, w_o, g_ffn, w_gate, w_up, w_down, g_final):
    W = _layer_weights(0, g_mix, w_in, mu, w0, w2, a0, a2, g2, k_k, k_a, r_k, lnx_w, lnx_b, phi_pe, phi_w1,
                       phi_w2, w_o, g_ffn, w_gate, w_up, w_down)
    bp, bs = x_prompt.shape[0], x_sample.shape[0]
    small = jnp.concatenate([x_sample.reshape(bs, D_MODEL), x_prompt[:, -1]], axis=0)
    pad = (-small.shape[0]) % 8
    xn_small = rmsnorm_rows(jnp.pad(small, ((0, pad), (0, 0))), W["g_mix"])
    xn_s, shift_p = xn_small[:bs], xn_small[bs:bs + bp]

    y_p, cmp_p, slc_p, win_p, wkv_p = _prompt_layer(x_prompt, W, g_final)
    y_s, cmp_s, slc_s, win_s, wkv_s = _sample_layer(x_sample, xn_s, cache_cmp_kv[0], cache_slc_kv[0],
                                                    cache_win_kv[0], state_wkv[0], state_shift[0], page_table,
                                                    W, g_final)
    return (y_p, y_s, cmp_p[None], slc_p[None], win_p[None], wkv_p[None], shift_p[None],
            cmp_s[None], slc_s[None], win_s[None], wkv_s[None], xn_s[None])
```

```python
import functools

import jax
import jax.numpy as jnp
import numpy as np
from jax import lax
from jax.experimental import pallas as pl
from jax.experimental.pallas import tpu as pltpu

D_MODEL = 2048
D_RWKV = 1024
D_NSA = 1024
HEAD_DIM_R = 64
H_R = 16
LORA_W = 96
LORA_A = 96
LORA_G = 64
LORA_ALL = LORA_W + LORA_A + LORA_G
HEAD_DIM = 64
N_HEADS = 16
N_KV = 4
GQA = 4
KV_W = 256
L_CMP = 32
D_CMP = 16
L_SLC = 64
N_SELECT = 16
WINDOW = 512
PHI_HIDDEN = 64
R_COLS = 3 * D_RWKV + LORA_ALL
RMS_EPS = 1e-6
LNX_EPS = 64e-5
SCALE = HEAD_DIM ** -0.5
NEG = -1e30

C_Q, C_R, C_K, C_V = 0, 1024, 2048, 3072
C_CMP, C_SLC, C_WIN = 4096, 4608, 5120
C_LORA, C_GATE = 5632, 5888
D_IN_PAD = 6144

VMEM_LIMIT = 48 * 1024 * 1024
HI = lax.Precision.HIGHEST


def _cp(sem, limit=VMEM_LIMIT):
    return pltpu.CompilerParams(dimension_semantics=sem, vmem_limit_bytes=limit)


def _bdot(a, b):
    return jnp.dot(a.astype(jnp.bfloat16), b.astype(jnp.bfloat16), preferred_element_type=jnp.float32)


def _hdot(a, b):
    return jnp.dot(a, b, precision=HI, preferred_element_type=jnp.float32)


def _dot_nt(a, b, precision=None):
    return lax.dot_general(a, b, (((1,), (1,)), ((), ())), precision=precision,
                           preferred_element_type=jnp.float32)


def _sigmoid(x):
    return 1.0 / (1.0 + jnp.exp(-x))


def _silu(x):
    return x * _sigmoid(x)


def _rmsnorm_rows_kernel(x_ref, g_ref, o_ref):
    x = x_ref[...]
    ms = jnp.mean(x * x, axis=-1, keepdims=True)
    o_ref[...] = x * lax.rsqrt(ms + RMS_EPS) * g_ref[...]


def rmsnorm_rows(x, g):
    m, d = x.shape
    return pl.pallas_call(
        _rmsnorm_rows_kernel,
        out_shape=jax.ShapeDtypeStruct((m, d), jnp.float32),
        name="rmsnorm_rows",
    )(x, g.reshape(1, d))


def _proj_kernel(x_ref, g_ref, w_ref, o_ref, xn_ref, *, apply_norm):
    @pl.when(pl.program_id(1) == 0)
    def _():
        x = x_ref[...]
        if apply_norm:
            ms = jnp.mean(x * x, axis=-1, keepdims=True)
            x = x * lax.rsqrt(ms + RMS_EPS) * g_ref[...]
        xn_ref[...] = x.astype(jnp.bfloat16)

    o_ref[...] = jnp.dot(xn_ref[...], w_ref[...], preferred_element_type=jnp.float32)


def proj_matmul(x, g, w_bf16, *, apply_norm, tm, tn):
    m, d = x.shape
    n = w_bf16.shape[1]
    return pl.pallas_call(
        functools.partial(_proj_kernel, apply_norm=apply_norm),
        grid=(m // tm, n // tn),
        in_specs=[
            pl.BlockSpec((tm, d), lambda i, j: (i, 0)),
            pl.BlockSpec((1, d), lambda i, j: (0, 0)),
            pl.BlockSpec((d, tn), lambda i, j: (0, j)),
        ],
        out_specs=pl.BlockSpec((tm, tn), lambda i, j: (i, j)),
        out_shape=jax.ShapeDtypeStruct((m, n), jnp.float32),
        scratch_shapes=[pltpu.VMEM((tm, d), jnp.bfloat16)],
        compiler_params=_cp(("parallel", "arbitrary")),
        name="proj_matmul",
    )(x, g.reshape(1, d), w_bf16)


def _rwkv_prep_kernel(pr_ref, pk_ref, pv_ref, pl_ref, prev_ref, mu_ref, w0_ref, a0_ref, kk_ref, ka_ref,
                      w2_ref, a2_ref, g2_ref,
                      r_out, w_out, k_out, v_out, kk_out, a_out, g_out, carry_ref, *, row_prev):
    tm = pr_ref.shape[1]
    pr, pk, pv, plo = pr_ref[0], pk_ref[0], pv_ref[0], pl_ref[0]
    if row_prev:
        def shift_mix(p, lo, hi):
            return p + (prev_ref[0, :, lo:hi] - p) * mu_ref[:, lo:hi]
    else:
        @pl.when(pl.program_id(1) == 0)
        def _():
            carry_ref[...] = prev_ref[0]

        row0 = lax.broadcasted_iota(jnp.int32, (tm, 1), 0) == 0

        def shift_mix(p, lo, hi):
            prev = pltpu.roll(p, 1, axis=0)
            prev = jnp.where(row0, carry_ref[:, lo:hi], prev)
            return p + (prev - p) * mu_ref[:, lo:hi]

    xr = shift_mix(pr, 0, 1024)
    xk = shift_mix(pk, 1024, 2048)
    xv = shift_mix(pv, 2048, 3072)
    xl = shift_mix(plo, 3072, 3328)
    if not row_prev:
        last = tm - 1
        carry_ref[:, 0:1024] = pr[last:last + 1]
        carry_ref[:, 1024:2048] = pk[last:last + 1]
        carry_ref[:, 2048:3072] = pv[last:last + 1]
        carry_ref[:, 3072:3328] = plo[last:last + 1]

    wl = _hdot(jnp.tanh(xl), w2_ref[...])
    al = _hdot(xl, a2_ref[...])
    gl = _hdot(_sigmoid(xl), g2_ref[...])
    z = -(w0_ref[...] + wl)
    softplus = jnp.maximum(z, 0.0) + jnp.log(1.0 + jnp.exp(-jnp.abs(z)))
    w_log = -softplus - 0.5
    decay = jnp.exp(-jnp.exp(w_log))
    a = _sigmoid(a0_ref[...] + al)
    r_out[0] = xr
    w_out[0] = decay
    k_out[0] = xk * (1.0 + (a - 1.0) * ka_ref[...])
    v_out[0] = xv
    kk_out[0] = xk * kk_ref[...]
    a_out[0] = a
    g_out[0] = gl


def rwkv_prep(proj, prev, mu_p, w0, a0, k_k, k_a, w2p, a2p, g2p, *, b, t, tm, row_prev=False):
    proj3 = proj.reshape(b, t, D_IN_PAD)
    nt = t // tm
    prev_spec = (pl.BlockSpec((1, tm, R_COLS), lambda bi, ti: (bi, ti, 0)) if row_prev
                 else pl.BlockSpec((1, 1, R_COLS), lambda bi, ti: (bi, 0, 0)))
    row = lambda blk, w: pl.BlockSpec((1, tm, w), lambda bi, ti: (bi, ti, blk))
    vec = lambda w: pl.BlockSpec((1, w), lambda bi, ti: (0, 0))
    mat = pl.BlockSpec((LORA_ALL, D_RWKV), lambda bi, ti: (0, 0))
    out = pl.BlockSpec((1, tm, D_RWKV), lambda bi, ti: (bi, ti, 0))
    shp = jax.ShapeDtypeStruct((b, t, D_RWKV), jnp.float32)
    return pl.pallas_call(
        functools.partial(_rwkv_prep_kernel, row_prev=row_prev),
        grid=(b, nt),
        in_specs=[row(C_R // 1024, 1024), row(C_K // 1024, 1024), row(C_V // 1024, 1024),
                  row(C_LORA // 256, 256),
                  prev_spec,
                  vec(R_COLS), vec(D_RWKV), vec(D_RWKV), vec(D_RWKV), vec(D_RWKV),
                  mat, mat, mat],
        out_specs=[out] * 7,
        out_shape=[shp] * 7,
        scratch_shapes=[pltpu.VMEM((1, R_COLS), jnp.float32)],
        compiler_params=_cp(("parallel", "arbitrary")),
        name="rwkv_prep",
    )(proj3, proj3, proj3, proj3, prev, mu_p.reshape(1, -1), w0.reshape(1, -1), a0.reshape(1, -1),
      k_k.reshape(1, -1), k_a.reshape(1, -1), w2p, a2p, g2p)


def _rwkv_scan_kernel(r_ref, w_ref, k_ref, kk_ref, a_ref, v_ref, s0_ref, rk_ref, lnw_ref, lnb_ref,
                      o_ref, sfin_ref, s_ref, kkn_ref, b_ref, y_ref, *, halved):
    tc = r_ref.shape[1]
    ni = v_ref.shape[2]

    @pl.when(pl.program_id(1) == 0)
    def _():
        s_ref[...] = s0_ref[0]

    kkraw = kk_ref[0]
    nrm = jnp.sqrt(jnp.sum(kkraw * kkraw, axis=1, keepdims=True))
    kkn = kkraw / jnp.maximum(nrm, 1e-12)
    kkn_ref[...] = kkn
    b_ref[...] = kkn * a_ref[0]

    def step(t, carry):
        w_t = w_ref[0, t]
        k_t = k_ref[0, t]
        r_t = r_ref[0, t]
        kk_t = kkn_ref[t]
        b_t = b_ref[t]
        for i in range(ni):
            s = s_ref[i]
            sa = -jnp.sum(s * kk_t, axis=0, keepdims=True)
            v_i = v_ref[0, t, pl.ds(i, 1), :]
            s = s * w_t + sa * b_t + v_i * k_t
            y_ref[t, pl.ds(i, 1), :] = jnp.sum(s * r_t, axis=0, keepdims=True)
            s_ref[i] = s
        return carry

    lax.fori_loop(0, tc, step, 0)

    y = y_ref[...]
    v = v_ref[0]

    def head_sum(z):
        s = jnp.sum(z, axis=1, keepdims=True)
        if halved:
            s = s + pltpu.roll(s, 64, axis=2)
        return s

    mean = head_sum(y) * (1.0 / HEAD_DIM_R)
    d = y - mean
    var = head_sum(d * d) * (1.0 / HEAD_DIM_R)
    yn = d * lax.rsqrt(var + LNX_EPS) * lnw_ref[0] + lnb_ref[0]
    bonus = jnp.sum(r_ref[0] * k_ref[0] * rk_ref[0], axis=1, keepdims=True)
    o_ref[0] = yn + bonus * v

    @pl.when(pl.program_id(1) == pl.num_programs(1) - 1)
    def _():
        sfin_ref[0] = s_ref[...]


def rwkv_scan(r, w, k, kk, a, v, s0, rk, lnw, lnb, *, tc, halved):
    g, t = r.shape[0], r.shape[1]
    ni = v.shape[2]
    col = pl.BlockSpec((1, tc, 64, 128), lambda gi, ti: (gi, ti, 0, 0))
    vspec = pl.BlockSpec((1, tc, ni, 128), lambda gi, ti: (gi, ti, 0, 0))
    sspec = pl.BlockSpec((1, ni, 64, 128), lambda gi, ti: (gi, 0, 0, 0))
    return pl.pallas_call(
        functools.partial(_rwkv_scan_kernel, halved=halved),
        grid=(g, t // tc),
        in_specs=[col, col, col, col, col, vspec, sspec,
                  pl.BlockSpec((1, 64, 128), lambda gi, ti: (gi, 0, 0)),
                  pl.BlockSpec((1, ni, 128), lambda gi, ti: (gi, 0, 0)),
                  pl.BlockSpec((1, ni, 128), lambda gi, ti: (gi, 0, 0))],
        out_specs=[vspec, sspec],
        out_shape=[jax.ShapeDtypeStruct((g, t, ni, 128), jnp.float32),
                   jax.ShapeDtypeStruct((g, ni, 64, 128), jnp.float32)],
        scratch_shapes=[pltpu.VMEM((ni, 64, 128), jnp.float32),
                        pltpu.VMEM((tc, 64, 128), jnp.float32),
                        pltpu.VMEM((tc, 64, 128), jnp.float32),
                        pltpu.VMEM((tc, ni, 128), jnp.float32)],
        compiler_params=_cp(("parallel", "arbitrary")),
        name="rwkv_scan",
    )(r, w, k, kk, a, v, s0, rk, lnw, lnb)


def _to_scan_layout(x, b, t, halved):
    z = x.reshape(b, t, H_R, 64).transpose(1, 3, 0, 2).reshape(t, 64, b * H_R)
    if halved:
        return jnp.concatenate([z, z], axis=-1)[None]
    g = (b * H_R) // 128
    return z.reshape(t, 64, g, 128).transpose(2, 0, 1, 3)


def _v_to_scan_layout(x, b, t, halved):
    if halved:
        return x.reshape(b, t, H_R, 2, 32).transpose(1, 4, 3, 0, 2).reshape(1, t, 32, 128)
    g = (b * H_R) // 128
    z = x.reshape(b, t, H_R, 64).transpose(1, 3, 0, 2).reshape(t, 64, g, 128)
    return z.transpose(2, 0, 1, 3)


def _vecparam_scan_layout(p, b, halved):
    z = jnp.broadcast_to(p.reshape(1, H_R, 64), (b, H_R, 64))
    if halved:
        return z.reshape(b, H_R, 2, 32).transpose(3, 2, 0, 1).reshape(1, 32, 128)
    g = (b * H_R) // 128
    return z.transpose(2, 0, 1).reshape(64, g, 128).transpose(1, 0, 2)


def _keyparam_scan_layout(p, b, halved):
    z = jnp.broadcast_to(p.reshape(1, H_R, 64), (b, H_R, 64)).transpose(2, 0, 1).reshape(64, b * H_R)
    if halved:
        return jnp.concatenate([z, z], axis=-1)[None]
    g = (b * H_R) // 128
    return z.reshape(64, g, 128).transpose(1, 0, 2)


def _from_scan_layout(o, b, t, halved):
    if halved:
        return o.reshape(t, 32, 2, b, H_R).transpose(3, 0, 4, 2, 1).reshape(b, t, D_RWKV)
    g = o.shape[0]
    return o.transpose(1, 2, 0, 3).reshape(t, 64, b, H_R).transpose(2, 0, 3, 1).reshape(b, t, D_RWKV)


def _state_to_scan_layout(s, b, halved):
    if halved:
        return s.reshape(b, H_R, 2, 32, 64).transpose(3, 4, 2, 0, 1).reshape(1, 32, 64, 128)
    g = (b * H_R) // 128
    return s.reshape(g, 128, 64, 64).transpose(0, 2, 3, 1)


def _state_from_scan_layout(s, b, halved):
    if halved:
        return s.reshape(32, 64, 2, b, H_R).transpose(3, 4, 2, 0, 1).reshape(b, H_R, 64, 64)
    return s.transpose(0, 3, 1, 2).reshape(b, H_R, 64, 64)


def rwkv_mix(proj, prev, wts, s0, *, b, t, tm, tc):
    halved = (b * H_R == 64)
    if t == 1:
        outs = rwkv_prep(proj, prev.reshape(1, b, R_COLS), wts["mu_p"], wts["w0"], wts["a0"], wts["k_k"],
                         wts["k_a"], wts["w2p"], wts["a2p"], wts["g2p"], b=1, t=b, tm=b, row_prev=True)
        r, w, k, v, kk, a, g = [z.reshape(b, 1, D_RWKV) for z in outs]
    else:
        r, w, k, v, kk, a, g = rwkv_prep(proj, prev, wts["mu_p"], wts["w0"], wts["a0"], wts["k_k"], wts["k_a"],
                                         wts["w2p"], wts["a2p"], wts["g2p"], b=b, t=t, tm=tm)
    tr = lambda z: _to_scan_layout(z, b, t, halved)
    o, sfin = rwkv_scan(tr(r), tr(w), tr(k), tr(kk), tr(a), _v_to_scan_layout(v, b, t, halved),
                        _state_to_scan_layout(s0, b, halved),
                        _keyparam_scan_layout(wts["r_k"], b, halved),
                        _vecparam_scan_layout(wts["lnx_w"], b, halved),
                        _vecparam_scan_layout(wts["lnx_b"], b, halved),
                        tc=tc, halved=halved)
    return _from_scan_layout(o, b, t, halved), g, _state_from_scan_layout(sfin, b, halved)


def _cmp_part_kernel(*refs, n_in, precise, prefetch):
    refs = refs[prefetch:]
    x_refs, w_ref, o_ref = refs[:n_in], refs[n_in], refs[n_in + 1]
    for e in range(2):
        for hp in range(2):
            acc = None
            for j in range(D_CMP):
                lo = j * 512 + e * 256 + hp * 128
                xs = [xr[0, :, lo:lo + 128] for xr in x_refs]
                x = xs[0] if n_in == 1 else jnp.concatenate(xs, axis=0)
                d = _hdot(x, w_ref[e, j]) if precise else _bdot(x, w_ref[e, j])
                acc = d if acc is None else acc + d
            c0 = e * 256 + hp * 128
            o_ref[0, :, c0:c0 + 128] = acc[:, :128]
            o_ref[0, :, 512 + c0:512 + c0 + 128] = acc[:, 128:]


def cmp_part_prompt(rows, w1x):
    b, nch, _ = rows.shape
    return pl.pallas_call(
        functools.partial(_cmp_part_kernel, n_in=1, precise=True, prefetch=0),
        grid=(b,),
        in_specs=[pl.BlockSpec((1, nch, 8192), lambda bi: (bi, 0, 0)),
                  pl.BlockSpec((2, D_CMP, 128, 256), lambda bi: (0, 0, 0, 0))],
        out_specs=pl.BlockSpec((1, nch, 1024), lambda bi: (bi, 0, 0)),
        out_shape=jax.ShapeDtypeStruct((b, nch, 1024), jnp.float32),
        compiler_params=_cp(("parallel",)),
        name="cmp_part_prompt",
    )(rows, w1x)


def _expand_w1(phi_w1):
    w1 = phi_w1.reshape(2, 2, D_CMP, HEAD_DIM, PHI_HIDDEN)
    eye = jnp.eye(2, dtype=phi_w1.dtype)
    w = jnp.einsum("ab,esjdf->ejadsbf", eye, w1)
    return w.reshape(2, D_CMP, 128, 256)


def _split2(x):
    hi = x.astype(jnp.bfloat16)
    return hi, (x - hi.astype(jnp.float32)).astype(jnp.bfloat16)


def _dot3(a, b):
    (ah, al), (bh, bl) = a, b
    d = lambda x, y: jnp.dot(x, y, preferred_element_type=jnp.float32)
    return d(ah, bh) + (d(ah, bl) + d(al, bh))


def _pe_term_kernel(pe_ref, w1t_ref, o_ref):
    for e in range(2):
        o_ref[e] = _hdot(pe_ref[e], w1t_ref[e])


def pe_term(pe8, w1t):
    return pl.pallas_call(
        _pe_term_kernel,
        out_shape=jax.ShapeDtypeStruct((2, 8, N_KV * PHI_HIDDEN), jnp.float32),
        name="pe_term",
    )(pe8, w1t)


def _cmp_hidden(part, pe_ref, e):
    n = part.shape[0]
    lo = e * 256
    h = part[:, lo:lo + 256] + pltpu.roll(part[:, 512 + lo:512 + lo + 256], n - 1, axis=0)
    return _silu(h + pe_ref[e, 0:1])


def _cmp_sel_kernel(part_ref, pe_ref, w2k_ref, w2vt_ref, q_ref, ovt_ref, o_ref, sel_ref,
                    kch_ref, kcl_ref, vct_ref, *, tq, n_blk):
    ti = pl.program_id(1)

    @pl.when(ti == 0)
    def _():
        part = part_ref[0]
        hk = _split2(_cmp_hidden(part, pe_ref, 0))
        hv = _cmp_hidden(part, pe_ref, 1)
        for kvh in range(N_KV):
            kch_ref[kvh], kcl_ref[kvh] = _split2(_dot3(hk, _split2(w2k_ref[kvh])))
            vct_ref[kvh] = _dot_nt(w2vt_ref[kvh], hv, HI).astype(jnp.bfloat16)

    nc = kch_ref.shape[1]
    nbp = -(-n_blk // 8) * 8
    k_sel = min(N_SELECT, n_blk)
    qpos = ti * tq + lax.broadcasted_iota(jnp.int32, (1, tq), 1)
    cidx = lax.broadcasted_iota(jnp.int32, (nc, 1), 0)
    cmask = (cidx * D_CMP + (L_CMP - 1)) <= qpos
    sidx = cidx[:nbp]
    qblk = qpos // L_SLC
    forced = (sidx == 0) | (sidx == qblk) | (sidx == qblk - 1)
    causal = sidx <= qblk
    pad = jnp.zeros((128 - nbp, tq), jnp.float32)
    need_rank = (ti + 1) * tq > k_sel * L_SLC

    for kvh in range(N_KV):
        qt = _split2(q_ref[0, :, kvh * 256:(kvh + 1) * 256].T)
        kc = (kch_ref[kvh], kcl_ref[kvh])
        impt = jnp.zeros((nc, tq), jnp.float32)
        parts = []
        for g in range(GQA):
            qg = (qt[0][g * 64:(g + 1) * 64], qt[1][g * 64:(g + 1) * 64])
            st = _dot3(kc, qg) * SCALE
            st = jnp.where(cmask, st, NEG)
            m = jnp.max(st, axis=0, keepdims=True)
            ex = jnp.where(cmask, jnp.exp(st - m), 0.0)
            pt = ex / jnp.maximum(jnp.sum(ex, axis=0, keepdims=True), 1e-30)
            impt = impt + pt
            parts.append(jnp.dot(vct_ref[kvh], pt.astype(jnp.bfloat16), preferred_element_type=jnp.float32))
        o_ref[0, :, kvh * 256:(kvh + 1) * 256] = jnp.concatenate(parts, axis=0).T

        @pl.when(need_rank)
        def _():
            ih, il = _split2(impt)
            ov = ovt_ref[...]
            imps = (jnp.dot(ov, ih, preferred_element_type=jnp.float32)
                    + jnp.dot(ov, il, preferred_element_type=jnp.float32))[:nbp]
            score = jnp.where(forced, -NEG, jnp.where(causal, imps, NEG))
            rank = jnp.zeros((nbp, tq), jnp.float32)
            for s2 in range(n_blk):
                row = score[s2:s2 + 1]
                beats = (row > score) | ((row == score) & (s2 < sidx))
                rank = rank + beats.astype(jnp.float32)
            seln = ((rank < k_sel) & causal).astype(jnp.float32)
            sel_ref[0, kvh] = jnp.concatenate([seln, pad], axis=0)

        @pl.when(jnp.logical_not(need_rank))
        def _():
            sel_ref[0, kvh] = jnp.concatenate([causal.astype(jnp.float32), pad], axis=0)


def cmp_select_prompt(part, pe, w2k, w2vt, proj3, ovt_bf16, *, tq):
    b, nc, _ = part.shape
    t = proj3.shape[1]
    c3 = lambda bi, ti: (0, 0, 0)
    return pl.pallas_call(
        functools.partial(_cmp_sel_kernel, tq=tq, n_blk=t // L_SLC),
        grid=(b, t // tq),
        in_specs=[pl.BlockSpec((1, nc, 1024), lambda bi, ti: (bi, 0, 0)),
                  pl.BlockSpec((2, 8, 256), c3),
                  pl.BlockSpec((N_KV, 256, 64), c3),
                  pl.BlockSpec((N_KV, 64, 256), c3),
                  pl.BlockSpec((1, tq, 1024), lambda bi, ti: (bi, ti, C_Q // 1024)),
                  pl.BlockSpec((128, nc), lambda bi, ti: (0, 0))],
        out_specs=[pl.BlockSpec((1, tq, 1024), lambda bi, ti: (bi, ti, 0)),
                   pl.BlockSpec((1, N_KV, 128, tq), lambda bi, ti: (bi, 0, 0, ti))],
        out_shape=[jax.ShapeDtypeStruct((b, t, 1024), jnp.float32),
                   jax.ShapeDtypeStruct((b, N_KV, 128, t), jnp.float32)],
        scratch_shapes=[pltpu.VMEM((N_KV, nc, 64), jnp.bfloat16),
                        pltpu.VMEM((N_KV, nc, 64), jnp.bfloat16),
                        pltpu.VMEM((N_KV, 64, nc), jnp.bfloat16)],
        compiler_params=_cp(("parallel", "arbitrary")),
        name="cmp_select_prompt",
    )(part, pe, w2k, w2vt, proj3, ovt_bf16)


def _overlap_t(n_blocks_pad, nc, nc_pad):
    c_start = np.arange(nc_pad) * D_CMP
    s_start = np.arange(n_blocks_pad) * L_SLC
    ov = (c_start[None, :] < s_start[:, None] + L_SLC) & (c_start[None, :] + L_CMP > s_start[:, None])
    ov &= (np.arange(nc_pad) < nc)[None, :]
    return jnp.asarray(ov.astype(np.float32))


def _phi_weights(phi_pe, phi_w1, phi_w2):
    w1t = jnp.tile(phi_w1, (1, 1, N_KV))
    pe8 = jnp.broadcast_to(phi_pe.reshape(2, 1, L_CMP * HEAD_DIM), (2, 8, L_CMP * HEAD_DIM))
    z = jnp.zeros((N_KV, N_KV, PHI_HIDDEN, HEAD_DIM), jnp.float32)
    idx = jnp.arange(N_KV)
    w2k = z.at[idx, idx].set(phi_w2[0]).reshape(N_KV, 256, 64)
    w2vt = jnp.transpose(z.at[idx, idx].set(phi_w2[1]), (0, 3, 1, 2)).reshape(N_KV, 64, 256)
    eye = jnp.eye(N_KV, dtype=jnp.float32)
    w2bd = jnp.einsum("ab,efd->eafbd", eye, phi_w2).reshape(2, 256, 256)
    return w1t, pe8, w2k, w2vt, w2bd


def _attend_t(k, vt, qt4, bias, tq):
    s = jnp.dot(k, qt4, preferred_element_type=jnp.float32) + jnp.concatenate([bias] * GQA, axis=1)
    m = jnp.max(s, axis=0, keepdims=True)
    p = jnp.exp(s - m)
    l = jnp.sum(p, axis=0, keepdims=True)
    o = jnp.dot(vt, p.astype(jnp.bfloat16), preferred_element_type=jnp.float32) / jnp.maximum(l, 1e-30)
    return jnp.concatenate([o[:, g * tq:(g + 1) * tq] for g in range(GQA)], axis=0).T


SLC_KEY_STEP = 512
WIN_BLOCKS = WINDOW // 128 + 1


def _slc_win_kernel(q_ref, ks_ref, vs_ref, kw_ref, vw_ref, sel_ref, ex_ref, os_ref, ow_ref, *, tq):
    ti = pl.program_id(2)
    t = ks_ref.shape[2]
    qt = (q_ref[0] * SCALE).T.astype(jnp.bfloat16)
    qt4 = jnp.concatenate([qt[g * HEAD_DIM:(g + 1) * HEAD_DIM] for g in range(GQA)], axis=1)
    qpos = ti * tq + lax.broadcasted_iota(jnp.int32, (1, tq), 1)
    selt = sel_ref[0, 0].astype(jnp.bfloat16)

    n_var = -(-t // SLC_KEY_STEP)
    for c in range(n_var):
        nk = min((c + 1) * SLC_KEY_STEP, t)

        @pl.when((ti * tq) // SLC_KEY_STEP == c)
        def _():
            on = jnp.dot(ex_ref[0:nk], selt, preferred_element_type=jnp.float32)
            kpos = lax.broadcasted_iota(jnp.int32, (nk, 1), 0)
            bias = (on - 1.0) * (-NEG) + jnp.where(kpos <= qpos, 0.0, NEG)
            os_ref[0] = _attend_t(ks_ref[0, 0, 0:nk], vs_ref[0, 0, :, 0:nk], qt4, bias, tq)

    nkb = kw_ref.shape[2]
    w0 = jnp.clip(ti - (WIN_BLOCKS - 1), 0, nkb - WIN_BLOCKS)
    kw = kw_ref[0, 0, pl.ds(w0, WIN_BLOCKS)].reshape(WIN_BLOCKS * 128, HEAD_DIM)
    vwt = jnp.concatenate([vw_ref[0, 0, w0 + i] for i in range(WIN_BLOCKS)], axis=1)
    diff = qpos - (w0 * 128 + lax.broadcasted_iota(jnp.int32, (WIN_BLOCKS * 128, 1), 0))
    ow_ref[0] = _attend_t(kw, vwt, qt4, jnp.where((diff >= 0) & (diff < WINDOW), 0.0, NEG), tq)


def slc_win_prompt(proj3, ks, vst, kw, vwt, selt, *, tq):
    b, t, _ = proj3.shape
    nkb = t // 128
    expand = jnp.asarray((np.arange(t)[:, None] // L_SLC == np.arange(128)[None, :]).astype(np.float32),
                         jnp.bfloat16)
    i4 = lambda bi, hi, ti: (bi, hi, 0, 0)
    i5 = lambda bi, hi, ti: (bi, hi, 0, 0, 0)
    out = pl.BlockSpec((1, tq, 256), lambda bi, hi, ti: (bi, ti, hi))
    shp = jax.ShapeDtypeStruct((b, t, 1024), jnp.float32)
    return pl.pallas_call(
        functools.partial(_slc_win_kernel, tq=tq),
        grid=(b, N_KV, t // tq),
        in_specs=[pl.BlockSpec((1, tq, 256), lambda bi, hi, ti: (bi, ti, hi)),
                  pl.BlockSpec((1, 1, t, HEAD_DIM), i4),
                  pl.BlockSpec((1, 1, HEAD_DIM, t), i4),
                  pl.BlockSpec((1, 1, nkb, 128, HEAD_DIM), i5),
                  pl.BlockSpec((1, 1, nkb, HEAD_DIM, 128), i5),
                  pl.BlockSpec((1, 1, 128, tq), lambda bi, hi, ti: (bi, hi, 0, ti)),
                  pl.BlockSpec((t, 128), lambda bi, hi, ti: (0, 0))],
        out_specs=[out, out],
        out_shape=[shp, shp],
        compiler_params=_cp(("parallel", "parallel", "arbitrary")),
        name="slc_win_prompt",
    )(proj3, ks, vst, kw, vwt, selt, expand)


def _kv_heads(kv_rows, b, t):
    kv = kv_rows.reshape(b, t, 2, N_KV, HEAD_DIM).astype(jnp.bfloat16)
    return kv[:, :, 0].transpose(0, 2, 1, 3), kv[:, :, 1].transpose(0, 2, 3, 1)


def _kv_blocks(kv_rows, b, t):
    k, vt = _kv_heads(kv_rows, b, t)
    return (k.reshape(b, N_KV, t // 128, 128, HEAD_DIM),
            vt.reshape(b, N_KV, HEAD_DIM, t // 128, 128).transpose(0, 1, 3, 2, 4))


def _out_proj_kernel(ro_ref, rg_ref, oc_ref, os_ref, ow_ref, gt_ref, ex_ref, x_ref, wo_ref, y_ref, a_ref):
    @pl.when(pl.program_id(1) == 0)
    def _():
        gates = _hdot(_sigmoid(gt_ref[...]), ex_ref[...])
        nsa = (gates[:, 0:1024] * oc_ref[...] + gates[:, 1024:2048] * os_ref[...]
               + gates[:, 2048:3072] * ow_ref[...])
        a_ref[:, 0:1024] = (ro_ref[...] * rg_ref[...]).astype(jnp.bfloat16)
        a_ref[:, 1024:2048] = nsa.astype(jnp.bfloat16)

    y_ref[...] = x_ref[...] + jnp.dot(a_ref[...], wo_ref[...], preferred_element_type=jnp.float32)


def out_proj(ro, rg, oc, osl, ow, proj, gate_expand, x, wo_bf16, *, tm, tn):
    m = x.shape[0]
    row = lambda w: pl.BlockSpec((tm, w), lambda i, j: (i, 0))
    return pl.pallas_call(
        _out_proj_kernel,
        grid=(m // tm, D_MODEL // tn),
        in_specs=[row(1024), row(1024), row(1024), row(1024), row(1024),
                  pl.BlockSpec((tm, 256), lambda i, j: (i, C_GATE // 256)),
                  pl.BlockSpec((256, 3072), lambda i, j: (0, 0)),
                  pl.BlockSpec((tm, tn), lambda i, j: (i, j)),
                  pl.BlockSpec((D_MODEL, tn), lambda i, j: (0, j))],
        out_specs=pl.BlockSpec((tm, tn), lambda i, j: (i, j)),
        out_shape=jax.ShapeDtypeStruct((m, D_MODEL), jnp.float32),
        scratch_shapes=[pltpu.VMEM((tm, D_MODEL), jnp.bfloat16)],
        compiler_params=_cp(("parallel", "arbitrary")),
        name="out_proj",
    )(ro, rg, oc, osl, ow, proj, gate_expand, x, wo_bf16)


def _gate_expand_matrix():
    e = np.zeros((256, 3 * D_NSA), np.float32)
    for k in range(3):
        for h in range(N_HEADS):
            e[k * N_HEADS + h, k * D_NSA + h * HEAD_DIM:k * D_NSA + (h + 1) * HEAD_DIM] = 1.0
    return jnp.asarray(e)


def _ffn_kernel(x_ref, gf_ref, wg_ref, wu_ref, wd_ref, gl_ref, y_ref, h_ref, acc_ref):
    f = pl.program_id(1)

    @pl.when(f == 0)
    def _():
        x = x_ref[...]
        ms = jnp.mean(x * x, axis=-1, keepdims=True)
        h_ref[...] = (x * lax.rsqrt(ms + RMS_EPS) * gf_ref[...]).astype(jnp.bfloat16)
        acc_ref[...] = jnp.zeros_like(acc_ref)

    h = h_ref[...]
    gate = jnp.dot(h, wg_ref[...], preferred_element_type=jnp.float32)
    up = jnp.dot(h, wu_ref[...], preferred_element_type=jnp.float32)
    act = (_silu(gate) * up).astype(jnp.bfloat16)
    acc_ref[...] += jnp.dot(act, wd_ref[...], preferred_element_type=jnp.float32)

    @pl.when(f == pl.num_programs(1) - 1)
    def _():
        z = x_ref[...] + acc_ref[...]
        ms = jnp.mean(z * z, axis=-1, keepdims=True)
        y_ref[...] = z * lax.rsqrt(ms + RMS_EPS) * gl_ref[...]


def ffn_final(x, g_ffn, wg, wu, wd, g_final, *, tm, tf):
    m = x.shape[0]
    dff = wg.shape[1]
    return pl.pallas_call(
        _ffn_kernel,
        grid=(m // tm, dff // tf),
        in_specs=[pl.BlockSpec((tm, D_MODEL), lambda i, f: (i, 0)),
                  pl.BlockSpec((1, D_MODEL), lambda i, f: (0, 0)),
                  pl.BlockSpec((D_MODEL, tf), lambda i, f: (0, f)),
                  pl.BlockSpec((D_MODEL, tf), lambda i, f: (0, f)),
                  pl.BlockSpec((tf, D_MODEL), lambda i, f: (f, 0)),
                  pl.BlockSpec((1, D_MODEL), lambda i, f: (0, 0))],
        out_specs=pl.BlockSpec((tm, D_MODEL), lambda i, f: (i, 0)),
        out_shape=jax.ShapeDtypeStruct((m, D_MODEL), jnp.float32),
        scratch_shapes=[pltpu.VMEM((tm, D_MODEL), jnp.bfloat16), pltpu.VMEM((tm, D_MODEL), jnp.float32)],
        compiler_params=_cp(("parallel", "arbitrary")),
        name="ffn_final",
    )(x, g_ffn.reshape(1, -1), wg, wu, wd, g_final.reshape(1, -1))


def _cmp_part_paged_kernel(pt_ref, *refs, n_in):
    x_refs = refs[:n_in]
    perm_ref, w_ref, o_ref, xs_ref = refs[n_in:]
    cpp = x_refs[0].shape[-1] // D_CMP
    n_pair = 2 * (N_KV // 2)
    for p in range(n_in):
        x = x_refs[p][0].reshape(2 * N_KV * HEAD_DIM, x_refs[p].shape[-1])
        xp_all = jnp.dot(x.astype(jnp.bfloat16), perm_ref[...], preferred_element_type=jnp.float32)
        for ep in range(n_pair):
            xp = xp_all[ep * 128:(ep + 1) * 128].T
            for j in range(D_CMP):
                xs_ref[ep, j, p * cpp:(p + 1) * cpp, :] = xp[j * cpp:(j + 1) * cpp]
    for e in range(2):
        for hp in range(2):
            acc = None
            for j in range(D_CMP):
                d = _bdot(xs_ref[e * 2 + hp, j], w_ref[e, j])
                acc = d if acc is None else acc + d
            c0 = e * 256 + hp * 128
            o_ref[0, :, c0:c0 + 128] = acc[:, :128]
            o_ref[0, :, 512 + c0:512 + c0 + 128] = acc[:, 128:]


def cmp_part_sample(pool_t, page_table, w1x_bf16, *, pp):
    b, n_pages = page_table.shape
    page = pool_t.shape[-1]
    cpp = page // D_CMP

    def xspec(k):
        return pl.BlockSpec((1, 2, N_KV, HEAD_DIM, page), lambda bi, gi, pt: (pt[bi, gi * pp + k], 0, 0, 0, 0))

    r = np.arange(page)
    perm = np.zeros((page, page), np.float32)
    perm[r, (r % D_CMP) * cpp + r // D_CMP] = 1.0
    grid_spec = pltpu.PrefetchScalarGridSpec(
        num_scalar_prefetch=1,
        grid=(b, n_pages // pp),
        in_specs=[xspec(k) for k in range(pp)]
        + [pl.BlockSpec((page, page), lambda bi, gi, pt: (0, 0)),
           pl.BlockSpec((2, D_CMP, 128, 256), lambda bi, gi, pt: (0, 0, 0, 0))],
        out_specs=pl.BlockSpec((1, cpp * pp, 1024), lambda bi, gi, pt: (bi, gi, 0)),
        scratch_shapes=[pltpu.VMEM((N_KV, D_CMP, pp * cpp, 128), jnp.float32)],
    )
    return pl.pallas_call(
        functools.partial(_cmp_part_paged_kernel, n_in=pp),
        grid_spec=grid_spec,
        out_shape=jax.ShapeDtypeStruct((b, n_pages * cpp, 1024), jnp.float32),
        compiler_params=_cp(("parallel", "arbitrary")),
        name="cmp_part_sample",
    )(page_table, *([pool_t] * pp), jnp.asarray(perm, jnp.bfloat16), w1x_bf16)


def _fold_heads(o_ext):
    row_kvh = lax.broadcasted_iota(jnp.int32, (N_HEADS, 1), 0) // GQA
    out = jnp.zeros((N_HEADS, HEAD_DIM), jnp.float32)
    for kvh in range(N_KV):
        out = out + jnp.where(row_kvh == kvh, o_ext[:, kvh * HEAD_DIM:(kvh + 1) * HEAD_DIM], 0.0)
    return out


def _cmp_attn_sample_kernel(part_ref, pe_ref, w2bd_ref, q_ref, g8_ref, ov_ref, o_ref, imps_ref, *, q_pos):
    part = part_ref[0]
    n = part.shape[0]
    kc = _split2(_dot3(_split2(_cmp_hidden(part, pe_ref, 0)), _split2(w2bd_ref[0])))
    vc = _bdot(_cmp_hidden(part, pe_ref, 1), w2bd_ref[1])
    qh, ql = _split2(q_ref[0])
    s = (_dot_nt(qh, kc[0]) + (_dot_nt(qh, kc[1]) + _dot_nt(ql, kc[0]))) * SCALE
    cidx = lax.broadcasted_iota(jnp.int32, (1, n), 1)
    cmask = (cidx * D_CMP + (L_CMP - 1)) <= q_pos
    s = jnp.where(cmask, s, NEG)
    m = jnp.max(s, axis=-1, keepdims=True)
    ex = jnp.where(cmask, jnp.exp(s - m), 0.0)
    p = ex / jnp.maximum(jnp.sum(ex, axis=-1, keepdims=True), 1e-30)
    o_ref[0] = _fold_heads(_bdot(p, vc))
    ih, il = _split2(_hdot(g8_ref[...], p))
    ov = ov_ref[...]
    imps_ref[0] = (jnp.dot(ih, ov, preferred_element_type=jnp.float32)
                   + jnp.dot(il, ov, preferred_element_type=jnp.float32))


def cmp_attn_sample(part, pe, w2bd, q16ext, g8, ov_bf16, *, q_pos):
    b, n, _ = part.shape
    nbp = ov_bf16.shape[1]
    c3 = lambda bi: (0, 0, 0)
    return pl.pallas_call(
        functools.partial(_cmp_attn_sample_kernel, q_pos=q_pos),
        grid=(b,),
        in_specs=[pl.BlockSpec((1, n, 1024), lambda bi: (bi, 0, 0)),
                  pl.BlockSpec((2, 8, 256), c3),
                  pl.BlockSpec((2, 256, 256), c3),
                  pl.BlockSpec((1, N_HEADS, 256), lambda bi: (bi, 0, 0)),
                  pl.BlockSpec((8, N_HEADS), lambda bi: (0, 0)),
                  pl.BlockSpec((n, nbp), lambda bi: (0, 0))],
        out_specs=[pl.BlockSpec((1, N_HEADS, HEAD_DIM), lambda bi: (bi, 0, 0)),
                   pl.BlockSpec((1, 8, nbp), lambda bi: (bi, 0, 0))],
        out_shape=[jax.ShapeDtypeStruct((b, N_HEADS, HEAD_DIM), jnp.float32),
                   jax.ShapeDtypeStruct((b, 8, nbp), jnp.float32)],
        compiler_params=_cp(("parallel",)),
        name="cmp_attn_sample",
    )(part, pe, w2bd, q16ext, g8, ov_bf16)


def _topk_kernel(sc_ref, idx_ref, *, n_blk, q_blk, k_sel):
    imps = sc_ref[...]
    nbp = imps.shape[0]
    sidx = lax.broadcasted_iota(jnp.int32, (nbp, 1), 0)
    sf = sidx.astype(jnp.float32)
    forced = (sidx == 0) | (sidx == q_blk) | (sidx == q_blk - 1)
    causal = (sidx <= q_blk) & (sidx < n_blk)
    score = jnp.where(forced, -NEG, jnp.where(causal, imps, NEG))
    rows = []
    for _ in range(k_sel):
        m = jnp.max(score, axis=0, keepdims=True)
        pick = jnp.min(jnp.where(score == m, sf, float(nbp)), axis=0, keepdims=True)
        rows.append(jnp.where(m > 0.5 * NEG, pick, -1.0))
        score = jnp.where(sf == pick, 2.0 * NEG, score)
    idx_ref[...] = jnp.concatenate(rows, axis=0).astype(jnp.int32)


def topk_blocks(scores, *, n_blk, q_blk, k_sel):
    nbp, lanes = scores.shape
    return pl.pallas_call(
        functools.partial(_topk_kernel, n_blk=n_blk, q_blk=q_blk, k_sel=k_sel),
        out_shape=jax.ShapeDtypeStruct((k_sel, lanes), jnp.int32),
        name="topk_blocks",
    )(scores)


def _slc_sample_kernel(idx_ref, pt_ref, *refs, n_sel, new_blk):
    blk_refs = refs[:N_KV * n_sel]
    q_ref, knew_ref, vnew_ref, o_ref = refs[N_KV * n_sel:]
    b = pl.program_id(0)
    page = blk_refs[0].shape[-1]
    lane = lax.broadcasted_iota(jnp.int32, (1, n_sel * page), 1)
    for kvh in range(N_KV):
        kts, vts = [], []
        live = lane < 0
        has_new = False
        for n in range(n_sel):
            blk = blk_refs[kvh * n_sel + n]
            kts.append(blk[0, 0, 0])
            vts.append(blk[0, 1, 0])
            s_n = idx_ref[(b * N_KV + kvh) * n_sel + n]
            lo = n * page + (s_n % (page // L_SLC)) * L_SLC
            live = live | ((lane >= lo) & (lane < lo + L_SLC) & (s_n >= 0) & (s_n != new_blk))
            has_new = jnp.logical_or(has_new, s_n == new_blk)
        kt = jnp.concatenate(kts, axis=1).astype(jnp.bfloat16)
        vt = jnp.concatenate(vts, axis=1).astype(jnp.bfloat16)
        q = q_ref[0, kvh]
        s = jnp.dot(q.astype(jnp.bfloat16), kt, preferred_element_type=jnp.float32) * SCALE
        s = jnp.where(live, s, NEG)
        s_new = jnp.where(has_new, jnp.sum(q * knew_ref[0, kvh], axis=-1, keepdims=True) * SCALE, NEG)
        m = jnp.maximum(jnp.max(s, axis=-1, keepdims=True), s_new)
        ex = jnp.where(live, jnp.exp(s - m), 0.0)
        ex_new = jnp.where(has_new, jnp.exp(s_new - m), 0.0)
        denom = jnp.maximum(jnp.sum(ex, axis=-1, keepdims=True) + ex_new, 1e-30)
        o_ref[0, kvh] = (_dot_nt(ex.astype(jnp.bfloat16), vt) + ex_new * vnew_ref[0, kvh]) / denom


def slc_sample(idx_flat, page_table, pool_t, q4, knew, vnew, *, n_sel, new_blk):
    b, n_pages = page_table.shape
    page = pool_t.shape[-1]
    bpp = page // L_SLC

    def bspec(kvh, n):
        def imap(bi, idx, pt):
            s = jnp.clip(idx[(bi * N_KV + kvh) * n_sel + n], 0, bpp * n_pages - 1)
            return (pt[bi, s // bpp], 0, kvh, 0, 0)
        return pl.BlockSpec((1, 2, 1, HEAD_DIM, page), imap)

    small = lambda r: pl.BlockSpec((1, N_KV, r, HEAD_DIM), lambda bi, idx, pt: (bi, 0, 0, 0))
    grid_spec = pltpu.PrefetchScalarGridSpec(
        num_scalar_prefetch=2,
        grid=(b,),
        in_specs=[bspec(kvh, n) for kvh in range(N_KV) for n in range(n_sel)] + [small(8), small(1), small(1)],
        out_specs=small(8),
    )
    return pl.pallas_call(
        functools.partial(_slc_sample_kernel, n_sel=n_sel, new_blk=new_blk),
        grid_spec=grid_spec,
        out_shape=jax.ShapeDtypeStruct((b, N_KV, 8, HEAD_DIM), jnp.float32),
        compiler_params=_cp(("arbitrary",)),
        name="slc_sample",
    )(idx_flat, page_table, *([pool_t] * (N_KV * n_sel)), q4, knew, vnew)


def _win_sample_kernel(w_ref, q_ref, o_ref):
    w = w_ref[0]
    s = _dot_nt(q_ref[0].astype(jnp.bfloat16), w[:, :KV_W].astype(jnp.bfloat16)) * SCALE
    m = jnp.max(s, axis=-1, keepdims=True)
    ex = jnp.exp(s - m)
    p = ex / jnp.maximum(jnp.sum(ex, axis=-1, keepdims=True), 1e-30)
    o_ref[0] = _fold_heads(_bdot(p, w[:, KV_W:]))


def win_sample(win_rows, q16ext):
    b, nk, _ = win_rows.shape
    return pl.pallas_call(
        _win_sample_kernel,
        grid=(b,),
        in_specs=[pl.BlockSpec((1, nk, 2 * KV_W), lambda bi: (bi, 0, 0)),
                  pl.BlockSpec((1, N_HEADS, KV_W), lambda bi: (bi, 0, 0))],
        out_specs=pl.BlockSpec((1, N_HEADS, HEAD_DIM), lambda bi: (bi, 0, 0)),
        out_shape=jax.ShapeDtypeStruct((b, N_HEADS, HEAD_DIM), jnp.float32),
        compiler_params=_cp(("parallel",)),
        name="win_sample",
    )(win_rows, q16ext)


def _prep_weights(w_in, mu, w2, a2, g2):
    o3 = 3 * D_RWKV
    nsa0 = R_COLS
    cols = [w_in[:, nsa0:nsa0 + D_NSA],
            w_in[:, 0:o3],
            w_in[:, nsa0 + D_NSA:nsa0 + D_NSA + 6 * KV_W],
            w_in[:, o3:R_COLS],
            w_in[:, nsa0 + D_NSA + 6 * KV_W:],
            jnp.zeros((D_MODEL, D_IN_PAD - C_GATE - 3 * N_HEADS), w_in.dtype)]
    w_in_p = jnp.concatenate(cols, axis=1).astype(jnp.bfloat16)
    z = jnp.zeros((LORA_ALL, D_RWKV), jnp.float32)
    return dict(w_in_p=w_in_p, mu_p=mu,
                w2p=z.at[0:LORA_W].set(w2), a2p=z.at[LORA_W:LORA_W + LORA_A].set(a2),
                g2p=z.at[LORA_W + LORA_A:].set(g2))


def _layer_weights(layer, g_mix, w_in, mu, w0, w2, a0, a2, g2, k_k, k_a, r_k, lnx_w, lnx_b, phi_pe, phi_w1,
                   phi_w2, w_o, g_ffn, w_gate, w_up, w_down):
    w = _prep_weights(w_in[layer], mu[layer], w2[layer], a2[layer], g2[layer])
    w1t, pe8, w2k, w2vt, w2bd = _phi_weights(phi_pe[layer], phi_w1[layer], phi_w2[layer])
    w1x = _expand_w1(phi_w1[layer])
    w.update(g_mix=g_mix[layer], w0=w0[layer], a0=a0[layer], k_k=k_k[layer], k_a=k_a[layer], r_k=r_k[layer],
             lnx_w=lnx_w[layer], lnx_b=lnx_b[layer], w1x=w1x, w1x_bf16=w1x.astype(jnp.bfloat16),
             pe=pe_term(pe8, w1t), w2k=w2k, w2vt=w2vt, w2bd=w2bd,
             gate_expand=_gate_expand_matrix(), wo=w_o[layer].astype(jnp.bfloat16), g_ffn=g_ffn[layer],
             wg=w_gate[layer].astype(jnp.bfloat16), wu=w_up[layer].astype(jnp.bfloat16),
             wd=w_down[layer].astype(jnp.bfloat16))
    return w


def _pick(n, pref):
    while n % pref:
        pref //= 2
    return pref


def _prompt_layer(x_prompt, W, g_final):
    B, T, _ = x_prompt.shape
    M = B * T
    x2 = x_prompt.reshape(M, D_MODEL)
    proj = proj_matmul(x2, W["g_mix"], W["w_in_p"], apply_norm=True, tm=_pick(M, 512), tn=1024)
    proj3 = proj.reshape(B, T, D_IN_PAD)
    ro, rg, wkv = rwkv_mix(proj, jnp.zeros((B, 1, R_COLS), jnp.float32), W,
                           jnp.zeros((B, H_R, 64, 64), jnp.float32), b=B, t=T, tm=_pick(T, 256), tc=_pick(T, 64))
    cmp_kv = proj3[..., C_CMP:C_CMP + 2 * KV_W]
    slc_kv = proj3[..., C_SLC:C_SLC + 2 * KV_W]
    win_kv = proj3[..., C_WIN:C_WIN + 2 * KV_W]
    nch = T // D_CMP
    part = cmp_part_prompt(cmp_kv.reshape(B, nch, D_CMP * 2 * KV_W), W["w1x"])
    ovt = _overlap_t(128, nch - 1, nch).astype(jnp.bfloat16)
    o_cmp, sel = cmp_select_prompt(part, W["pe"], W["w2k"], W["w2vt"], proj3, ovt, tq=128)
    ks, vs = _kv_heads(slc_kv, B, T)
    kw, vw = _kv_blocks(win_kv, B, T)
    o_slc, o_win = slc_win_prompt(proj3, ks, vs, kw, vw, sel, tq=128)
    r2 = lambda z: z.reshape(M, 1024)
    x1 = out_proj(r2(ro), r2(rg), r2(o_cmp), r2(o_slc), r2(o_win), proj, W["gate_expand"], x2, W["wo"],
                  tm=_pick(M, 512), tn=1024)
    y = ffn_final(x1, W["g_ffn"], W["wg"], W["wu"], W["wd"], g_final, tm=_pick(M, 512), tf=512)
    kv5 = lambda z: z.reshape(B, T, 2, N_KV, HEAD_DIM)
    nwin = min(WINDOW, T)
    return y.reshape(B, T, D_MODEL), kv5(cmp_kv), kv5(slc_kv), kv5(win_kv)[:, T - nwin:], wkv


def _sample_layer(x_sample, xn_s, cache_cmp, cache_slc, cache_win, wkv0, shift, page_table, W, g_final):
    b = x_sample.shape[0]
    n_pool, page = cache_cmp.shape[0], cache_cmp.shape[1]
    n_pages = page_table.shape[1]
    past_len = n_pages * page
    n_rows = -(-(past_len + 1) // L_SLC) * L_SLC
    n_blk = n_rows // L_SLC
    q_blk = past_len // L_SLC
    nbp = -(-n_blk // 128) * 128

    rows = jnp.concatenate([xn_s, shift], axis=0)
    proj = proj_matmul(rows, W["g_mix"], W["w_in_p"], apply_norm=False, tm=2 * b, tn=1024)
    prev = jnp.concatenate([proj[b:, C_R:C_R + 3 * D_RWKV], proj[b:, C_LORA:C_LORA + LORA_ALL]], axis=1)
    ro, rg, wkv = rwkv_mix(proj[:b], prev, W, wkv0, b=b, t=1, tm=1, tc=1)

    q = proj[:b, C_Q:C_Q + D_NSA].reshape(b, N_KV, GQA, 1, HEAD_DIM)
    eye = jnp.eye(N_KV, dtype=jnp.float32).reshape(1, N_KV, 1, N_KV, 1)
    qext = (q * eye).reshape(b, N_KV, GQA, KV_W)
    q16ext = qext.reshape(b, N_HEADS, KV_W)
    cmp_new = proj[:b, C_CMP:C_CMP + 2 * KV_W]
    slc_new = proj[:b, C_SLC:C_SLC + 2 * KV_W]
    win_new = proj[:b, C_WIN:C_WIN + 2 * KV_W]

    part = cmp_part_sample(cache_cmp.transpose(0, 2, 3, 4, 1), page_table, W["w1x_bf16"], pp=_pick(n_pages, 32))
    nc_rows = part.shape[1]
    ov = _overlap_t(nbp, n_rows // D_CMP - L_CMP // D_CMP + 1, nc_rows)[:n_blk].T
    ov = jnp.pad(ov, ((0, 0), (0, nbp - n_blk)))
    g8 = jnp.asarray((np.arange(8)[:, None] == np.arange(N_HEADS)[None, :] // GQA).astype(np.float32))
    o_cmp, imps = cmp_attn_sample(part, W["pe"], W["w2bd"], q16ext, g8, ov.astype(jnp.bfloat16), q_pos=past_len)
    scores = imps[:, :N_KV].transpose(2, 0, 1).reshape(nbp, b * N_KV)
    k_sel = min(N_SELECT, n_blk)
    idx = topk_blocks(scores, n_blk=n_blk, q_blk=q_blk, k_sel=k_sel)
    q4 = jnp.pad(q.reshape(b, N_KV, GQA, HEAD_DIM), ((0, 0), (0, 0), (0, 8 - GQA), (0, 0)))
    slc_new5 = slc_new.reshape(b, 2, N_KV, 1, HEAD_DIM)
    o_slc = slc_sample(idx.T.reshape(-1), page_table, cache_slc.transpose(0, 2, 3, 4, 1), q4,
                       slc_new5[:, 0], slc_new5[:, 1], n_sel=k_sel, new_blk=q_blk)
    o_slc = o_slc[:, :, :GQA].reshape(b, D_NSA)

    n_buf = cache_win.shape[1]
    win_all = jnp.concatenate([cache_win, win_new.reshape(b, 1, 2, N_KV, HEAD_DIM)], axis=1)
    win_keep = win_all[:, n_buf + 1 - min(WINDOW, n_buf + 1):]
    o_win = win_sample(win_keep.reshape(b, -1, 2 * KV_W), q16ext)

    x2 = x_sample.reshape(b, D_MODEL)
    x1 = out_proj(ro.reshape(b, -1), rg.reshape(b, -1), o_cmp.reshape(b, D_NSA), o_slc, o_win.reshape(b, D_NSA),
                  proj, W["gate_expand"], x2, W["wo"], tm=b, tn=1024)
    y = ffn_final(x1, W["g_ffn"], W["wg"], W["wu"], W["wd"], g_final, tm=b, tf=512)
    kv5 = lambda z: z.reshape(b, 1, 2, N_KV, HEAD_DIM)
    return y.reshape(b, 1, D_MODEL), kv5(cmp_new), kv5(slc_new), win_keep, wkv


def kernel(x_prompt, x_sample, cache_cmp_kv, cache_slc_kv, cache_win_kv, state_wkv, state_shift, page_table, g_mix, w_in, mu, w0, w2, a0, a2, g2, k_k, k_a, r_k, lnx_w, lnx_b, phi_pe, phi_w1, phi_w2, w_o, g_ffn, w_gate, w_up, w_down, g_final):
    W = _layer_weights(0, g_mix, w_in, mu, w0, w2, a0, a2, g2, k_k, k_a, r_k, lnx_w, lnx_b, phi_pe, phi_w1,
                       phi_w2, w_o, g_ffn, w_gate, w_up, w_down)
    bp, bs = x_prompt.shape[0], x_sample.shape[0]
    small = jnp.concatenate([x_sample.reshape(bs, D_MODEL), x_prompt[:, -1]], axis=0)
    pad = (-small.shape[0]) % 8
    xn_small = rmsnorm_rows(jnp.pad(small, ((0, pad), (0, 0))), W["g_mix"])
    xn_s, shift_p = xn_small[:bs], xn_small[bs:bs + bp]

    y_p, cmp_p, slc_p, win_p, wkv_p = _prompt_layer(x_prompt, W, g_final)
    y_s, cmp_s, slc_s, win_s, wkv_s = _sample_layer(x_sample, xn_s, cache_cmp_kv[0], cache_slc_kv[0],
                                                    cache_win_kv[0], state_wkv[0], state_shift[0], page_table,
                                                    W, g_final)
    return (y_p, y_s, cmp_p[None], slc_p[None], win_p[None], wkv_p[None], shift_p[None],
            cmp_s[None], slc_s[None], win_s[None], wkv_s[None], xn_s[None])
```

```python
import functools

import jax
import jax.numpy as jnp
import numpy as np
from jax import lax
from jax.experimental import pallas as pl
from jax.experimental.pallas import tpu as pltpu

D_MODEL = 2048
D_RWKV = 1024
D_NSA = 1024
HEAD_DIM_R = 64
H_R = 16
LORA_W = 96
LORA_A = 96
LORA_G = 64
LORA_ALL = LORA_W + LORA_A + LORA_G
HEAD_DIM = 64
N_HEADS = 16
N_KV = 4
GQA = 4
KV_W = 256
L_CMP = 32
D_CMP = 16
L_SLC = 64
N_SELECT = 16
WINDOW = 512
PHI_HIDDEN = 64
R_COLS = 3 * D_RWKV + LORA_ALL
RMS_EPS = 1e-6
LNX_EPS = 64e-5
SCALE = HEAD_DIM ** -0.5
NEG = -1e30

C_Q, C_R, C_K, C_V = 0, 1024, 2048, 3072
C_CMP, C_SLC, C_WIN = 4096, 4608, 5120
C_LORA, C_GATE = 5632, 5888
D_IN_PAD = 6144

VMEM_LIMIT = 48 * 1024 * 1024
HI = lax.Precision.HIGHEST


def _cp(sem, limit=VMEM_LIMIT):
    return pltpu.CompilerParams(dimension_semantics=sem, vmem_limit_bytes=limit)


def _bdot(a, b):
    return jnp.dot(a.astype(jnp.bfloat16), b.astype(jnp.bfloat16), preferred_element_type=jnp.float32)


def _hdot(a, b):
    return jnp.dot(a, b, precision=HI, preferred_element_type=jnp.float32)


def _dot_nt(a, b, precision=None):
    return lax.dot_general(a, b, (((1,), (1,)), ((), ())), precision=precision,
                           preferred_element_type=jnp.float32)


def _sigmoid(x):
    return 1.0 / (1.0 + jnp.exp(-x))


def _silu(x):
    return x * _sigmoid(x)


def _rmsnorm_rows_kernel(x_ref, g_ref, o_ref):
    x = x_ref[...]
    ms = jnp.mean(x * x, axis=-1, keepdims=True)
    o_ref[...] = x * lax.rsqrt(ms + RMS_EPS) * g_ref[...]


def rmsnorm_rows(x, g):
    m, d = x.shape
    return pl.pallas_call(
        _rmsnorm_rows_kernel,
        out_shape=jax.ShapeDtypeStruct((m, d), jnp.float32),
        name="rmsnorm_rows",
    )(x, g.reshape(1, d))


def _proj_kernel(x_ref, g_ref, w_ref, o_ref, xn_ref, *, apply_norm):
    @pl.when(pl.program_id(1) == 0)
    def _():
        x = x_ref[...]
        if apply_norm:
            ms = jnp.mean(x * x, axis=-1, keepdims=True)
            x = x * lax.rsqrt(ms + RMS_EPS) * g_ref[...]
        xn_ref[...] = x.astype(jnp.bfloat16)

    o_ref[...] = jnp.dot(xn_ref[...], w_ref[...], preferred_element_type=jnp.float32)


def proj_matmul(x, g, w_bf16, *, apply_norm, tm, tn):
    m, d = x.shape
    n = w_bf16.shape[1]
    return pl.pallas_call(
        functools.partial(_proj_kernel, apply_norm=apply_norm),
        grid=(m // tm, n // tn),
        in_specs=[
            pl.BlockSpec((tm, d), lambda i, j: (i, 0)),
            pl.BlockSpec((1, d), lambda i, j: (0, 0)),
            pl.BlockSpec((d, tn), lambda i, j: (0, j)),
        ],
        out_specs=pl.BlockSpec((tm, tn), lambda i, j: (i, j)),
        out_shape=jax.ShapeDtypeStruct((m, n), jnp.float32),
        scratch_shapes=[pltpu.VMEM((tm, d), jnp.bfloat16)],
        compiler_params=_cp(("parallel", "arbitrary")),
        name="proj_matmul",
    )(x, g.reshape(1, d), w_bf16)


def _rwkv_prep_kernel(pr_ref, pk_ref, pv_ref, pl_ref, prev_ref, mu_ref, w0_ref, a0_ref, kk_ref, ka_ref,
                      w2_ref, a2_ref, g2_ref,
                      r_out, w_out, k_out, v_out, kk_out, a_out, g_out, carry_ref, *, row_prev):
    tm = pr_ref.shape[1]
    pr, pk, pv, plo = pr_ref[0], pk_ref[0], pv_ref[0], pl_ref[0]
    if row_prev:
        def shift_mix(p, lo, hi):
            return p + (prev_ref[0, :, lo:hi] - p) * mu_ref[:, lo:hi]
    else:
        @pl.when(pl.program_id(1) == 0)
        def _():
            carry_ref[...] = prev_ref[0]

        row0 = lax.broadcasted_iota(jnp.int32, (tm, 1), 0) == 0

        def shift_mix(p, lo, hi):
            prev = pltpu.roll(p, 1, axis=0)
            prev = jnp.where(row0, carry_ref[:, lo:hi], prev)
            return p + (prev - p) * mu_ref[:, lo:hi]

    xr = shift_mix(pr, 0, 1024)
    xk = shift_mix(pk, 1024, 2048)
    xv = shift_mix(pv, 2048, 3072)
    xl = shift_mix(plo, 3072, 3328)
    if not row_prev:
        last = tm - 1
        carry_ref[:, 0:1024] = pr[last:last + 1]
        carry_ref[:, 1024:2048] = pk[last:last + 1]
        carry_ref[:, 2048:3072] = pv[last:last + 1]
        carry_ref[:, 3072:3328] = plo[last:last + 1]

    wl = _hdot(jnp.tanh(xl), w2_ref[...])
    al = _hdot(xl, a2_ref[...])
    gl = _hdot(_sigmoid(xl), g2_ref[...])
    z = -(w0_ref[...] + wl)
    softplus = jnp.maximum(z, 0.0) + jnp.log(1.0 + jnp.exp(-jnp.abs(z)))
    w_log = -softplus - 0.5
    decay = jnp.exp(-jnp.exp(w_log))
    a = _sigmoid(a0_ref[...] + al)
    r_out[0] = xr
    w_out[0] = decay
    k_out[0] = xk * (1.0 + (a - 1.0) * ka_ref[...])
    v_out[0] = xv
    kk_out[0] = xk * kk_ref[...]
    a_out[0] = a
    g_out[0] = gl


def rwkv_prep(proj, prev, mu_p, w0, a0, k_k, k_a, w2p, a2p, g2p, *, b, t, tm, row_prev=False):
    proj3 = proj.reshape(b, t, D_IN_PAD)
    nt = t // tm
    prev_spec = (pl.BlockSpec((1, tm, R_COLS), lambda bi, ti: (bi, ti, 0)) if row_prev
                 else pl.BlockSpec((1, 1, R_COLS), lambda bi, ti: (bi, 0, 0)))
    row = lambda blk, w: pl.BlockSpec((1, tm, w), lambda bi, ti: (bi, ti, blk))
    vec = lambda w: pl.BlockSpec((1, w), lambda bi, ti: (0, 0))
    mat = pl.BlockSpec((LORA_ALL, D_RWKV), lambda bi, ti: (0, 0))
    out = pl.BlockSpec((1, tm, D_RWKV), lambda bi, ti: (bi, ti, 0))
    shp = jax.ShapeDtypeStruct((b, t, D_RWKV), jnp.float32)
    return pl.pallas_call(
        functools.partial(_rwkv_prep_kernel, row_prev=row_prev),
        grid=(b, nt),
        in_specs=[row(C_R // 1024, 1024), row(C_K // 1024, 1024), row(C_V // 1024, 1024),
                  row(C_LORA // 256, 256),
                  prev_spec,
                  vec(R_COLS), vec(D_RWKV), vec(D_RWKV), vec(D_RWKV), vec(D_RWKV),
                  mat, mat, mat],
        out_specs=[out] * 7,
        out_shape=[shp] * 7,
        scratch_shapes=[pltpu.VMEM((1, R_COLS), jnp.float32)],
        compiler_params=_cp(("parallel", "arbitrary")),
        name="rwkv_prep",
    )(proj3, proj3, proj3, proj3, prev, mu_p.reshape(1, -1), w0.reshape(1, -1), a0.reshape(1, -1),
      k_k.reshape(1, -1), k_a.reshape(1, -1), w2p, a2p, g2p)


def _rwkv_scan_kernel(r_ref, w_ref, k_ref, kk_ref, a_ref, v_ref, s0_ref, rk_ref, lnw_ref, lnb_ref,
                      o_ref, sfin_ref, s_ref, kkn_ref, b_ref, y_ref, *, halved):
    tc = r_ref.shape[1]
    ni = v_ref.shape[2]

    @pl.when(pl.program_id(1) == 0)
    def _():
        s_ref[...] = s0_ref[0]

    kkraw = kk_ref[0]
    nrm = jnp.sqrt(jnp.sum(kkraw * kkraw, axis=1, keepdims=True))
    kkn = kkraw / jnp.maximum(nrm, 1e-12)
    kkn_ref[...] = kkn
    b_ref[...] = kkn * a_ref[0]

    def step(t, carry):
        w_t = w_ref[0, t]
        k_t = k_ref[0, t]
        r_t = r_ref[0, t]
        kk_t = kkn_ref[t]
        b_t = b_ref[t]
        for i in range(ni):
            s = s_ref[i]
            sa = -jnp.sum(s * kk_t, axis=0, keepdims=True)
            v_i = v_ref[0, t, pl.ds(i, 1), :]
            s = s * w_t + sa * b_t + v_i * k_t
            y_ref[t, pl.ds(i, 1), :] = jnp.sum(s * r_t, axis=0, keepdims=True)
            s_ref[i] = s
        return carry

    lax.fori_loop(0, tc, step, 0)

    y = y_ref[...]
    v = v_ref[0]

    def head_sum(z):
        s = jnp.sum(z, axis=1, keepdims=True)
        if halved:
            s = s + pltpu.roll(s, 64, axis=2)
        return s

    mean = head_sum(y) * (1.0 / HEAD_DIM_R)
    d = y - mean
    var = head_sum(d * d) * (1.0 / HEAD_DIM_R)
    yn = d * lax.rsqrt(var + LNX_EPS) * lnw_ref[0] + lnb_ref[0]
    bonus = jnp.sum(r_ref[0] * k_ref[0] * rk_ref[0], axis=1, keepdims=True)
    o_ref[0] = yn + bonus * v

    @pl.when(pl.program_id(1) == pl.num_programs(1) - 1)
    def _():
        sfin_ref[0] = s_ref[...]


def rwkv_scan(r, w, k, kk, a, v, s0, rk, lnw, lnb, *, tc, halved):
    g, t = r.shape[0], r.shape[1]
    ni = v.shape[2]
    col = pl.BlockSpec((1, tc, 64, 128), lambda gi, ti: (gi, ti, 0, 0))
    vspec = pl.BlockSpec((1, tc, ni, 128), lambda gi, ti: (gi, ti, 0, 0))
    sspec = pl.BlockSpec((1, ni, 64, 128), lambda gi, ti: (gi, 0, 0, 0))
    return pl.pallas_call(
        functools.partial(_rwkv_scan_kernel, halved=halved),
        grid=(g, t // tc),
        in_specs=[col, col, col, col, col, vspec, sspec,
                  pl.BlockSpec((1, 64, 128), lambda gi, ti: (gi, 0, 0)),
                  pl.BlockSpec((1, ni, 128), lambda gi, ti: (gi, 0, 0)),
                  pl.BlockSpec((1, ni, 128), lambda gi, ti: (gi, 0, 0))],
        out_specs=[vspec, sspec],
        out_shape=[jax.ShapeDtypeStruct((g, t, ni, 128), jnp.float32),
                   jax.ShapeDtypeStruct((g, ni, 64, 128), jnp.float32)],
        scratch_shapes=[pltpu.VMEM((ni, 64, 128), jnp.float32),
                        pltpu.VMEM((tc, 64, 128), jnp.float32),
                        pltpu.VMEM((tc, 64, 128), jnp.float32),
                        pltpu.VMEM((tc, ni, 128), jnp.float32)],
        compiler_params=_cp(("parallel", "arbitrary")),
        name="rwkv_scan",
    )(r, w, k, kk, a, v, s0, rk, lnw, lnb)


def _to_scan_layout(x, b, t, halved):
    z = x.reshape(b, t, H_R, 64).transpose(1, 3, 0, 2).reshape(t, 64, b * H_R)
    if halved:
        return jnp.concatenate([z, z], axis=-1)[None]
    g = (b * H_R) // 128
    return z.reshape(t, 64, g, 128).transpose(2, 0, 1, 3)


def _v_to_scan_layout(x, b, t, halved):
    if halved:
        return x.reshape(b, t, H_R, 2, 32).transpose(1, 4, 3, 0, 2).reshape(1, t, 32, 128)
    g = (b * H_R) // 128
    z = x.reshape(b, t, H_R, 64).transpose(1, 3, 0, 2).reshape(t, 64, g, 128)
    return z.transpose(2, 0, 1, 3)


def _vecparam_scan_layout(p, b, halved):
    z = jnp.broadcast_to(p.reshape(1, H_R, 64), (b, H_R, 64))
    if halved:
        return z.reshape(b, H_R, 2, 32).transpose(3, 2, 0, 1).reshape(1, 32, 128)
    g = (b * H_R) // 128
    return z.transpose(2, 0, 1).reshape(64, g, 128).transpose(1, 0, 2)


def _keyparam_scan_layout(p, b, halved):
    z = jnp.broadcast_to(p.reshape(1, H_R, 64), (b, H_R, 64)).transpose(2, 0, 1).reshape(64, b * H_R)
    if halved:
        return jnp.concatenate([z, z], axis=-1)[None]
    g = (b * H_R) // 128
    return z.reshape(64, g, 128).transpose(1, 0, 2)


def _from_scan_layout(o, b, t, halved):
    if halved:
        return o.reshape(t, 32, 2, b, H_R).transpose(3, 0, 4, 2, 1).reshape(b, t, D_RWKV)
    g = o.shape[0]
    return o.transpose(1, 2, 0, 3).reshape(t, 64, b, H_R).transpose(2, 0, 3, 1).reshape(b, t, D_RWKV)


def _state_to_scan_layout(s, b, halved):
    if halved:
        return s.reshape(b, H_R, 2, 32, 64).transpose(3, 4, 2, 0, 1).reshape(1, 32, 64, 128)
    g = (b * H_R) // 128
    return s.reshape(g, 128, 64, 64).transpose(0, 2, 3, 1)


def _state_from_scan_layout(s, b, halved):
    if halved:
        return s.reshape(32, 64, 2, b, H_R).transpose(3, 4, 2, 0, 1).reshape(b, H_R, 64, 64)
    return s.transpose(0, 3, 1, 2).reshape(b, H_R, 64, 64)


def rwkv_mix(proj, prev, wts, s0, *, b, t, tm, tc):
    halved = (b * H_R == 64)
    if t == 1:
        outs = rwkv_prep(proj, prev.reshape(1, b, R_COLS), wts["mu_p"], wts["w0"], wts["a0"], wts["k_k"],
                         wts["k_a"], wts["w2p"], wts["a2p"], wts["g2p"], b=1, t=b, tm=b, row_prev=True)
        r, w, k, v, kk, a, g = [z.reshape(b, 1, D_RWKV) for z in outs]
    else:
        r, w, k, v, kk, a, g = rwkv_prep(proj, prev, wts["mu_p"], wts["w0"], wts["a0"], wts["k_k"], wts["k_a"],
                                         wts["w2p"], wts["a2p"], wts["g2p"], b=b, t=t, tm=tm)
    tr = lambda z: _to_scan_layout(z, b, t, halved)
    o, sfin = rwkv_scan(tr(r), tr(w), tr(k), tr(kk), tr(a), _v_to_scan_layout(v, b, t, halved),
                        _state_to_scan_layout(s0, b, halved),
                        _keyparam_scan_layout(wts["r_k"], b, halved),
                        _vecparam_scan_layout(wts["lnx_w"], b, halved),
                        _vecparam_scan_layout(wts["lnx_b"], b, halved),
                        tc=tc, halved=halved)
    return _from_scan_layout(o, b, t, halved), g, _state_from_scan_layout(sfin, b, halved)


def _cmp_part_kernel(*refs, n_in, precise, prefetch):
    refs = refs[prefetch:]
    x_refs, w_ref, o_ref = refs[:n_in], refs[n_in], refs[n_in + 1]
    for e in range(2):
        for hp in range(2):
            acc = None
            for j in range(D_CMP):
                lo = j * 512 + e * 256 + hp * 128
                xs = [xr[0, :, lo:lo + 128] for xr in x_refs]
                x = xs[0] if n_in == 1 else jnp.concatenate(xs, axis=0)
                d = _hdot(x, w_ref[e, j]) if precise else _bdot(x, w_ref[e, j])
                acc = d if acc is None else acc + d
            c0 = e * 256 + hp * 128
            o_ref[0, :, c0:c0 + 128] = acc[:, :128]
            o_ref[0, :, 512 + c0:512 + c0 + 128] = acc[:, 128:]


def cmp_part_prompt(rows, w1x):
    b, nch, _ = rows.shape
    return pl.pallas_call(
        functools.partial(_cmp_part_kernel, n_in=1, precise=True, prefetch=0),
        grid=(b,),
        in_specs=[pl.BlockSpec((1, nch, 8192), lambda bi: (bi, 0, 0)),
                  pl.BlockSpec((2, D_CMP, 128, 256), lambda bi: (0, 0, 0, 0))],
        out_specs=pl.BlockSpec((1, nch, 1024), lambda bi: (bi, 0, 0)),
        out_shape=jax.ShapeDtypeStruct((b, nch, 1024), jnp.float32),
        compiler_params=_cp(("parallel",)),
        name="cmp_part_prompt",
    )(rows, w1x)


def _expand_w1(phi_w1):
    w1 = phi_w1.reshape(2, 2, D_CMP, HEAD_DIM, PHI_HIDDEN)
    eye = jnp.eye(2, dtype=phi_w1.dtype)
    w = jnp.einsum("ab,esjdf->ejadsbf", eye, w1)
    return w.reshape(2, D_CMP, 128, 256)


def _split2(x):
    hi = x.astype(jnp.bfloat16)
    return hi, (x - hi.astype(jnp.float32)).astype(jnp.bfloat16)


def _dot3(a, b):
    (ah, al), (bh, bl) = a, b
    d = lambda x, y: jnp.dot(x, y, preferred_element_type=jnp.float32)
    return d(ah, bh) + (d(ah, bl) + d(al, bh))


def _pe_term_kernel(pe_ref, w1t_ref, o_ref):
    for e in range(2):
        o_ref[e] = _hdot(pe_ref[e], w1t_ref[e])


def pe_term(pe8, w1t):
    return pl.pallas_call(
        _pe_term_kernel,
        out_shape=jax.ShapeDtypeStruct((2, 8, N_KV * PHI_HIDDEN), jnp.float32),
        name="pe_term",
    )(pe8, w1t)


def _cmp_hidden(part, pe_ref, e):
    n = part.shape[0]
    lo = e * 256
    h = part[:, lo:lo + 256] + pltpu.roll(part[:, 512 + lo:512 + lo + 256], n - 1, axis=0)
    return _silu(h + pe_ref[e, 0:1])


def _cmp_sel_kernel(part_ref, pe_ref, w2k_ref, w2vt_ref, q_ref, ovt_ref, o_ref, sel_ref,
                    kch_ref, kcl_ref, vct_ref, *, tq, n_blk):
    ti = pl.program_id(1)

    @pl.when(ti == 0)
    def _():
        part = part_ref[0]
        hk = _split2(_cmp_hidden(part, pe_ref, 0))
        hv = _cmp_hidden(part, pe_ref, 1)
        for kvh in range(N_KV):
            kch_ref[kvh], kcl_ref[kvh] = _split2(_dot3(hk, _split2(w2k_ref[kvh])))
            vct_ref[kvh] = _dot_nt(w2vt_ref[kvh], hv, HI).astype(jnp.bfloat16)

    nc = kch_ref.shape[1]
    nbp = -(-n_blk // 8) * 8
    k_sel = min(N_SELECT, n_blk)
    qpos = ti * tq + lax.broadcasted_iota(jnp.int32, (1, tq), 1)
    cidx = lax.broadcasted_iota(jnp.int32, (nc, 1), 0)
    cmask = (cidx * D_CMP + (L_CMP - 1)) <= qpos
    sidx = cidx[:nbp]
    qblk = qpos // L_SLC
    forced = (sidx == 0) | (sidx == qblk) | (sidx == qblk - 1)
    causal = sidx <= qblk
    pad = jnp.zeros((128 - nbp, tq), jnp.float32)
    need_rank = (ti + 1) * tq > k_sel * L_SLC

    cmask4 = jnp.concatenate([cmask.astype(jnp.float32)] * GQA, axis=1) > 0.5
    for kvh in range(N_KV):
        qt = (q_ref[0, :, kvh * 256:(kvh + 1) * 256] * SCALE).T
        qt4 = _split2(jnp.concatenate([qt[g * 64:(g + 1) * 64] for g in range(GQA)], axis=1))
        st = _dot3((kch_ref[kvh], kcl_ref[kvh]), qt4)
        st = jnp.where(cmask4, st, NEG)
        m = jnp.max(st, axis=0, keepdims=True)
        ex = jnp.where(cmask4, jnp.exp(st - m), 0.0)
        pt = ex / jnp.maximum(jnp.sum(ex, axis=0, keepdims=True), 1e-30)
        ot = jnp.dot(vct_ref[kvh], pt.astype(jnp.bfloat16), preferred_element_type=jnp.float32)
        o_ref[0, :, kvh * 256:(kvh + 1) * 256] = jnp.concatenate(
            [ot[:, g * tq:(g + 1) * tq] for g in range(GQA)], axis=0).T
        impt = pt[:, 0:tq]
        for g in range(1, GQA):
            impt = impt + pt[:, g * tq:(g + 1) * tq]

        @pl.when(need_rank)
        def _():
            ih, il = _split2(impt)
            ov = ovt_ref[...]
            imps = (jnp.dot(ov, ih, preferred_element_type=jnp.float32)
                    + jnp.dot(ov, il, preferred_element_type=jnp.float32))[:nbp]
            score = jnp.where(forced, -NEG, jnp.where(causal, imps, NEG))
            rank = jnp.zeros((nbp, tq), jnp.float32)
            for s2 in range(n_blk):
                row = score[s2:s2 + 1]
                beats = (row > score) | ((row == score) & (s2 < sidx))
                rank = rank + beats.astype(jnp.float32)
            seln = ((rank < k_sel) & causal).astype(jnp.float32)
            sel_ref[0, kvh] = jnp.concatenate([seln, pad], axis=0)

        @pl.when(jnp.logical_not(need_rank))
        def _():
            sel_ref[0, kvh] = jnp.concatenate([causal.astype(jnp.float32), pad], axis=0)


def cmp_select_prompt(part, pe, w2k, w2vt, proj3, ovt_bf16, *, tq):
    b, nc, _ = part.shape
    t = proj3.shape[1]
    c3 = lambda bi, ti: (0, 0, 0)
    return pl.pallas_call(
        functools.partial(_cmp_sel_kernel, tq=tq, n_blk=t // L_SLC),
        grid=(b, t // tq),
        in_specs=[pl.BlockSpec((1, nc, 1024), lambda bi, ti: (bi, 0, 0)),
                  pl.BlockSpec((2, 8, 256), c3),
                  pl.BlockSpec((N_KV, 256, 64), c3),
                  pl.BlockSpec((N_KV, 64, 256), c3),
                  pl.BlockSpec((1, tq, 1024), lambda bi, ti: (bi, ti, C_Q // 1024)),
                  pl.BlockSpec((128, nc), lambda bi, ti: (0, 0))],
        out_specs=[pl.BlockSpec((1, tq, 1024), lambda bi, ti: (bi, ti, 0)),
                   pl.BlockSpec((1, N_KV, 128, tq), lambda bi, ti: (bi, 0, 0, ti))],
        out_shape=[jax.ShapeDtypeStruct((b, t, 1024), jnp.float32),
                   jax.ShapeDtypeStruct((b, N_KV, 128, t), jnp.float32)],
        scratch_shapes=[pltpu.VMEM((N_KV, nc, 64), jnp.bfloat16),
                        pltpu.VMEM((N_KV, nc, 64), jnp.bfloat16),
                        pltpu.VMEM((N_KV, 64, nc), jnp.bfloat16)],
        compiler_params=_cp(("parallel", "arbitrary")),
        name="cmp_select_prompt",
    )(part, pe, w2k, w2vt, proj3, ovt_bf16)


def _overlap_t(n_blocks_pad, nc, nc_pad):
    c_start = np.arange(nc_pad) * D_CMP
    s_start = np.arange(n_blocks_pad) * L_SLC
    ov = (c_start[None, :] < s_start[:, None] + L_SLC) & (c_start[None, :] + L_CMP > s_start[:, None])
    ov &= (np.arange(nc_pad) < nc)[None, :]
    return jnp.asarray(ov.astype(np.float32))


def _phi_weights(phi_pe, phi_w1, phi_w2):
    w1t = jnp.tile(phi_w1, (1, 1, N_KV))
    pe8 = jnp.broadcast_to(phi_pe.reshape(2, 1, L_CMP * HEAD_DIM), (2, 8, L_CMP * HEAD_DIM))
    z = jnp.zeros((N_KV, N_KV, PHI_HIDDEN, HEAD_DIM), jnp.float32)
    idx = jnp.arange(N_KV)
    w2k = z.at[idx, idx].set(phi_w2[0]).reshape(N_KV, 256, 64)
    w2vt = jnp.transpose(z.at[idx, idx].set(phi_w2[1]), (0, 3, 1, 2)).reshape(N_KV, 64, 256)
    eye = jnp.eye(N_KV, dtype=jnp.float32)
    w2bd = jnp.einsum("ab,efd->eafbd", eye, phi_w2).reshape(2, 256, 256)
    return w1t, pe8, w2k, w2vt, w2bd


def _attend_t(k, vt, qt4, bias, tq):
    s = jnp.dot(k, qt4, preferred_element_type=jnp.float32) + jnp.concatenate([bias] * GQA, axis=1)
    m = jnp.max(s, axis=0, keepdims=True)
    p = jnp.exp((s - m).astype(jnp.bfloat16))
    vt1 = jnp.concatenate([vt, jnp.ones((16, vt.shape[1]), jnp.bfloat16)], axis=0)
    ol = jnp.dot(vt1, p, preferred_element_type=jnp.float32)
    o = ol[:HEAD_DIM] / jnp.maximum(ol[HEAD_DIM:HEAD_DIM + 1], 1e-30)
    return jnp.concatenate([o[:, g * tq:(g + 1) * tq] for g in range(GQA)], axis=0).T


SLC_KEY_STEP = 512
WIN_BLOCKS = WINDOW // 128 + 1


def _slc_win_kernel(q_ref, ks_ref, vs_ref, kw_ref, vw_ref, sel_ref, ex_ref, os_ref, ow_ref, *, tq):
    ti = pl.program_id(2)
    t = ks_ref.shape[2]
    qt = (q_ref[0] * SCALE).T.astype(jnp.bfloat16)
    qt4 = jnp.concatenate([qt[g * HEAD_DIM:(g + 1) * HEAD_DIM] for g in range(GQA)], axis=1)
    qpos = ti * tq + lax.broadcasted_iota(jnp.int32, (1, tq), 1)
    selt = sel_ref[0, 0].astype(jnp.bfloat16)

    n_var = -(-t // SLC_KEY_STEP)
    for c in range(n_var):
        nk = min((c + 1) * SLC_KEY_STEP, t)

        @pl.when((ti * tq) // SLC_KEY_STEP == c)
        def _():
            on = jnp.dot(ex_ref[0:nk], selt, preferred_element_type=jnp.float32)
            kpos = lax.broadcasted_iota(jnp.int32, (nk, 1), 0)
            bias = (on - 1.0) * (-NEG) + jnp.where(kpos <= qpos, 0.0, NEG)
            os_ref[0] = _attend_t(ks_ref[0, 0, 0:nk], vs_ref[0, 0, :, 0:nk], qt4, bias, tq)

    nkb = kw_ref.shape[2]
    w0 = jnp.clip(ti - (WIN_BLOCKS - 1), 0, nkb - WIN_BLOCKS)
    kw = kw_ref[0, 0, pl.ds(w0, WIN_BLOCKS)].reshape(WIN_BLOCKS * 128, HEAD_DIM)
    vwt = jnp.concatenate([vw_ref[0, 0, w0 + i] for i in range(WIN_BLOCKS)], axis=1)
    diff = qpos - (w0 * 128 + lax.broadcasted_iota(jnp.int32, (WIN_BLOCKS * 128, 1), 0))
    ow_ref[0] = _attend_t(kw, vwt, qt4, jnp.where((diff >= 0) & (diff < WINDOW), 0.0, NEG), tq)


def slc_win_prompt(proj3, ks, vst, kw, vwt, selt, *, tq):
    b, t, _ = proj3.shape
    nkb = t // 128
    expand = jnp.asarray((np.arange(t)[:, None] // L_SLC == np.arange(128)[None, :]).astype(np.float32),
                         jnp.bfloat16)
    i4 = lambda bi, hi, ti: (bi, hi, 0, 0)
    i5 = lambda bi, hi, ti: (bi, hi, 0, 0, 0)
    out = pl.BlockSpec((1, tq, 256), lambda bi, hi, ti: (bi, ti, hi))
    shp = jax.ShapeDtypeStruct((b, t, 1024), jnp.float32)
    return pl.pallas_call(
        functools.partial(_slc_win_kernel, tq=tq),
        grid=(b, N_KV, t // tq),
        in_specs=[pl.BlockSpec((1, tq, 256), lambda bi, hi, ti: (bi, ti, hi)),
                  pl.BlockSpec((1, 1, t, HEAD_DIM), i4),
                  pl.BlockSpec((1, 1, HEAD_DIM, t), i4),
                  pl.BlockSpec((1, 1, nkb, 128, HEAD_DIM), i5),
                  pl.BlockSpec((1, 1, nkb, HEAD_DIM, 128), i5),
                  pl.BlockSpec((1, 1, 128, tq), lambda bi, hi, ti: (bi, hi, 0, ti)),
                  pl.BlockSpec((t, 128), lambda bi, hi, ti: (0, 0))],
        out_specs=[out, out],
        out_shape=[shp, shp],
        compiler_params=_cp(("parallel", "parallel", "arbitrary")),
        name="slc_win_prompt",
    )(proj3, ks, vst, kw, vwt, selt, expand)


def _kv_heads(kv_rows, b, t):
    kv = kv_rows.reshape(b, t, 2, N_KV, HEAD_DIM).astype(jnp.bfloat16)
    return kv[:, :, 0].transpose(0, 2, 1, 3), kv[:, :, 1].transpose(0, 2, 3, 1)


def _kv_blocks(kv_rows, b, t):
    k, vt = _kv_heads(kv_rows, b, t)
    return (k.reshape(b, N_KV, t // 128, 128, HEAD_DIM),
            vt.reshape(b, N_KV, HEAD_DIM, t // 128, 128).transpose(0, 1, 3, 2, 4))


def _out_proj_kernel(ro_ref, rg_ref, oc_ref, os_ref, ow_ref, gt_ref, ex_ref, x_ref, wo_ref, y_ref, a_ref):
    @pl.when(pl.program_id(1) == 0)
    def _():
        gh, gl = _split2(_sigmoid(gt_ref[:, 0:128]))
        ex = ex_ref[...]
        gates = (jnp.dot(gh, ex, preferred_element_type=jnp.float32)
                 + jnp.dot(gl, ex, preferred_element_type=jnp.float32))
        nsa = (gates[:, 0:1024] * oc_ref[...] + gates[:, 1024:2048] * os_ref[...]
               + gates[:, 2048:3072] * ow_ref[...])
        a_ref[:, 0:1024] = (ro_ref[...] * rg_ref[...]).astype(jnp.bfloat16)
        a_ref[:, 1024:2048] = nsa.astype(jnp.bfloat16)

    y_ref[...] = x_ref[...] + jnp.dot(a_ref[...], wo_ref[...], preferred_element_type=jnp.float32)


def out_proj(ro, rg, oc, osl, ow, proj, gate_expand, x, wo_bf16, *, tm, tn):
    m = x.shape[0]
    row = lambda w: pl.BlockSpec((tm, w), lambda i, j: (i, 0))
    return pl.pallas_call(
        _out_proj_kernel,
        grid=(m // tm, D_MODEL // tn),
        in_specs=[row(1024), row(1024), row(1024), row(1024), row(1024),
                  pl.BlockSpec((tm, 256), lambda i, j: (i, C_GATE // 256)),
                  pl.BlockSpec((128, 3072), lambda i, j: (0, 0)),
                  pl.BlockSpec((tm, tn), lambda i, j: (i, j)),
                  pl.BlockSpec((D_MODEL, tn), lambda i, j: (0, j))],
        out_specs=pl.BlockSpec((tm, tn), lambda i, j: (i, j)),
        out_shape=jax.ShapeDtypeStruct((m, D_MODEL), jnp.float32),
        scratch_shapes=[pltpu.VMEM((tm, D_MODEL), jnp.bfloat16)],
        compiler_params=_cp(("parallel", "arbitrary")),
        name="out_proj",
    )(ro, rg, oc, osl, ow, proj, gate_expand, x, wo_bf16)


def _gate_expand_matrix():
    e = np.zeros((128, 3 * D_NSA), np.float32)
    for k in range(3):
        for h in range(N_HEADS):
            e[k * N_HEADS + h, k * D_NSA + h * HEAD_DIM:k * D_NSA + (h + 1) * HEAD_DIM] = 1.0
    return jnp.asarray(e, jnp.bfloat16)


def _ffn_kernel(x_ref, gf_ref, wg_ref, wu_ref, wd_ref, gl_ref, y_ref, h_ref, acc_ref):
    f = pl.program_id(1)

    @pl.when(f == 0)
    def _():
        x = x_ref[...]
        ms = jnp.mean(x * x, axis=-1, keepdims=True)
        h_ref[...] = (x * lax.rsqrt(ms + RMS_EPS) * gf_ref[...]).astype(jnp.bfloat16)
        acc_ref[...] = jnp.zeros_like(acc_ref)

    h = h_ref[...]
    gate = jnp.dot(h, wg_ref[...], preferred_element_type=jnp.float32)
    up = jnp.dot(h, wu_ref[...], preferred_element_type=jnp.float32)
    act = (_silu(gate) * up).astype(jnp.bfloat16)
    acc_ref[...] += jnp.dot(act, wd_ref[...], preferred_element_type=jnp.float32)

    @pl.when(f == pl.num_programs(1) - 1)
    def _():
        z = x_ref[...] + acc_ref[...]
        ms = jnp.mean(z * z, axis=-1, keepdims=True)
        y_ref[...] = z * lax.rsqrt(ms + RMS_EPS) * gl_ref[...]


def ffn_final(x, g_ffn, wg, wu, wd, g_final, *, tm, tf):
    m = x.shape[0]
    dff = wg.shape[1]
    return pl.pallas_call(
        _ffn_kernel,
        grid=(m // tm, dff // tf),
        in_specs=[pl.BlockSpec((tm, D_MODEL), lambda i, f: (i, 0)),
                  pl.BlockSpec((1, D_MODEL), lambda i, f: (0, 0)),
                  pl.BlockSpec((D_MODEL, tf), lambda i, f: (0, f)),
                  pl.BlockSpec((D_MODEL, tf), lambda i, f: (0, f)),
                  pl.BlockSpec((tf, D_MODEL), lambda i, f: (f, 0)),
                  pl.BlockSpec((1, D_MODEL), lambda i, f: (0, 0))],
        out_specs=pl.BlockSpec((tm, D_MODEL), lambda i, f: (i, 0)),
        out_shape=jax.ShapeDtypeStruct((m, D_MODEL), jnp.float32),
        scratch_shapes=[pltpu.VMEM((tm, D_MODEL), jnp.bfloat16), pltpu.VMEM((tm, D_MODEL), jnp.float32)],
        compiler_params=_cp(("parallel", "arbitrary")),
        name="ffn_final",
    )(x, g_ffn.reshape(1, -1), wg, wu, wd, g_final.reshape(1, -1))


def _cmp_part_paged_kernel(pt_ref, *refs, n_in):
    x_refs = refs[:n_in]
    perm_ref, w_ref, o_ref, xs_ref = refs[n_in:]
    cpp = x_refs[0].shape[-1] // D_CMP
    n_pair = 2 * (N_KV // 2)
    for p in range(n_in):
        x = x_refs[p][0].reshape(2 * N_KV * HEAD_DIM, x_refs[p].shape[-1])
        xp_all = jnp.dot(x.astype(jnp.bfloat16), perm_ref[...], preferred_element_type=jnp.float32)
        for ep in range(n_pair):
            xp = xp_all[ep * 128:(ep + 1) * 128].T
            for j in range(D_CMP):
                xs_ref[ep, j, p * cpp:(p + 1) * cpp, :] = xp[j * cpp:(j + 1) * cpp]
    for e in range(2):
        for hp in range(2):
            acc = None
            for j in range(D_CMP):
                d = _bdot(xs_ref[e * 2 + hp, j], w_ref[e, j])
                acc = d if acc is None else acc + d
            c0 = e * 256 + hp * 128
            o_ref[0, :, c0:c0 + 128] = acc[:, :128]
            o_ref[0, :, 512 + c0:512 + c0 + 128] = acc[:, 128:]


def cmp_part_sample(pool_t, page_table, w1x_bf16, *, pp):
    b, n_pages = page_table.shape
    page = pool_t.shape[-1]
    cpp = page // D_CMP

    def xspec(k):
        return pl.BlockSpec((1, 2, N_KV, HEAD_DIM, page), lambda bi, gi, pt: (pt[bi, gi * pp + k], 0, 0, 0, 0))

    r = np.arange(page)
    perm = np.zeros((page, page), np.float32)
    perm[r, (r % D_CMP) * cpp + r // D_CMP] = 1.0
    grid_spec = pltpu.PrefetchScalarGridSpec(
        num_scalar_prefetch=1,
        grid=(b, n_pages // pp),
        in_specs=[xspec(k) for k in range(pp)]
        + [pl.BlockSpec((page, page), lambda bi, gi, pt: (0, 0)),
           pl.BlockSpec((2, D_CMP, 128, 256), lambda bi, gi, pt: (0, 0, 0, 0))],
        out_specs=pl.BlockSpec((1, cpp * pp, 1024), lambda bi, gi, pt: (bi, gi, 0)),
        scratch_shapes=[pltpu.VMEM((N_KV, D_CMP, pp * cpp, 128), jnp.float32)],
    )
    return pl.pallas_call(
        functools.partial(_cmp_part_paged_kernel, n_in=pp),
        grid_spec=grid_spec,
        out_shape=jax.ShapeDtypeStruct((b, n_pages * cpp, 1024), jnp.float32),
        compiler_params=_cp(("parallel", "arbitrary")),
        name="cmp_part_sample",
    )(page_table, *([pool_t] * pp), jnp.asarray(perm, jnp.bfloat16), w1x_bf16)


def _fold_heads(o_ext):
    row_kvh = lax.broadcasted_iota(jnp.int32, (N_HEADS, 1), 0) // GQA
    out = jnp.zeros((N_HEADS, HEAD_DIM), jnp.float32)
    for kvh in range(N_KV):
        out = out + jnp.where(row_kvh == kvh, o_ext[:, kvh * HEAD_DIM:(kvh + 1) * HEAD_DIM], 0.0)
    return out


def _cmp_attn_sample_kernel(part_ref, pe_ref, w2bd_ref, q_ref, g8_ref, ov_ref, o_ref, imps_ref, *, q_pos):
    part = part_ref[0]
    n = part.shape[0]
    kc = _split2(_dot3(_split2(_cmp_hidden(part, pe_ref, 0)), _split2(w2bd_ref[0])))
    vc = _bdot(_cmp_hidden(part, pe_ref, 1), w2bd_ref[1])
    qh, ql = _split2(q_ref[0])
    s = (_dot_nt(qh, kc[0]) + (_dot_nt(qh, kc[1]) + _dot_nt(ql, kc[0]))) * SCALE
    cidx = lax.broadcasted_iota(jnp.int32, (1, n), 1)
    cmask = (cidx * D_CMP + (L_CMP - 1)) <= q_pos
    s = jnp.where(cmask, s, NEG)
    m = jnp.max(s, axis=-1, keepdims=True)
    ex = jnp.where(cmask, jnp.exp(s - m), 0.0)
    p = ex / jnp.maximum(jnp.sum(ex, axis=-1, keepdims=True), 1e-30)
    o_ref[0] = _fold_heads(_bdot(p, vc))
    ih, il = _split2(_hdot(g8_ref[...], p))
    ov = ov_ref[...]
    imps_ref[0] = (jnp.dot(ih, ov, preferred_element_type=jnp.float32)
                   + jnp.dot(il, ov, preferred_element_type=jnp.float32))


def cmp_attn_sample(part, pe, w2bd, q16ext, g8, ov_bf16, *, q_pos):
    b, n, _ = part.shape
    nbp = ov_bf16.shape[1]
    c3 = lambda bi: (0, 0, 0)
    return pl.pallas_call(
        functools.partial(_cmp_attn_sample_kernel, q_pos=q_pos),
        grid=(b,),
        in_specs=[pl.BlockSpec((1, n, 1024), lambda bi: (bi, 0, 0)),
                  pl.BlockSpec((2, 8, 256), c3),
                  pl.BlockSpec((2, 256, 256), c3),
                  pl.BlockSpec((1, N_HEADS, 256), lambda bi: (bi, 0, 0)),
                  pl.BlockSpec((8, N_HEADS), lambda bi: (0, 0)),
                  pl.BlockSpec((n, nbp), lambda bi: (0, 0))],
        out_specs=[pl.BlockSpec((1, N_HEADS, HEAD_DIM), lambda bi: (bi, 0, 0)),
                   pl.BlockSpec((1, 8, nbp), lambda bi: (bi, 0, 0))],
        out_shape=[jax.ShapeDtypeStruct((b, N_HEADS, HEAD_DIM), jnp.float32),
                   jax.ShapeDtypeStruct((b, 8, nbp), jnp.float32)],
        compiler_params=_cp(("parallel",)),
        name="cmp_attn_sample",
    )(part, pe, w2bd, q16ext, g8, ov_bf16)


def _topk_kernel(sc_ref, idx_ref, *, n_blk, q_blk, k_sel):
    imps = sc_ref[...]
    nbp = imps.shape[0]
    sidx = lax.broadcasted_iota(jnp.int32, (nbp, 1), 0)
    sf = sidx.astype(jnp.float32)
    forced = (sidx == 0) | (sidx == q_blk) | (sidx == q_blk - 1)
    causal = (sidx <= q_blk) & (sidx < n_blk)
    score = jnp.where(forced, -NEG, jnp.where(causal, imps, NEG))
    rows = []
    for _ in range(k_sel):
        m = jnp.max(score, axis=0, keepdims=True)
        pick = jnp.min(jnp.where(score == m, sf, float(nbp)), axis=0, keepdims=True)
        rows.append(jnp.where(m > 0.5 * NEG, pick, -1.0))
        score = jnp.where(sf == pick, 2.0 * NEG, score)
    idx_ref[...] = jnp.concatenate(rows, axis=0).astype(jnp.int32)


def topk_blocks(scores, *, n_blk, q_blk, k_sel):
    nbp, lanes = scores.shape
    return pl.pallas_call(
        functools.partial(_topk_kernel, n_blk=n_blk, q_blk=q_blk, k_sel=k_sel),
        out_shape=jax.ShapeDtypeStruct((k_sel, lanes), jnp.int32),
        name="topk_blocks",
    )(scores)


def _slc_sample_kernel(idx_ref, pt_ref, *refs, n_sel, new_blk):
    blk_refs = refs[:N_KV * n_sel]
    q_ref, knew_ref, vnew_ref, o_ref = refs[N_KV * n_sel:]
    b = pl.program_id(0)
    page = blk_refs[0].shape[-1]
    lane = lax.broadcasted_iota(jnp.int32, (1, n_sel * page), 1)
    for kvh in range(N_KV):
        kts, vts = [], []
        live = lane < 0
        has_new = False
        for n in range(n_sel):
            blk = blk_refs[kvh * n_sel + n]
            kts.append(blk[0, 0, 0])
            vts.append(blk[0, 1, 0])
            s_n = idx_ref[(b * N_KV + kvh) * n_sel + n]
            lo = n * page + (s_n % (page // L_SLC)) * L_SLC
            live = live | ((lane >= lo) & (lane < lo + L_SLC) & (s_n >= 0) & (s_n != new_blk))
            has_new = jnp.logical_or(has_new, s_n == new_blk)
        kt = jnp.concatenate(kts, axis=1).astype(jnp.bfloat16)
        vt = jnp.concatenate(vts, axis=1).astype(jnp.bfloat16)
        q = q_ref[0, kvh]
        s = jnp.dot(q.astype(jnp.bfloat16), kt, preferred_element_type=jnp.float32) * SCALE
        s = jnp.where(live, s, NEG)
        s_new = jnp.where(has_new, jnp.sum(q * knew_ref[0, kvh], axis=-1, keepdims=True) * SCALE, NEG)
        m = jnp.maximum(jnp.max(s, axis=-1, keepdims=True), s_new)
        ex = jnp.where(live, jnp.exp(s - m), 0.0)
        ex_new = jnp.where(has_new, jnp.exp(s_new - m), 0.0)
        denom = jnp.maximum(jnp.sum(ex, axis=-1, keepdims=True) + ex_new, 1e-30)
        o_ref[0, kvh] = (_dot_nt(ex.astype(jnp.bfloat16), vt) + ex_new * vnew_ref[0, kvh]) / denom


def slc_sample(idx_flat, page_table, pool_t, q4, knew, vnew, *, n_sel, new_blk):
    b, n_pages = page_table.shape
    page = pool_t.shape[-1]
    bpp = page // L_SLC

    def bspec(kvh, n):
        def imap(bi, idx, pt):
            s = jnp.clip(idx[(bi * N_KV + kvh) * n_sel + n], 0, bpp * n_pages - 1)
            return (pt[bi, s // bpp], 0, kvh, 0, 0)
        return pl.BlockSpec((1, 2, 1, HEAD_DIM, page), imap)

    small = lambda r: pl.BlockSpec((1, N_KV, r, HEAD_DIM), lambda bi, idx, pt: (bi, 0, 0, 0))
    grid_spec = pltpu.PrefetchScalarGridSpec(
        num_scalar_prefetch=2,
        grid=(b,),
        in_specs=[bspec(kvh, n) for kvh in range(N_KV) for n in range(n_sel)] + [small(8), small(1), small(1)],
        out_specs=small(8),
    )
    return pl.pallas_call(
        functools.partial(_slc_sample_kernel, n_sel=n_sel, new_blk=new_blk),
        grid_spec=grid_spec,
        out_shape=jax.ShapeDtypeStruct((b, N_KV, 8, HEAD_DIM), jnp.float32),
        compiler_params=_cp(("arbitrary",)),
        name="slc_sample",
    )(idx_flat, page_table, *([pool_t] * (N_KV * n_sel)), q4, knew, vnew)


def _win_sample_kernel(w_ref, q_ref, o_ref):
    w = w_ref[0]
    s = _dot_nt(q_ref[0].astype(jnp.bfloat16), w[:, :KV_W].astype(jnp.bfloat16)) * SCALE
    m = jnp.max(s, axis=-1, keepdims=True)
    ex = jnp.exp(s - m)
    p = ex / jnp.maximum(jnp.sum(ex, axis=-1, keepdims=True), 1e-30)
    o_ref[0] = _fold_heads(_bdot(p, w[:, KV_W:]))


def win_sample(win_rows, q16ext):
    b, nk, _ = win_rows.shape
    return pl.pallas_call(
        _win_sample_kernel,
        grid=(b,),
        in_specs=[pl.BlockSpec((1, nk, 2 * KV_W), lambda bi: (bi, 0, 0)),
                  pl.BlockSpec((1, N_HEADS, KV_W), lambda bi: (bi, 0, 0))],
        out_specs=pl.BlockSpec((1, N_HEADS, HEAD_DIM), lambda bi: (bi, 0, 0)),
        out_shape=jax.ShapeDtypeStruct((b, N_HEADS, HEAD_DIM), jnp.float32),
        compiler_params=_cp(("parallel",)),
        name="win_sample",
    )(win_rows, q16ext)


def _prep_weights(w_in, mu, w2, a2, g2):
    o3 = 3 * D_RWKV
    nsa0 = R_COLS
    cols = [w_in[:, nsa0:nsa0 + D_NSA],
            w_in[:, 0:o3],
            w_in[:, nsa0 + D_NSA:nsa0 + D_NSA + 6 * KV_W],
            w_in[:, o3:R_COLS],
            w_in[:, nsa0 + D_NSA + 6 * KV_W:],
            jnp.zeros((D_MODEL, D_IN_PAD - C_GATE - 3 * N_HEADS), w_in.dtype)]
    w_in_p = jnp.concatenate(cols, axis=1).astype(jnp.bfloat16)
    z = jnp.zeros((LORA_ALL, D_RWKV), jnp.float32)
    return dict(w_in_p=w_in_p, mu_p=mu,
                w2p=z.at[0:LORA_W].set(w2), a2p=z.at[LORA_W:LORA_W + LORA_A].set(a2),
                g2p=z.at[LORA_W + LORA_A:].set(g2))


def _layer_weights(layer, g_mix, w_in, mu, w0, w2, a0, a2, g2, k_k, k_a, r_k, lnx_w, lnx_b, phi_pe, phi_w1,
                   phi_w2, w_o, g_ffn, w_gate, w_up, w_down):
    w = _prep_weights(w_in[layer], mu[layer], w2[layer], a2[layer], g2[layer])
    w1t, pe8, w2k, w2vt, w2bd = _phi_weights(phi_pe[layer], phi_w1[layer], phi_w2[layer])
    w1x = _expand_w1(phi_w1[layer])
    w.update(g_mix=g_mix[layer], w0=w0[layer], a0=a0[layer], k_k=k_k[layer], k_a=k_a[layer], r_k=r_k[layer],
             lnx_w=lnx_w[layer], lnx_b=lnx_b[layer], w1x=w1x, w1x_bf16=w1x.astype(jnp.bfloat16),
             pe=pe_term(pe8, w1t), w2k=w2k, w2vt=w2vt, w2bd=w2bd,
             gate_expand=_gate_expand_matrix(), wo=w_o[layer].astype(jnp.bfloat16), g_ffn=g_ffn[layer],
             wg=w_gate[layer].astype(jnp.bfloat16), wu=w_up[layer].astype(jnp.bfloat16),
             wd=w_down[layer].astype(jnp.bfloat16))
    return w


def _pick(n, pref):
    while n % pref:
        pref //= 2
    return pref


def _prompt_layer(x_prompt, W, g_final):
    B, T, _ = x_prompt.shape
    M = B * T
    x2 = x_prompt.reshape(M, D_MODEL)
    proj = proj_matmul(x2, W["g_mix"], W["w_in_p"], apply_norm=True, tm=_pick(M, 1024), tn=1024)
    proj3 = proj.reshape(B, T, D_IN_PAD)
    ro, rg, wkv = rwkv_mix(proj, jnp.zeros((B, 1, R_COLS), jnp.float32), W,
                           jnp.zeros((B, H_R, 64, 64), jnp.float32), b=B, t=T, tm=_pick(T, 256), tc=_pick(T, 64))
    cmp_kv = proj3[..., C_CMP:C_CMP + 2 * KV_W]
    slc_kv = proj3[..., C_SLC:C_SLC + 2 * KV_W]
    win_kv = proj3[..., C_WIN:C_WIN + 2 * KV_W]
    nch = T // D_CMP
    part = cmp_part_prompt(cmp_kv.reshape(B, nch, D_CMP * 2 * KV_W), W["w1x"])
    ovt = _overlap_t(128, nch - 1, nch).astype(jnp.bfloat16)
    o_cmp, sel = cmp_select_prompt(part, W["pe"], W["w2k"], W["w2vt"], proj3, ovt, tq=128)
    ks, vs = _kv_heads(slc_kv, B, T)
    kw, vw = _kv_blocks(win_kv, B, T)
    o_slc, o_win = slc_win_prompt(proj3, ks, vs, kw, vw, sel, tq=128)
    r2 = lambda z: z.reshape(M, 1024)
    x1 = out_proj(r2(ro), r2(rg), r2(o_cmp), r2(o_slc), r2(o_win), proj, W["gate_expand"], x2, W["wo"],
                  tm=_pick(M, 512), tn=1024)
    y = ffn_final(x1, W["g_ffn"], W["wg"], W["wu"], W["wd"], g_final, tm=_pick(M, 512), tf=512)
    kv5 = lambda z: z.reshape(B, T, 2, N_KV, HEAD_DIM)
    nwin = min(WINDOW, T)
    return y.reshape(B, T, D_MODEL), kv5(cmp_kv), kv5(slc_kv), kv5(win_kv)[:, T - nwin:], wkv


def _sample_layer(x_sample, xn_s, cache_cmp, cache_slc, cache_win, wkv0, shift, page_table, W, g_final):
    b = x_sample.shape[0]
    n_pool, page = cache_cmp.shape[0], cache_cmp.shape[1]
    n_pages = page_table.shape[1]
    past_len = n_pages * page
    n_rows = -(-(past_len + 1) // L_SLC) * L_SLC
    n_blk = n_rows // L_SLC
    q_blk = past_len // L_SLC
    nbp = -(-n_blk // 128) * 128

    rows = jnp.concatenate([xn_s, shift], axis=0)
    proj = proj_matmul(rows, W["g_mix"], W["w_in_p"], apply_norm=False, tm=2 * b, tn=1024)
    prev = jnp.concatenate([proj[b:, C_R:C_R + 3 * D_RWKV], proj[b:, C_LORA:C_LORA + LORA_ALL]], axis=1)
    ro, rg, wkv = rwkv_mix(proj[:b], prev, W, wkv0, b=b, t=1, tm=1, tc=1)

    q = proj[:b, C_Q:C_Q + D_NSA].reshape(b, N_KV, GQA, 1, HEAD_DIM)
    eye = jnp.eye(N_KV, dtype=jnp.float32).reshape(1, N_KV, 1, N_KV, 1)
    qext = (q * eye).reshape(b, N_KV, GQA, KV_W)
    q16ext = qext.reshape(b, N_HEADS, KV_W)
    cmp_new = proj[:b, C_CMP:C_CMP + 2 * KV_W]
    slc_new = proj[:b, C_SLC:C_SLC + 2 * KV_W]
    win_new = proj[:b, C_WIN:C_WIN + 2 * KV_W]

    part = cmp_part_sample(cache_cmp.transpose(0, 2, 3, 4, 1), page_table, W["w1x_bf16"], pp=_pick(n_pages, 32))
    nc_rows = part.shape[1]
    ov = _overlap_t(nbp, n_rows // D_CMP - L_CMP // D_CMP + 1, nc_rows)[:n_blk].T
    ov = jnp.pad(ov, ((0, 0), (0, nbp - n_blk)))
    g8 = jnp.asarray((np.arange(8)[:, None] == np.arange(N_HEADS)[None, :] // GQA).astype(np.float32))
    o_cmp, imps = cmp_attn_sample(part, W["pe"], W["w2bd"], q16ext, g8, ov.astype(jnp.bfloat16), q_pos=past_len)
    scores = imps[:, :N_KV].transpose(2, 0, 1).reshape(nbp, b * N_KV)
    k_sel = min(N_SELECT, n_blk)
    idx = topk_blocks(scores, n_blk=n_blk, q_blk=q_blk, k_sel=k_sel)
    q4 = jnp.pad(q.reshape(b, N_KV, GQA, HEAD_DIM), ((0, 0), (0, 0), (0, 8 - GQA), (0, 0)))
    slc_new5 = slc_new.reshape(b, 2, N_KV, 1, HEAD_DIM)
    o_slc = slc_sample(idx.T.reshape(-1), page_table, cache_slc.transpose(0, 2, 3, 4, 1), q4,
                       slc_new5[:, 0], slc_new5[:, 1], n_sel=k_sel, new_blk=q_blk)
    o_slc = o_slc[:, :, :GQA].reshape(b, D_NSA)

    n_buf = cache_win.shape[1]
    win_all = jnp.concatenate([cache_win, win_new.reshape(b, 1, 2, N_KV, HEAD_DIM)], axis=1)
    win_keep = win_all[:, n_buf + 1 - min(WINDOW, n_buf + 1):]
    o_win = win_sample(win_keep.reshape(b, -1, 2 * KV_W), q16ext)

    x2 = x_sample.reshape(b, D_MODEL)
    x1 = out_proj(ro.reshape(b, -1), rg.reshape(b, -1), o_cmp.reshape(b, D_NSA), o_slc, o_win.reshape(b, D_NSA),
                  proj, W["gate_expand"], x2, W["wo"], tm=b, tn=1024)
    y = ffn_final(x1, W["g_ffn"], W["wg"], W["wu"], W["wd"], g_final, tm=b, tf=512)
    kv5 = lambda z: z.reshape(b, 1, 2, N_KV, HEAD_DIM)
    return y.reshape(b, 1, D_MODEL), kv5(cmp_new), kv5(slc_new), win_keep, wkv


def kernel(x_prompt, x_sample, cache_cmp_kv, cache_slc_kv, cache_win_kv, state_wkv, state_shift, page_table, g_mix, w_in, mu, w0, w2, a0, a2, g2, k_k, k_a, r_k, lnx_w, lnx_b, phi_pe, phi_w1, phi_w2, w_o, g_ffn, w_gate, w_up, w_down, g_final):
    W = _layer_weights(0, g_mix, w_in, mu, w0, w2, a0, a2, g2, k_k, k_a, r_k, lnx_w, lnx_b, phi_pe, phi_w1,
                       phi_w2, w_o, g_ffn, w_gate, w_up, w_down)
    bp, bs = x_prompt.shape[0], x_sample.shape[0]
    small = jnp.concatenate([x_sample.reshape(bs, D_MODEL), x_prompt[:, -1]], axis=0)
    pad = (-small.shape[0]) % 8
    xn_small = rmsnorm_rows(jnp.pad(small, ((0, pad), (0, 0))), W["g_mix"])
    xn_s, shift_p = xn_small[:bs], xn_small[bs:bs + bp]

    y_p, cmp_p, slc_p, win_p, wkv_p = _prompt_layer(x_prompt, W, g_final)
    y_s, cmp_s, slc_s, win_s, wkv_s = _sample_layer(x_sample, xn_s, cache_cmp_kv[0], cache_slc_kv[0],
                                                    cache_win_kv[0], state_wkv[0], state_shift[0], page_table,
                                                    W, g_final)
    return (y_p, y_s, cmp_p[None], slc_p[None], win_p[None], wkv_p[None], shift_p[None],
            cmp_s[None], slc_s[None], win_s[None], wkv_s[None], xn_s[None])
```

```python
import functools

import jax
import jax.numpy as jnp
import numpy as np
from jax import lax
from jax.experimental import pallas as pl
from jax.experimental.pallas import tpu as pltpu

D_MODEL = 2048
D_RWKV = 1024
D_NSA = 1024
HEAD_DIM_R = 64
H_R = 16
LORA_W = 96
LORA_A = 96
LORA_G = 64
LORA_ALL = LORA_W + LORA_A + LORA_G
HEAD_DIM = 64
N_HEADS = 16
N_KV = 4
GQA = 4
KV_W = 256
L_CMP = 32
D_CMP = 16
L_SLC = 64
N_SELECT = 16
WINDOW = 512
PHI_HIDDEN = 64
R_COLS = 3 * D_RWKV + LORA_ALL
RMS_EPS = 1e-6
LNX_EPS = 64e-5
SCALE = HEAD_DIM ** -0.5
NEG = -1e30

C_Q, C_R, C_K, C_V = 0, 1024, 2048, 3072
C_CMP, C_SLC, C_WIN = 4096, 4608, 5120
C_LORA, C_GATE = 5632, 5888
D_IN_PAD = 6144

VMEM_LIMIT = 48 * 1024 * 1024
HI = lax.Precision.HIGHEST


def _cp(sem, limit=VMEM_LIMIT):
    return pltpu.CompilerParams(dimension_semantics=sem, vmem_limit_bytes=limit)


def _bdot(a, b):
    return jnp.dot(a.astype(jnp.bfloat16), b.astype(jnp.bfloat16), preferred_element_type=jnp.float32)


def _hdot(a, b):
    return jnp.dot(a, b, precision=HI, preferred_element_type=jnp.float32)


def _dot_nt(a, b, precision=None):
    return lax.dot_general(a, b, (((1,), (1,)), ((), ())), precision=precision,
                           preferred_element_type=jnp.float32)


def _sigmoid(x):
    return 1.0 / (1.0 + jnp.exp(-x))


def _silu(x):
    return x * _sigmoid(x)


def _rmsnorm_rows_kernel(x_ref, g_ref, o_ref):
    x = x_ref[...]
    ms = jnp.mean(x * x, axis=-1, keepdims=True)
    o_ref[...] = x * lax.rsqrt(ms + RMS_EPS) * g_ref[...]


def rmsnorm_rows(x, g):
    m, d = x.shape
    return pl.pallas_call(
        _rmsnorm_rows_kernel,
        out_shape=jax.ShapeDtypeStruct((m, d), jnp.float32),
        name="rmsnorm_rows",
    )(x, g.reshape(1, d))


def _proj_kernel(x_ref, g_ref, w_ref, o_ref, xn_ref, *, apply_norm):
    @pl.when(pl.program_id(1) == 0)
    def _():
        x = x_ref[...]
        if apply_norm:
            ms = jnp.mean(x * x, axis=-1, keepdims=True)
            x = x * lax.rsqrt(ms + RMS_EPS) * g_ref[...]
        xn_ref[...] = x.astype(jnp.bfloat16)

    o_ref[...] = jnp.dot(xn_ref[...], w_ref[...], preferred_element_type=jnp.float32)


def proj_matmul(x, g, w_bf16, *, apply_norm, tm, tn):
    m, d = x.shape
    n = w_bf16.shape[1]
    return pl.pallas_call(
        functools.partial(_proj_kernel, apply_norm=apply_norm),
        grid=(m // tm, n // tn),
        in_specs=[
            pl.BlockSpec((tm, d), lambda i, j: (i, 0)),
            pl.BlockSpec((1, d), lambda i, j: (0, 0)),
            pl.BlockSpec((d, tn), lambda i, j: (0, j)),
        ],
        out_specs=pl.BlockSpec((tm, tn), lambda i, j: (i, j)),
        out_shape=jax.ShapeDtypeStruct((m, n), jnp.float32),
        scratch_shapes=[pltpu.VMEM((tm, d), jnp.bfloat16)],
        compiler_params=_cp(("parallel", "arbitrary")),
        name="proj_matmul",
    )(x, g.reshape(1, d), w_bf16)


def _rwkv_prep_kernel(pr_ref, pk_ref, pv_ref, pl_ref, prev_ref, mu_ref, w0_ref, a0_ref, kk_ref, ka_ref,
                      w2_ref, a2_ref, g2_ref,
                      r_out, w_out, k_out, v_out, kk_out, a_out, g_out, carry_ref, *, row_prev):
    tm = pr_ref.shape[1]
    pr, pk, pv, plo = pr_ref[0], pk_ref[0], pv_ref[0], pl_ref[0]
    if row_prev:
        def shift_mix(p, lo, hi):
            return p + (prev_ref[0, :, lo:hi] - p) * mu_ref[:, lo:hi]
    else:
        @pl.when(pl.program_id(1) == 0)
        def _():
            carry_ref[...] = prev_ref[0]

        row0 = lax.broadcasted_iota(jnp.int32, (tm, 1), 0) == 0

        def shift_mix(p, lo, hi):
            prev = pltpu.roll(p, 1, axis=0)
            prev = jnp.where(row0, carry_ref[:, lo:hi], prev)
            return p + (prev - p) * mu_ref[:, lo:hi]

    xr = shift_mix(pr, 0, 1024)
    xk = shift_mix(pk, 1024, 2048)
    xv = shift_mix(pv, 2048, 3072)
    xl = shift_mix(plo, 3072, 3328)
    if not row_prev:
        last = tm - 1
        carry_ref[:, 0:1024] = pr[last:last + 1]
        carry_ref[:, 1024:2048] = pk[last:last + 1]
        carry_ref[:, 2048:3072] = pv[last:last + 1]
        carry_ref[:, 3072:3328] = plo[last:last + 1]

    wl = _dot3(_split2(jnp.tanh(xl)), _split2(w2_ref[...]))
    al = _dot3(_split2(xl), _split2(a2_ref[...]))
    gl = _dot3(_split2(_sigmoid(xl)), _split2(g2_ref[...]))
    z = -(w0_ref[...] + wl)
    softplus = jnp.maximum(z, 0.0) + jnp.log(1.0 + jnp.exp(-jnp.abs(z)))
    w_log = -softplus - 0.5
    decay = jnp.exp(-jnp.exp(w_log))
    a = _sigmoid(a0_ref[...] + al)
    r_out[0] = xr
    w_out[0] = decay
    k_out[0] = xk * (1.0 + (a - 1.0) * ka_ref[...])
    v_out[0] = xv
    kk_out[0] = xk * kk_ref[...]
    a_out[0] = a
    g_out[0] = gl


def rwkv_prep(proj, prev, mu_p, w0, a0, k_k, k_a, w2p, a2p, g2p, *, b, t, tm, row_prev=False):
    proj3 = proj.reshape(b, t, D_IN_PAD)
    nt = t // tm
    prev_spec = (pl.BlockSpec((1, tm, R_COLS), lambda bi, ti: (bi, ti, 0)) if row_prev
                 else pl.BlockSpec((1, 1, R_COLS), lambda bi, ti: (bi, 0, 0)))
    row = lambda blk, w: pl.BlockSpec((1, tm, w), lambda bi, ti: (bi, ti, blk))
    vec = lambda w: pl.BlockSpec((1, w), lambda bi, ti: (0, 0))
    mat = pl.BlockSpec((LORA_ALL, D_RWKV), lambda bi, ti: (0, 0))
    out = pl.BlockSpec((1, tm, D_RWKV), lambda bi, ti: (bi, ti, 0))
    shp = jax.ShapeDtypeStruct((b, t, D_RWKV), jnp.float32)
    return pl.pallas_call(
        functools.partial(_rwkv_prep_kernel, row_prev=row_prev),
        grid=(b, nt),
        in_specs=[row(C_R // 1024, 1024), row(C_K // 1024, 1024), row(C_V // 1024, 1024),
                  row(C_LORA // 256, 256),
                  prev_spec,
                  vec(R_COLS), vec(D_RWKV), vec(D_RWKV), vec(D_RWKV), vec(D_RWKV),
                  mat, mat, mat],
        out_specs=[out] * 7,
        out_shape=[shp] * 7,
        scratch_shapes=[pltpu.VMEM((1, R_COLS), jnp.float32)],
        compiler_params=_cp(("parallel", "arbitrary")),
        name="rwkv_prep",
    )(proj3, proj3, proj3, proj3, prev, mu_p.reshape(1, -1), w0.reshape(1, -1), a0.reshape(1, -1),
      k_k.reshape(1, -1), k_a.reshape(1, -1), w2p, a2p, g2p)


def _rwkv_scan_kernel(r_ref, w_ref, k_ref, kk_ref, a_ref, v_ref, s0_ref, rk_ref, lnw_ref, lnb_ref,
                      o_ref, sfin_ref, s_ref, kkn_ref, b_ref, y_ref, *, halved):
    tc = r_ref.shape[1]
    ni = v_ref.shape[2]

    @pl.when(pl.program_id(1) == 0)
    def _():
        s_ref[...] = s0_ref[0]

    kkraw = kk_ref[0]
    nrm = jnp.sqrt(jnp.sum(kkraw * kkraw, axis=1, keepdims=True))
    kkn = kkraw / jnp.maximum(nrm, 1e-12)
    kkn_ref[...] = kkn
    b_ref[...] = kkn * a_ref[0]

    def step(t, carry):
        w_t = w_ref[0, t]
        k_t = k_ref[0, t]
        r_t = r_ref[0, t]
        kk_t = kkn_ref[t]
        b_t = b_ref[t]
        for i in range(ni):
            s = s_ref[i]
            sa = -jnp.sum(s * kk_t, axis=0, keepdims=True)
            v_i = v_ref[0, t, pl.ds(i, 1), :]
            s = s * w_t + sa * b_t + v_i * k_t
            y_ref[t, pl.ds(i, 1), :] = jnp.sum(s * r_t, axis=0, keepdims=True)
            s_ref[i] = s
        return carry

    lax.fori_loop(0, tc, step, 0)

    y = y_ref[...]
    v = v_ref[0]

    def head_sum(z):
        s = jnp.sum(z, axis=1, keepdims=True)
        if halved:
            s = s + pltpu.roll(s, 64, axis=2)
        return s

    mean = head_sum(y) * (1.0 / HEAD_DIM_R)
    d = y - mean
    var = head_sum(d * d) * (1.0 / HEAD_DIM_R)
    yn = d * lax.rsqrt(var + LNX_EPS) * lnw_ref[0] + lnb_ref[0]
    bonus = jnp.sum(r_ref[0] * k_ref[0] * rk_ref[0], axis=1, keepdims=True)
    o_ref[0] = yn + bonus * v

    @pl.when(pl.program_id(1) == pl.num_programs(1) - 1)
    def _():
        sfin_ref[0] = s_ref[...]


def rwkv_scan(r, w, k, kk, a, v, s0, rk, lnw, lnb, *, tc, halved):
    g, t = r.shape[0], r.shape[1]
    ni = v.shape[2]
    col = pl.BlockSpec((1, tc, 64, 128), lambda gi, ti: (gi, ti, 0, 0))
    vspec = pl.BlockSpec((1, tc, ni, 128), lambda gi, ti: (gi, ti, 0, 0))
    sspec = pl.BlockSpec((1, ni, 64, 128), lambda gi, ti: (gi, 0, 0, 0))
    return pl.pallas_call(
        functools.partial(_rwkv_scan_kernel, halved=halved),
        grid=(g, t // tc),
        in_specs=[col, col, col, col, col, vspec, sspec,
                  pl.BlockSpec((1, 64, 128), lambda gi, ti: (gi, 0, 0)),
                  pl.BlockSpec((1, ni, 128), lambda gi, ti: (gi, 0, 0)),
                  pl.BlockSpec((1, ni, 128), lambda gi, ti: (gi, 0, 0))],
        out_specs=[vspec, sspec],
        out_shape=[jax.ShapeDtypeStruct((g, t, ni, 128), jnp.float32),
                   jax.ShapeDtypeStruct((g, ni, 64, 128), jnp.float32)],
        scratch_shapes=[pltpu.VMEM((ni, 64, 128), jnp.float32),
                        pltpu.VMEM((tc, 64, 128), jnp.float32),
                        pltpu.VMEM((tc, 64, 128), jnp.float32),
                        pltpu.VMEM((tc, ni, 128), jnp.float32)],
        compiler_params=_cp(("parallel", "arbitrary")),
        name="rwkv_scan",
    )(r, w, k, kk, a, v, s0, rk, lnw, lnb)


def _to_scan_layout(x, b, t, halved):
    z = x.reshape(b, t, H_R, 64).transpose(1, 3, 0, 2).reshape(t, 64, b * H_R)
    if halved:
        return jnp.concatenate([z, z], axis=-1)[None]
    g = (b * H_R) // 128
    return z.reshape(t, 64, g, 128).transpose(2, 0, 1, 3)


def _v_to_scan_layout(x, b, t, halved):
    if halved:
        return x.reshape(b, t, H_R, 2, 32).transpose(1, 4, 3, 0, 2).reshape(1, t, 32, 128)
    g = (b * H_R) // 128
    z = x.reshape(b, t, H_R, 64).transpose(1, 3, 0, 2).reshape(t, 64, g, 128)
    return z.transpose(2, 0, 1, 3)


def _vecparam_scan_layout(p, b, halved):
    z = jnp.broadcast_to(p.reshape(1, H_R, 64), (b, H_R, 64))
    if halved:
        return z.reshape(b, H_R, 2, 32).transpose(3, 2, 0, 1).reshape(1, 32, 128)
    g = (b * H_R) // 128
    return z.transpose(2, 0, 1).reshape(64, g, 128).transpose(1, 0, 2)


def _keyparam_scan_layout(p, b, halved):
    z = jnp.broadcast_to(p.reshape(1, H_R, 64), (b, H_R, 64)).transpose(2, 0, 1).reshape(64, b * H_R)
    if halved:
        return jnp.concatenate([z, z], axis=-1)[None]
    g = (b * H_R) // 128
    return z.reshape(64, g, 128).transpose(1, 0, 2)


def _from_scan_layout(o, b, t, halved):
    if halved:
        return o.reshape(t, 32, 2, b, H_R).transpose(3, 0, 4, 2, 1).reshape(b, t, D_RWKV)
    g = o.shape[0]
    return o.transpose(1, 2, 0, 3).reshape(t, 64, b, H_R).transpose(2, 0, 3, 1).reshape(b, t, D_RWKV)


def _state_to_scan_layout(s, b, halved):
    if halved:
        return s.reshape(b, H_R, 2, 32, 64).transpose(3, 4, 2, 0, 1).reshape(1, 32, 64, 128)
    g = (b * H_R) // 128
    return s.reshape(g, 128, 64, 64).transpose(0, 2, 3, 1)


def _state_from_scan_layout(s, b, halved):
    if halved:
        return s.reshape(32, 64, 2, b, H_R).transpose(3, 4, 2, 0, 1).reshape(b, H_R, 64, 64)
    return s.transpose(0, 3, 1, 2).reshape(b, H_R, 64, 64)


def rwkv_mix(proj, prev, wts, s0, *, b, t, tm, tc):
    halved = (b * H_R == 64)
    if t == 1:
        outs = rwkv_prep(proj, prev.reshape(1, b, R_COLS), wts["mu_p"], wts["w0"], wts["a0"], wts["k_k"],
                         wts["k_a"], wts["w2p"], wts["a2p"], wts["g2p"], b=1, t=b, tm=b, row_prev=True)
        r, w, k, v, kk, a, g = [z.reshape(b, 1, D_RWKV) for z in outs]
    else:
        r, w, k, v, kk, a, g = rwkv_prep(proj, prev, wts["mu_p"], wts["w0"], wts["a0"], wts["k_k"], wts["k_a"],
                                         wts["w2p"], wts["a2p"], wts["g2p"], b=b, t=t, tm=tm)
    tr = lambda z: _to_scan_layout(z, b, t, halved)
    o, sfin = rwkv_scan(tr(r), tr(w), tr(k), tr(kk), tr(a), _v_to_scan_layout(v, b, t, halved),
                        _state_to_scan_layout(s0, b, halved),
                        _keyparam_scan_layout(wts["r_k"], b, halved),
                        _vecparam_scan_layout(wts["lnx_w"], b, halved),
                        _vecparam_scan_layout(wts["lnx_b"], b, halved),
                        tc=tc, halved=halved)
    return _from_scan_layout(o, b, t, halved), g, _state_from_scan_layout(sfin, b, halved)


def _cmp_part_kernel(*refs, n_in, precise, prefetch):
    refs = refs[prefetch:]
    x_refs, w_ref, o_ref = refs[:n_in], refs[n_in], refs[n_in + 1]
    for e in range(2):
        for hp in range(2):
            acc = None
            for j in range(D_CMP):
                lo = j * 512 + e * 256 + hp * 128
                xs = [xr[0, :, lo:lo + 128] for xr in x_refs]
                x = xs[0] if n_in == 1 else jnp.concatenate(xs, axis=0)
                d = _hdot(x, w_ref[e, j]) if precise else _bdot(x, w_ref[e, j])
                acc = d if acc is None else acc + d
            c0 = e * 256 + hp * 128
            o_ref[0, :, c0:c0 + 128] = acc[:, :128]
            o_ref[0, :, 512 + c0:512 + c0 + 128] = acc[:, 128:]


def cmp_part_prompt(rows, w1x):
    b, nch, _ = rows.shape
    return pl.pallas_call(
        functools.partial(_cmp_part_kernel, n_in=1, precise=True, prefetch=0),
        grid=(b,),
        in_specs=[pl.BlockSpec((1, nch, 8192), lambda bi: (bi, 0, 0)),
                  pl.BlockSpec((2, D_CMP, 128, 256), lambda bi: (0, 0, 0, 0))],
        out_specs=pl.BlockSpec((1, nch, 1024), lambda bi: (bi, 0, 0)),
        out_shape=jax.ShapeDtypeStruct((b, nch, 1024), jnp.float32),
        compiler_params=_cp(("parallel",)),
        name="cmp_part_prompt",
    )(rows, w1x)


def _expand_w1(phi_w1):
    w1 = phi_w1.reshape(2, 2, D_CMP, HEAD_DIM, PHI_HIDDEN)
    eye = jnp.eye(2, dtype=phi_w1.dtype)
    w = jnp.einsum("ab,esjdf->ejadsbf", eye, w1)
    return w.reshape(2, D_CMP, 128, 256)


def _split2(x):
    hi = x.astype(jnp.bfloat16)
    return hi, (x - hi.astype(jnp.float32)).astype(jnp.bfloat16)


def _dot3(a, b):
    (ah, al), (bh, bl) = a, b
    d = lambda x, y: jnp.dot(x, y, preferred_element_type=jnp.float32)
    return d(ah, bh) + (d(ah, bl) + d(al, bh))


def _pe_term_kernel(pe_ref, w1t_ref, o_ref):
    for e in range(2):
        o_ref[e] = _hdot(pe_ref[e], w1t_ref[e])


def pe_term(pe8, w1t):
    return pl.pallas_call(
        _pe_term_kernel,
        out_shape=jax.ShapeDtypeStruct((2, 8, N_KV * PHI_HIDDEN), jnp.float32),
        name="pe_term",
    )(pe8, w1t)


def _cmp_hidden(part, pe_ref, e):
    n = part.shape[0]
    lo = e * 256
    h = part[:, lo:lo + 256] + pltpu.roll(part[:, 512 + lo:512 + lo + 256], n - 1, axis=0)
    return _silu(h + pe_ref[e, 0:1])


def _cmp_sel_kernel(part_ref, pe_ref, w2k_ref, w2vt_ref, q_ref, ovt_ref, o_ref, sel_ref,
                    kch_ref, kcl_ref, vct_ref, *, tq, n_blk):
    ti = pl.program_id(1)

    @pl.when(ti == 0)
    def _():
        part = part_ref[0]
        hk = _split2(_cmp_hidden(part, pe_ref, 0))
        hv = _cmp_hidden(part, pe_ref, 1)
        for kvh in range(N_KV):
            kch_ref[kvh], kcl_ref[kvh] = _split2(_dot3(hk, _split2(w2k_ref[kvh])))
            vct_ref[kvh] = _dot_nt(w2vt_ref[kvh], hv, HI).astype(jnp.bfloat16)

    nc = kch_ref.shape[1]
    nbp = -(-n_blk // 8) * 8
    k_sel = min(N_SELECT, n_blk)
    qpos = ti * tq + lax.broadcasted_iota(jnp.int32, (1, tq), 1)
    cidx = lax.broadcasted_iota(jnp.int32, (nc, 1), 0)
    cmask = (cidx * D_CMP + (L_CMP - 1)) <= qpos
    sidx = cidx[:nbp]
    qblk = qpos // L_SLC
    forced = (sidx == 0) | (sidx == qblk) | (sidx == qblk - 1)
    causal = sidx <= qblk
    pad = jnp.zeros((128 - nbp, tq), jnp.float32)
    need_rank = (ti + 1) * tq > k_sel * L_SLC

    cmask4 = jnp.concatenate([cmask.astype(jnp.float32)] * GQA, axis=1) > 0.5
    for kvh in range(N_KV):
        qt = (q_ref[0, :, kvh * 256:(kvh + 1) * 256] * SCALE).T
        qt4 = _split2(jnp.concatenate([qt[g * 64:(g + 1) * 64] for g in range(GQA)], axis=1))
        st = _dot3((kch_ref[kvh], kcl_ref[kvh]), qt4)
        st = jnp.where(cmask4, st, NEG)
        m = jnp.max(st, axis=0, keepdims=True)
        ex = jnp.where(cmask4, jnp.exp(st - m), 0.0)
        pt = ex / jnp.maximum(jnp.sum(ex, axis=0, keepdims=True), 1e-30)
        ot = jnp.dot(vct_ref[kvh], pt.astype(jnp.bfloat16), preferred_element_type=jnp.float32)
        o_ref[0, :, kvh * 256:(kvh + 1) * 256] = jnp.concatenate(
            [ot[:, g * tq:(g + 1) * tq] for g in range(GQA)], axis=0).T
        impt = pt[:, 0:tq]
        for g in range(1, GQA):
            impt = impt + pt[:, g * tq:(g + 1) * tq]

        @pl.when(need_rank)
        def _():
            ih, il = _split2(impt)
            ov = ovt_ref[...]
            imps = (jnp.dot(ov, ih, preferred_element_type=jnp.float32)
                    + jnp.dot(ov, il, preferred_element_type=jnp.float32))[:nbp]
            score = jnp.where(forced, -NEG, jnp.where(causal, imps, NEG))
            rank = jnp.zeros((nbp, tq), jnp.float32)
            for s2 in range(n_blk):
                row = score[s2:s2 + 1]
                beats = (row > score) | ((row == score) & (s2 < sidx))
                rank = rank + beats.astype(jnp.float32)
            seln = ((rank < k_sel) & causal).astype(jnp.float32)
            sel_ref[0, kvh] = jnp.concatenate([seln, pad], axis=0)

        @pl.when(jnp.logical_not(need_rank))
        def _():
            sel_ref[0, kvh] = jnp.concatenate([causal.astype(jnp.float32), pad], axis=0)


def cmp_select_prompt(part, pe, w2k, w2vt, proj3, ovt_bf16, *, tq):
    b, nc, _ = part.shape
    t = proj3.shape[1]
    c3 = lambda bi, ti: (0, 0, 0)
    return pl.pallas_call(
        functools.partial(_cmp_sel_kernel, tq=tq, n_blk=t // L_SLC),
        grid=(b, t // tq),
        in_specs=[pl.BlockSpec((1, nc, 1024), lambda bi, ti: (bi, 0, 0)),
                  pl.BlockSpec((2, 8, 256), c3),
                  pl.BlockSpec((N_KV, 256, 64), c3),
                  pl.BlockSpec((N_KV, 64, 256), c3),
                  pl.BlockSpec((1, tq, 1024), lambda bi, ti: (bi, ti, C_Q // 1024)),
                  pl.BlockSpec((128, nc), lambda bi, ti: (0, 0))],
        out_specs=[pl.BlockSpec((1, tq, 1024), lambda bi, ti: (bi, ti, 0)),
                   pl.BlockSpec((1, N_KV, 128, tq), lambda bi, ti: (bi, 0, 0, ti))],
        out_shape=[jax.ShapeDtypeStruct((b, t, 1024), jnp.float32),
                   jax.ShapeDtypeStruct((b, N_KV, 128, t), jnp.float32)],
        scratch_shapes=[pltpu.VMEM((N_KV, nc, 64), jnp.bfloat16),
                        pltpu.VMEM((N_KV, nc, 64), jnp.bfloat16),
                        pltpu.VMEM((N_KV, 64, nc), jnp.bfloat16)],
        compiler_params=_cp(("parallel", "arbitrary")),
        name="cmp_select_prompt",
    )(part, pe, w2k, w2vt, proj3, ovt_bf16)


def _overlap_t(n_blocks_pad, nc, nc_pad):
    c_start = np.arange(nc_pad) * D_CMP
    s_start = np.arange(n_blocks_pad) * L_SLC
    ov = (c_start[None, :] < s_start[:, None] + L_SLC) & (c_start[None, :] + L_CMP > s_start[:, None])
    ov &= (np.arange(nc_pad) < nc)[None, :]
    return jnp.asarray(ov.astype(np.float32))


def _phi_weights(phi_pe, phi_w1, phi_w2):
    w1t = jnp.tile(phi_w1, (1, 1, N_KV))
    pe8 = jnp.broadcast_to(phi_pe.reshape(2, 1, L_CMP * HEAD_DIM), (2, 8, L_CMP * HEAD_DIM))
    z = jnp.zeros((N_KV, N_KV, PHI_HIDDEN, HEAD_DIM), jnp.float32)
    idx = jnp.arange(N_KV)
    w2k = z.at[idx, idx].set(phi_w2[0]).reshape(N_KV, 256, 64)
    w2vt = jnp.transpose(z.at[idx, idx].set(phi_w2[1]), (0, 3, 1, 2)).reshape(N_KV, 64, 256)
    eye = jnp.eye(N_KV, dtype=jnp.float32)
    w2bd = jnp.einsum("ab,efd->eafbd", eye, phi_w2).reshape(2, 256, 256)
    return w1t, pe8, w2k, w2vt, w2bd


def _attend_t(k, vt, qt4, bias, tq):
    s = jnp.dot(k, qt4, preferred_element_type=jnp.float32) + jnp.concatenate([bias] * GQA, axis=1)
    s = s.astype(jnp.bfloat16)
    p = jnp.exp(s - jnp.max(s, axis=0, keepdims=True))
    vt1 = jnp.concatenate([vt, jnp.ones((16, vt.shape[1]), jnp.bfloat16)], axis=0)
    ol = jnp.dot(vt1, p, preferred_element_type=jnp.float32)
    o = ol[:HEAD_DIM] / jnp.maximum(ol[HEAD_DIM:HEAD_DIM + 1], 1e-30)
    return jnp.concatenate([o[:, g * tq:(g + 1) * tq] for g in range(GQA)], axis=0).T


SLC_KEY_STEP = 512
WIN_BLOCKS = WINDOW // 128 + 1


def _slc_win_kernel(q_ref, ks_ref, vs_ref, kw_ref, vw_ref, sel_ref, ex_ref, os_ref, ow_ref, *, tq):
    ti = pl.program_id(2)
    t = ks_ref.shape[2]
    qt = (q_ref[0] * SCALE).T.astype(jnp.bfloat16)
    qt4 = jnp.concatenate([qt[g * HEAD_DIM:(g + 1) * HEAD_DIM] for g in range(GQA)], axis=1)
    qpos = ti * tq + lax.broadcasted_iota(jnp.int32, (1, tq), 1)
    selt = sel_ref[0, 0].astype(jnp.bfloat16)

    n_var = -(-t // SLC_KEY_STEP)
    for c in range(n_var):
        nk = min((c + 1) * SLC_KEY_STEP, t)

        @pl.when((ti * tq) // SLC_KEY_STEP == c)
        def _():
            on = jnp.dot(ex_ref[0:nk], selt, preferred_element_type=jnp.float32)
            kpos = lax.broadcasted_iota(jnp.int32, (nk, 1), 0)
            bias = (on - 1.0) * (-NEG) + jnp.where(kpos <= qpos, 0.0, NEG)
            os_ref[0] = _attend_t(ks_ref[0, 0, 0:nk], vs_ref[0, 0, :, 0:nk], qt4, bias, tq)

    nkb = kw_ref.shape[2]
    w0 = jnp.clip(ti - (WIN_BLOCKS - 1), 0, nkb - WIN_BLOCKS)
    kw = kw_ref[0, 0, pl.ds(w0, WIN_BLOCKS)].reshape(WIN_BLOCKS * 128, HEAD_DIM)
    vwt = jnp.concatenate([vw_ref[0, 0, w0 + i] for i in range(WIN_BLOCKS)], axis=1)
    diff = qpos - (w0 * 128 + lax.broadcasted_iota(jnp.int32, (WIN_BLOCKS * 128, 1), 0))
    ow_ref[0] = _attend_t(kw, vwt, qt4, jnp.where((diff >= 0) & (diff < WINDOW), 0.0, NEG), tq)


def slc_win_prompt(proj3, ks, vst, kw, vwt, selt, *, tq):
    b, t, _ = proj3.shape
    nkb = t // 128
    expand = jnp.asarray((np.arange(t)[:, None] // L_SLC == np.arange(128)[None, :]).astype(np.float32),
                         jnp.bfloat16)
    i4 = lambda bi, hi, ti: (bi, hi, 0, 0)
    i5 = lambda bi, hi, ti: (bi, hi, 0, 0, 0)
    out = pl.BlockSpec((1, tq, 256), lambda bi, hi, ti: (bi, ti, hi))
    shp = jax.ShapeDtypeStruct((b, t, 1024), jnp.float32)
    return pl.pallas_call(
        functools.partial(_slc_win_kernel, tq=tq),
        grid=(b, N_KV, t // tq),
        in_specs=[pl.BlockSpec((1, tq, 256), lambda bi, hi, ti: (bi, ti, hi)),
                  pl.BlockSpec((1, 1, t, HEAD_DIM), i4),
                  pl.BlockSpec((1, 1, HEAD_DIM, t), i4),
                  pl.BlockSpec((1, 1, nkb, 128, HEAD_DIM), i5),
                  pl.BlockSpec((1, 1, nkb, HEAD_DIM, 128), i5),
                  pl.BlockSpec((1, 1, 128, tq), lambda bi, hi, ti: (bi, hi, 0, ti)),
                  pl.BlockSpec((t, 128), lambda bi, hi, ti: (0, 0))],
        out_specs=[out, out],
        out_shape=[shp, shp],
        compiler_params=_cp(("parallel", "parallel", "arbitrary")),
        name="slc_win_prompt",
    )(proj3, ks, vst, kw, vwt, selt, expand)


def _kv_heads(kv_rows, b, t):
    kv = kv_rows.reshape(b, t, 2, N_KV, HEAD_DIM).astype(jnp.bfloat16)
    return kv[:, :, 0].transpose(0, 2, 1, 3), kv[:, :, 1].transpose(0, 2, 3, 1)


def _kv_blocks(kv_rows, b, t):
    k, vt = _kv_heads(kv_rows, b, t)
    return (k.reshape(b, N_KV, t // 128, 128, HEAD_DIM),
            vt.reshape(b, N_KV, HEAD_DIM, t // 128, 128).transpose(0, 1, 3, 2, 4))


def _out_proj_kernel(ro_ref, rg_ref, oc_ref, os_ref, ow_ref, gt_ref, ex_ref, x_ref, wo_ref, y_ref, a_ref):
    @pl.when(pl.program_id(1) == 0)
    def _():
        gh, gl = _split2(_sigmoid(gt_ref[:, 0:128]))
        ex = ex_ref[...]
        gates = (jnp.dot(gh, ex, preferred_element_type=jnp.float32)
                 + jnp.dot(gl, ex, preferred_element_type=jnp.float32))
        nsa = (gates[:, 0:1024] * oc_ref[...] + gates[:, 1024:2048] * os_ref[...]
               + gates[:, 2048:3072] * ow_ref[...])
        a_ref[:, 0:1024] = (ro_ref[...] * rg_ref[...]).astype(jnp.bfloat16)
        a_ref[:, 1024:2048] = nsa.astype(jnp.bfloat16)

    y_ref[...] = x_ref[...] + jnp.dot(a_ref[...], wo_ref[...], preferred_element_type=jnp.float32)


def out_proj(ro, rg, oc, osl, ow, proj, gate_expand, x, wo_bf16, *, tm, tn):
    m = x.shape[0]
    row = lambda w: pl.BlockSpec((tm, w), lambda i, j: (i, 0))
    return pl.pallas_call(
        _out_proj_kernel,
        grid=(m // tm, D_MODEL // tn),
        in_specs=[row(1024), row(1024), row(1024), row(1024), row(1024),
                  pl.BlockSpec((tm, 256), lambda i, j: (i, C_GATE // 256)),
                  pl.BlockSpec((128, 3072), lambda i, j: (0, 0)),
                  pl.BlockSpec((tm, tn), lambda i, j: (i, j)),
                  pl.BlockSpec((D_MODEL, tn), lambda i, j: (0, j))],
        out_specs=pl.BlockSpec((tm, tn), lambda i, j: (i, j)),
        out_shape=jax.ShapeDtypeStruct((m, D_MODEL), jnp.float32),
        scratch_shapes=[pltpu.VMEM((tm, D_MODEL), jnp.bfloat16)],
        compiler_params=_cp(("parallel", "arbitrary")),
        name="out_proj",
    )(ro, rg, oc, osl, ow, proj, gate_expand, x, wo_bf16)


def _gate_expand_matrix():
    e = np.zeros((128, 3 * D_NSA), np.float32)
    for k in range(3):
        for h in range(N_HEADS):
            e[k * N_HEADS + h, k * D_NSA + h * HEAD_DIM:k * D_NSA + (h + 1) * HEAD_DIM] = 1.0
    return jnp.asarray(e, jnp.bfloat16)


def _ffn_kernel(x_ref, gf_ref, wg_ref, wu_ref, wd_ref, gl_ref, y_ref, h_ref, acc_ref):
    f = pl.program_id(1)

    @pl.when(f == 0)
    def _():
        x = x_ref[...]
        ms = jnp.mean(x * x, axis=-1, keepdims=True)
        h_ref[...] = (x * lax.rsqrt(ms + RMS_EPS) * gf_ref[...]).astype(jnp.bfloat16)
        acc_ref[...] = jnp.zeros_like(acc_ref)

    h = h_ref[...]
    gate = jnp.dot(h, wg_ref[...], preferred_element_type=jnp.float32)
    up = jnp.dot(h, wu_ref[...], preferred_element_type=jnp.float32)
    act = (_silu(gate) * up).astype(jnp.bfloat16)
    acc_ref[...] += jnp.dot(act, wd_ref[...], preferred_element_type=jnp.float32)

    @pl.when(f == pl.num_programs(1) - 1)
    def _():
        z = x_ref[...] + acc_ref[...]
        ms = jnp.mean(z * z, axis=-1, keepdims=True)
        y_ref[...] = z * lax.rsqrt(ms + RMS_EPS) * gl_ref[...]


def ffn_final(x, g_ffn, wg, wu, wd, g_final, *, tm, tf):
    m = x.shape[0]
    dff = wg.shape[1]
    return pl.pallas_call(
        _ffn_kernel,
        grid=(m // tm, dff // tf),
        in_specs=[pl.BlockSpec((tm, D_MODEL), lambda i, f: (i, 0)),
                  pl.BlockSpec((1, D_MODEL), lambda i, f: (0, 0)),
                  pl.BlockSpec((D_MODEL, tf), lambda i, f: (0, f)),
                  pl.BlockSpec((D_MODEL, tf), lambda i, f: (0, f)),
                  pl.BlockSpec((tf, D_MODEL), lambda i, f: (f, 0)),
                  pl.BlockSpec((1, D_MODEL), lambda i, f: (0, 0))],
        out_specs=pl.BlockSpec((tm, D_MODEL), lambda i, f: (i, 0)),
        out_shape=jax.ShapeDtypeStruct((m, D_MODEL), jnp.float32),
        scratch_shapes=[pltpu.VMEM((tm, D_MODEL), jnp.bfloat16), pltpu.VMEM((tm, D_MODEL), jnp.float32)],
        compiler_params=_cp(("parallel", "arbitrary")),
        name="ffn_final",
    )(x, g_ffn.reshape(1, -1), wg, wu, wd, g_final.reshape(1, -1))


def _cmp_part_paged_kernel(pt_ref, *refs, n_in):
    x_refs = refs[:n_in]
    perm_ref, w_ref, o_ref, xs_ref = refs[n_in:]
    cpp = x_refs[0].shape[-1] // D_CMP
    n_pair = 2 * (N_KV // 2)
    for p in range(n_in):
        x = x_refs[p][0].reshape(2 * N_KV * HEAD_DIM, x_refs[p].shape[-1])
        xp_all = jnp.dot(x.astype(jnp.bfloat16), perm_ref[...], preferred_element_type=jnp.float32)
        for ep in range(n_pair):
            xp = xp_all[ep * 128:(ep + 1) * 128].T
            for j in range(D_CMP):
                xs_ref[ep, j, p * cpp:(p + 1) * cpp, :] = xp[j * cpp:(j + 1) * cpp]
    for e in range(2):
        for hp in range(2):
            acc = None
            for j in range(0, D_CMP, 2):
                x2 = jnp.concatenate([xs_ref[e * 2 + hp, j], xs_ref[e * 2 + hp, j + 1]], axis=1)
                w2 = jnp.concatenate([w_ref[e, j], w_ref[e, j + 1]], axis=0)
                d = _bdot(x2, w2)
                acc = d if acc is None else acc + d
            c0 = e * 256 + hp * 128
            o_ref[0, :, c0:c0 + 128] = acc[:, :128]
            o_ref[0, :, 512 + c0:512 + c0 + 128] = acc[:, 128:]


def cmp_part_sample(pool_t, page_table, w1x_bf16, *, pp):
    b, n_pages = page_table.shape
    page = pool_t.shape[-1]
    cpp = page // D_CMP

    def xspec(k):
        return pl.BlockSpec((1, 2, N_KV, HEAD_DIM, page), lambda bi, gi, pt: (pt[bi, gi * pp + k], 0, 0, 0, 0))

    r = np.arange(page)
    perm = np.zeros((page, page), np.float32)
    perm[r, (r % D_CMP) * cpp + r // D_CMP] = 1.0
    grid_spec = pltpu.PrefetchScalarGridSpec(
        num_scalar_prefetch=1,
        grid=(b, n_pages // pp),
        in_specs=[xspec(k) for k in range(pp)]
        + [pl.BlockSpec((page, page), lambda bi, gi, pt: (0, 0)),
           pl.BlockSpec((2, D_CMP, 128, 256), lambda bi, gi, pt: (0, 0, 0, 0))],
        out_specs=pl.BlockSpec((1, cpp * pp, 1024), lambda bi, gi, pt: (bi, gi, 0)),
        scratch_shapes=[pltpu.VMEM((N_KV, D_CMP, pp * cpp, 128), jnp.float32)],
    )
    return pl.pallas_call(
        functools.partial(_cmp_part_paged_kernel, n_in=pp),
        grid_spec=grid_spec,
        out_shape=jax.ShapeDtypeStruct((b, n_pages * cpp, 1024), jnp.float32),
        compiler_params=_cp(("parallel", "arbitrary")),
        name="cmp_part_sample",
    )(page_table, *([pool_t] * pp), jnp.asarray(perm, jnp.bfloat16), w1x_bf16)


def _fold_heads(o_ext):
    row_kvh = lax.broadcasted_iota(jnp.int32, (N_HEADS, 1), 0) // GQA
    out = jnp.zeros((N_HEADS, HEAD_DIM), jnp.float32)
    for kvh in range(N_KV):
        out = out + jnp.where(row_kvh == kvh, o_ext[:, kvh * HEAD_DIM:(kvh + 1) * HEAD_DIM], 0.0)
    return out


def _cmp_attn_sample_kernel(part_ref, pe_ref, w2bd_ref, q_ref, g8_ref, ov_ref, o_ref, imps_ref, *, q_pos):
    part = part_ref[0]
    n = part.shape[0]
    kc = _split2(_dot3(_split2(_cmp_hidden(part, pe_ref, 0)), _split2(w2bd_ref[0])))
    vc = _bdot(_cmp_hidden(part, pe_ref, 1), w2bd_ref[1])
    qh, ql = _split2(q_ref[0])
    s = (_dot_nt(qh, kc[0]) + (_dot_nt(qh, kc[1]) + _dot_nt(ql, kc[0]))) * SCALE
    cidx = lax.broadcasted_iota(jnp.int32, (1, n), 1)
    cmask = (cidx * D_CMP + (L_CMP - 1)) <= q_pos
    s = jnp.where(cmask, s, NEG)
    m = jnp.max(s, axis=-1, keepdims=True)
    ex = jnp.where(cmask, jnp.exp(s - m), 0.0)
    p = ex / jnp.maximum(jnp.sum(ex, axis=-1, keepdims=True), 1e-30)
    o_ref[0] = _fold_heads(_bdot(p, vc))
    ih, il = _split2(_hdot(g8_ref[...], p))
    ov = ov_ref[...]
    imps_ref[0] = (jnp.dot(ih, ov, preferred_element_type=jnp.float32)
                   + jnp.dot(il, ov, preferred_element_type=jnp.float32))


def cmp_attn_sample(part, pe, w2bd, q16ext, g8, ov_bf16, *, q_pos):
    b, n, _ = part.shape
    nbp = ov_bf16.shape[1]
    c3 = lambda bi: (0, 0, 0)
    return pl.pallas_call(
        functools.partial(_cmp_attn_sample_kernel, q_pos=q_pos),
        grid=(b,),
        in_specs=[pl.BlockSpec((1, n, 1024), lambda bi: (bi, 0, 0)),
                  pl.BlockSpec((2, 8, 256), c3),
                  pl.BlockSpec((2, 256, 256), c3),
                  pl.BlockSpec((1, N_HEADS, 256), lambda bi: (bi, 0, 0)),
                  pl.BlockSpec((8, N_HEADS), lambda bi: (0, 0)),
                  pl.BlockSpec((n, nbp), lambda bi: (0, 0))],
        out_specs=[pl.BlockSpec((1, N_HEADS, HEAD_DIM), lambda bi: (bi, 0, 0)),
                   pl.BlockSpec((1, 8, nbp), lambda bi: (bi, 0, 0))],
        out_shape=[jax.ShapeDtypeStruct((b, N_HEADS, HEAD_DIM), jnp.float32),
                   jax.ShapeDtypeStruct((b, 8, nbp), jnp.float32)],
        compiler_params=_cp(("parallel",)),
        name="cmp_attn_sample",
    )(part, pe, w2bd, q16ext, g8, ov_bf16)


def _topk_kernel(sc_ref, idx_ref, *, n_blk, q_blk, k_sel):
    imps = sc_ref[...]
    nbp = imps.shape[0]
    sidx = lax.broadcasted_iota(jnp.int32, (nbp, 1), 0)
    sf = sidx.astype(jnp.float32)
    forced = (sidx == 0) | (sidx == q_blk) | (sidx == q_blk - 1)
    causal = (sidx <= q_blk) & (sidx < n_blk)
    score = jnp.where(forced, -NEG, jnp.where(causal, imps, NEG))
    rows = []
    for _ in range(k_sel):
        m = jnp.max(score, axis=0, keepdims=True)
        pick = jnp.min(jnp.where(score == m, sf, float(nbp)), axis=0, keepdims=True)
        rows.append(jnp.where(m > 0.5 * NEG, pick, -1.0))
        score = jnp.where(sf == pick, 2.0 * NEG, score)
    idx_ref[...] = jnp.concatenate(rows, axis=0).astype(jnp.int32)


def topk_blocks(scores, *, n_blk, q_blk, k_sel):
    nbp, lanes = scores.shape
    return pl.pallas_call(
        functools.partial(_topk_kernel, n_blk=n_blk, q_blk=q_blk, k_sel=k_sel),
        out_shape=jax.ShapeDtypeStruct((k_sel, lanes), jnp.int32),
        name="topk_blocks",
    )(scores)


def _slc_sample_kernel(idx_ref, pt_ref, *refs, n_sel, new_blk):
    blk_refs = refs[:N_KV * n_sel]
    q_ref, knew_ref, vnew_ref, o_ref = refs[N_KV * n_sel:]
    b = pl.program_id(0)
    page = blk_refs[0].shape[-1]
    lane = lax.broadcasted_iota(jnp.int32, (1, n_sel * page), 1)
    for kvh in range(N_KV):
        kts, vts = [], []
        live = lane < 0
        has_new = False
        for n in range(n_sel):
            blk = blk_refs[kvh * n_sel + n]
            kts.append(blk[0, 0, 0])
            vts.append(blk[0, 1, 0])
            s_n = idx_ref[(b * N_KV + kvh) * n_sel + n]
            lo = n * page + (s_n % (page // L_SLC)) * L_SLC
            live = live | ((lane >= lo) & (lane < lo + L_SLC) & (s_n >= 0) & (s_n != new_blk))
            has_new = jnp.logical_or(has_new, s_n == new_blk)
        kt = jnp.concatenate(kts, axis=1).astype(jnp.bfloat16)
        vt = jnp.concatenate(vts, axis=1).astype(jnp.bfloat16)
        q = q_ref[0, kvh]
        s = jnp.dot(q.astype(jnp.bfloat16), kt, preferred_element_type=jnp.float32) * SCALE
        s = jnp.where(live, s, NEG)
        s_new = jnp.where(has_new, jnp.sum(q * knew_ref[0, kvh], axis=-1, keepdims=True) * SCALE, NEG)
        m = jnp.maximum(jnp.max(s, axis=-1, keepdims=True), s_new)
        ex = jnp.where(live, jnp.exp(s - m), 0.0)
        ex_new = jnp.where(has_new, jnp.exp(s_new - m), 0.0)
        denom = jnp.maximum(jnp.sum(ex, axis=-1, keepdims=True) + ex_new, 1e-30)
        o_ref[0, kvh] = (_dot_nt(ex.astype(jnp.bfloat16), vt) + ex_new * vnew_ref[0, kvh]) / denom


def slc_sample(idx_flat, page_table, pool_t, q4, knew, vnew, *, n_sel, new_blk):
    b, n_pages = page_table.shape
    page = pool_t.shape[-1]
    bpp = page // L_SLC

    def bspec(kvh, n):
        def imap(bi, idx, pt):
            s = jnp.clip(idx[(bi * N_KV + kvh) * n_sel + n], 0, bpp * n_pages - 1)
            return (pt[bi, s // bpp], 0, kvh, 0, 0)
        return pl.BlockSpec((1, 2, 1, HEAD_DIM, page), imap)

    small = lambda r: pl.BlockSpec((1, N_KV, r, HEAD_DIM), lambda bi, idx, pt: (bi, 0, 0, 0))
    grid_spec = pltpu.PrefetchScalarGridSpec(
        num_scalar_prefetch=2,
        grid=(b,),
        in_specs=[bspec(kvh, n) for kvh in range(N_KV) for n in range(n_sel)] + [small(8), small(1), small(1)],
        out_specs=small(8),
    )
    return pl.pallas_call(
        functools.partial(_slc_sample_kernel, n_sel=n_sel, new_blk=new_blk),
        grid_spec=grid_spec,
        out_shape=jax.ShapeDtypeStruct((b, N_KV, 8, HEAD_DIM), jnp.float32),
        compiler_params=_cp(("arbitrary",)),
        name="slc_sample",
    )(idx_flat, page_table, *([pool_t] * (N_KV * n_sel)), q4, knew, vnew)


def _win_sample_kernel(w_ref, q_ref, o_ref):
    w = w_ref[0]
    s = _dot_nt(q_ref[0].astype(jnp.bfloat16), w[:, :KV_W].astype(jnp.bfloat16)) * SCALE
    m = jnp.max(s, axis=-1, keepdims=True)
    ex = jnp.exp(s - m)
    p = ex / jnp.maximum(jnp.sum(ex, axis=-1, keepdims=True), 1e-30)
    o_ref[0] = _fold_heads(_bdot(p, w[:, KV_W:]))


def win_sample(win_rows, q16ext):
    b, nk, _ = win_rows.shape
    return pl.pallas_call(
        _win_sample_kernel,
        grid=(b,),
        in_specs=[pl.BlockSpec((1, nk, 2 * KV_W), lambda bi: (bi, 0, 0)),
                  pl.BlockSpec((1, N_HEADS, KV_W), lambda bi: (bi, 0, 0))],
        out_specs=pl.BlockSpec((1, N_HEADS, HEAD_DIM), lambda bi: (bi, 0, 0)),
        out_shape=jax.ShapeDtypeStruct((b, N_HEADS, HEAD_DIM), jnp.float32),
        compiler_params=_cp(("parallel",)),
        name="win_sample",
    )(win_rows, q16ext)


def _prep_weights(w_in, mu, w2, a2, g2):
    o3 = 3 * D_RWKV
    nsa0 = R_COLS
    cols = [w_in[:, nsa0:nsa0 + D_NSA],
            w_in[:, 0:o3],
            w_in[:, nsa0 + D_NSA:nsa0 + D_NSA + 6 * KV_W],
            w_in[:, o3:R_COLS],
            w_in[:, nsa0 + D_NSA + 6 * KV_W:],
            jnp.zeros((D_MODEL, D_IN_PAD - C_GATE - 3 * N_HEADS), w_in.dtype)]
    w_in_p = jnp.concatenate(cols, axis=1).astype(jnp.bfloat16)
    z = jnp.zeros((LORA_ALL, D_RWKV), jnp.float32)
    return dict(w_in_p=w_in_p, mu_p=mu,
                w2p=z.at[0:LORA_W].set(w2), a2p=z.at[LORA_W:LORA_W + LORA_A].set(a2),
                g2p=z.at[LORA_W + LORA_A:].set(g2))


def _layer_weights(layer, g_mix, w_in, mu, w0, w2, a0, a2, g2, k_k, k_a, r_k, lnx_w, lnx_b, phi_pe, phi_w1,
                   phi_w2, w_o, g_ffn, w_gate, w_up, w_down):
    w = _prep_weights(w_in[layer], mu[layer], w2[layer], a2[layer], g2[layer])
    w1t, pe8, w2k, w2vt, w2bd = _phi_weights(phi_pe[layer], phi_w1[layer], phi_w2[layer])
    w1x = _expand_w1(phi_w1[layer])
    w.update(g_mix=g_mix[layer], w0=w0[layer], a0=a0[layer], k_k=k_k[layer], k_a=k_a[layer], r_k=r_k[layer],
             lnx_w=lnx_w[layer], lnx_b=lnx_b[layer], w1x=w1x, w1x_bf16=w1x.astype(jnp.bfloat16),
             pe=pe_term(pe8, w1t), w2k=w2k, w2vt=w2vt, w2bd=w2bd,
             gate_expand=_gate_expand_matrix(), wo=w_o[layer].astype(jnp.bfloat16), g_ffn=g_ffn[layer],
             wg=w_gate[layer].astype(jnp.bfloat16), wu=w_up[layer].astype(jnp.bfloat16),
             wd=w_down[layer].astype(jnp.bfloat16))
    return w


def _pick(n, pref):
    while n % pref:
        pref //= 2
    return pref


def _prompt_layer(x_prompt, W, g_final):
    B, T, _ = x_prompt.shape
    M = B * T
    x2 = x_prompt.reshape(M, D_MODEL)
    proj = proj_matmul(x2, W["g_mix"], W["w_in_p"], apply_norm=True, tm=_pick(M, 1024), tn=1024)
    proj3 = proj.reshape(B, T, D_IN_PAD)
    ro, rg, wkv = rwkv_mix(proj, jnp.zeros((B, 1, R_COLS), jnp.float32), W,
                           jnp.zeros((B, H_R, 64, 64), jnp.float32), b=B, t=T, tm=_pick(T, 256), tc=_pick(T, 64))
    cmp_kv = proj3[..., C_CMP:C_CMP + 2 * KV_W]
    slc_kv = proj3[..., C_SLC:C_SLC + 2 * KV_W]
    win_kv = proj3[..., C_WIN:C_WIN + 2 * KV_W]
    nch = T // D_CMP
    part = cmp_part_prompt(cmp_kv.reshape(B, nch, D_CMP * 2 * KV_W), W["w1x"])
    ovt = _overlap_t(128, nch - 1, nch).astype(jnp.bfloat16)
    o_cmp, sel = cmp_select_prompt(part, W["pe"], W["w2k"], W["w2vt"], proj3, ovt, tq=128)
    ks, vs = _kv_heads(slc_kv, B, T)
    kw, vw = _kv_blocks(win_kv, B, T)
    o_slc, o_win = slc_win_prompt(proj3, ks, vs, kw, vw, sel, tq=128)
    r2 = lambda z: z.reshape(M, 1024)
    x1 = out_proj(r2(ro), r2(rg), r2(o_cmp), r2(o_slc), r2(o_win), proj, W["gate_expand"], x2, W["wo"],
                  tm=_pick(M, 512), tn=1024)
    y = ffn_final(x1, W["g_ffn"], W["wg"], W["wu"], W["wd"], g_final, tm=_pick(M, 512), tf=512)
    kv5 = lambda z: z.reshape(B, T, 2, N_KV, HEAD_DIM)
    nwin = min(WINDOW, T)
    return y.reshape(B, T, D_MODEL), kv5(cmp_kv), kv5(slc_kv), kv5(win_kv)[:, T - nwin:], wkv


def _sample_layer(x_sample, xn_s, cache_cmp, cache_slc, cache_win, wkv0, shift, page_table, W, g_final):
    b = x_sample.shape[0]
    n_pool, page = cache_cmp.shape[0], cache_cmp.shape[1]
    n_pages = page_table.shape[1]
    past_len = n_pages * page
    n_rows = -(-(past_len + 1) // L_SLC) * L_SLC
    n_blk = n_rows // L_SLC
    q_blk = past_len // L_SLC
    nbp = -(-n_blk // 128) * 128

    rows = jnp.concatenate([xn_s, shift], axis=0)
    proj = proj_matmul(rows, W["g_mix"], W["w_in_p"], apply_norm=False, tm=2 * b, tn=1024)
    prev = jnp.concatenate([proj[b:, C_R:C_R + 3 * D_RWKV], proj[b:, C_LORA:C_LORA + LORA_ALL]], axis=1)
    ro, rg, wkv = rwkv_mix(proj[:b], prev, W, wkv0, b=b, t=1, tm=1, tc=1)

    q = proj[:b, C_Q:C_Q + D_NSA].reshape(b, N_KV, GQA, 1, HEAD_DIM)
    eye = jnp.eye(N_KV, dtype=jnp.float32).reshape(1, N_KV, 1, N_KV, 1)
    qext = (q * eye).reshape(b, N_KV, GQA, KV_W)
    q16ext = qext.reshape(b, N_HEADS, KV_W)
    cmp_new = proj[:b, C_CMP:C_CMP + 2 * KV_W]
    slc_new = proj[:b, C_SLC:C_SLC + 2 * KV_W]
    win_new = proj[:b, C_WIN:C_WIN + 2 * KV_W]

    part = cmp_part_sample(cache_cmp.transpose(0, 2, 3, 4, 1), page_table, W["w1x_bf16"], pp=_pick(n_pages, 32))
    nc_rows = part.shape[1]
    ov = _overlap_t(nbp, n_rows // D_CMP - L_CMP // D_CMP + 1, nc_rows)[:n_blk].T
    ov = jnp.pad(ov, ((0, 0), (0, nbp - n_blk)))
    g8 = jnp.asarray((np.arange(8)[:, None] == np.arange(N_HEADS)[None, :] // GQA).astype(np.float32))
    o_cmp, imps = cmp_attn_sample(part, W["pe"], W["w2bd"], q16ext, g8, ov.astype(jnp.bfloat16), q_pos=past_len)
    scores = imps[:, :N_KV].transpose(2, 0, 1).reshape(nbp, b * N_KV)
    k_sel = min(N_SELECT, n_blk)
    idx = topk_blocks(scores, n_blk=n_blk, q_blk=q_blk, k_sel=k_sel)
    q4 = jnp.pad(q.reshape(b, N_KV, GQA, HEAD_DIM), ((0, 0), (0, 0), (0, 8 - GQA), (0, 0)))
    slc_new5 = slc_new.reshape(b, 2, N_KV, 1, HEAD_DIM)
    o_slc = slc_sample(idx.T.reshape(-1), page_table, cache_slc.transpose(0, 2, 3, 4, 1), q4,
                       slc_new5[:, 0], slc_new5[:, 1], n_sel=k_sel, new_blk=q_blk)
    o_slc = o_slc[:, :, :GQA].reshape(b, D_NSA)

    n_buf = cache_win.shape[1]
    win_all = jnp.concatenate([cache_win, win_new.reshape(b, 1, 2, N_KV, HEAD_DIM)], axis=1)
    win_keep = win_all[:, n_buf + 1 - min(WINDOW, n_buf + 1):]
    o_win = win_sample(win_keep.reshape(b, -1, 2 * KV_W), q16ext)

    x2 = x_sample.reshape(b, D_MODEL)
    x1 = out_proj(ro.reshape(b, -1), rg.reshape(b, -1), o_cmp.reshape(b, D_NSA), o_slc, o_win.reshape(b, D_NSA),
                  proj, W["gate_expand"], x2, W["wo"], tm=b, tn=1024)
    y = ffn_final(x1, W["g_ffn"], W["wg"], W["wu"], W["wd"], g_final, tm=b, tf=512)
    kv5 = lambda z: z.reshape(b, 1, 2, N_KV, HEAD_DIM)
    return y.reshape(b, 1, D_MODEL), kv5(cmp_new), kv5(slc_new), win_keep, wkv


def kernel(x_prompt, x_sample, cache_cmp_kv, cache_slc_kv, cache_win_kv, state_wkv, state_shift, page_table, g_mix, w_in, mu, w0, w2, a0, a2, g2, k_k, k_a, r_k, lnx_w, lnx_b, phi_pe, phi_w1, phi_w2, w_o, g_ffn, w_gate, w_up, w_down, g_final):
    W = _layer_weights(0, g_mix, w_in, mu, w0, w2, a0, a2, g2, k_k, k_a, r_k, lnx_w, lnx_b, phi_pe, phi_w1,
                       phi_w2, w_o, g_ffn, w_gate, w_up, w_down)
    bp, bs = x_prompt.shape[0], x_sample.shape[0]
    small = jnp.concatenate([x_sample.reshape(bs, D_MODEL), x_prompt[:, -1]], axis=0)
    pad = (-small.shape[0]) % 8
    xn_small = rmsnorm_rows(jnp.pad(small, ((0, pad), (0, 0))), W["g_mix"])
    xn_s, shift_p = xn_small[:bs], xn_small[bs:bs + bp]

    y_p, cmp_p, slc_p, win_p, wkv_p = _prompt_layer(x_prompt, W, g_final)
    y_s, cmp_s, slc_s, win_s, wkv_s = _sample_layer(x_sample, xn_s, cache_cmp_kv[0], cache_slc_kv[0],
                                                    cache_win_kv[0], state_wkv[0], state_shift[0], page_table,
                                                    W, g_final)
    return (y_p, y_s, cmp_p[None], slc_p[None], win_p[None], wkv_p[None], shift_p[None],
            cmp_s[None], slc_s[None], win_s[None], wkv_s[None], xn_s[None])
```

```python
import functools

import jax
import jax.numpy as jnp
import numpy as np
from jax import lax
from jax.experimental import pallas as pl
from jax.experimental.pallas import tpu as pltpu

D_MODEL = 2048
D_RWKV = 1024
D_NSA = 1024
HEAD_DIM_R = 64
H_R = 16
LORA_W = 96
LORA_A = 96
LORA_G = 64
LORA_ALL = LORA_W + LORA_A + LORA_G
HEAD_DIM = 64
N_HEADS = 16
N_KV = 4
GQA = 4
KV_W = 256
L_CMP = 32
D_CMP = 16
L_SLC = 64
N_SELECT = 16
WINDOW = 512
PHI_HIDDEN = 64
R_COLS = 3 * D_RWKV + LORA_ALL
RMS_EPS = 1e-6
LNX_EPS = 64e-5
SCALE = HEAD_DIM ** -0.5
NEG = -1e30

C_Q, C_R, C_K, C_V = 0, 1024, 2048, 3072
C_CMP, C_SLC, C_WIN = 4096, 4608, 5120
C_LORA, C_GATE = 5632, 5888
D_IN_PAD = 6144

VMEM_LIMIT = 48 * 1024 * 1024
HI = lax.Precision.HIGHEST


def _cp(sem, limit=VMEM_LIMIT):
    return pltpu.CompilerParams(dimension_semantics=sem, vmem_limit_bytes=limit)


def _bdot(a, b):
    return jnp.dot(a.astype(jnp.bfloat16), b.astype(jnp.bfloat16), preferred_element_type=jnp.float32)


def _hdot(a, b):
    return jnp.dot(a, b, precision=HI, preferred_element_type=jnp.float32)


def _dot_nt(a, b, precision=None):
    return lax.dot_general(a, b, (((1,), (1,)), ((), ())), precision=precision,
                           preferred_element_type=jnp.float32)


def _sigmoid(x):
    return 1.0 / (1.0 + jnp.exp(-x))


def _silu(x):
    return x * _sigmoid(x)


def _rmsnorm_rows_kernel(x_ref, g_ref, o_ref):
    x = x_ref[...]
    ms = jnp.mean(x * x, axis=-1, keepdims=True)
    o_ref[...] = x * lax.rsqrt(ms + RMS_EPS) * g_ref[...]


def rmsnorm_rows(x, g):
    m, d = x.shape
    return pl.pallas_call(
        _rmsnorm_rows_kernel,
        out_shape=jax.ShapeDtypeStruct((m, d), jnp.float32),
        name="rmsnorm_rows",
    )(x, g.reshape(1, d))


def _proj_kernel(x_ref, g_ref, w_ref, o_ref, xn_ref, *, apply_norm):
    @pl.when(pl.program_id(1) == 0)
    def _():
        x = x_ref[...]
        if apply_norm:
            ms = jnp.mean(x * x, axis=-1, keepdims=True)
            x = x * lax.rsqrt(ms + RMS_EPS) * g_ref[...]
        xn_ref[...] = x.astype(jnp.bfloat16)

    o_ref[...] = jnp.dot(xn_ref[...], w_ref[...], preferred_element_type=jnp.float32)


def proj_matmul(x, g, w_bf16, *, apply_norm, tm, tn):
    m, d = x.shape
    n = w_bf16.shape[1]
    return pl.pallas_call(
        functools.partial(_proj_kernel, apply_norm=apply_norm),
        grid=(m // tm, n // tn),
        in_specs=[
            pl.BlockSpec((tm, d), lambda i, j: (i, 0)),
            pl.BlockSpec((1, d), lambda i, j: (0, 0)),
            pl.BlockSpec((d, tn), lambda i, j: (0, j)),
        ],
        out_specs=pl.BlockSpec((tm, tn), lambda i, j: (i, j)),
        out_shape=jax.ShapeDtypeStruct((m, n), jnp.float32),
        scratch_shapes=[pltpu.VMEM((tm, d), jnp.bfloat16)],
        compiler_params=_cp(("parallel", "arbitrary")),
        name="proj_matmul",
    )(x, g.reshape(1, d), w_bf16)


def _rwkv_prep_kernel(pr_ref, pk_ref, pv_ref, pl_ref, prev_ref, mu_ref, w0_ref, a0_ref, kk_ref, ka_ref,
                      w2_ref, a2_ref, g2_ref,
                      r_out, w_out, k_out, v_out, kk_out, a_out, g_out, carry_ref, *, row_prev):
    tm = pr_ref.shape[1]
    pr, pk, pv, plo = pr_ref[0], pk_ref[0], pv_ref[0], pl_ref[0]
    if row_prev:
        def shift_mix(p, lo, hi):
            return p + (prev_ref[0, :, lo:hi] - p) * mu_ref[:, lo:hi]
    else:
        @pl.when(pl.program_id(1) == 0)
        def _():
            carry_ref[...] = prev_ref[0]

        row0 = lax.broadcasted_iota(jnp.int32, (tm, 1), 0) == 0

        def shift_mix(p, lo, hi):
            prev = pltpu.roll(p, 1, axis=0)
            prev = jnp.where(row0, carry_ref[:, lo:hi], prev)
            return p + (prev - p) * mu_ref[:, lo:hi]

    xr = shift_mix(pr, 0, 1024)
    xk = shift_mix(pk, 1024, 2048)
    xv = shift_mix(pv, 2048, 3072)
    xl = shift_mix(plo, 3072, 3328)
    if not row_prev:
        last = tm - 1
        carry_ref[:, 0:1024] = pr[last:last + 1]
        carry_ref[:, 1024:2048] = pk[last:last + 1]
        carry_ref[:, 2048:3072] = pv[last:last + 1]
        carry_ref[:, 3072:3328] = plo[last:last + 1]

    wl = _dot3(_split2(jnp.tanh(xl)), _split2(w2_ref[...]))
    al = _dot3(_split2(xl), _split2(a2_ref[...]))
    gl = _dot3(_split2(_sigmoid(xl)), _split2(g2_ref[...]))
    z = -(w0_ref[...] + wl)
    softplus = jnp.maximum(z, 0.0) + jnp.log(1.0 + jnp.exp(-jnp.abs(z)))
    w_log = -softplus - 0.5
    decay = jnp.exp(-jnp.exp(w_log))
    a = _sigmoid(a0_ref[...] + al)
    r_out[0] = xr
    w_out[0] = decay
    k_out[0] = xk * (1.0 + (a - 1.0) * ka_ref[...])
    v_out[0] = xv
    kk_out[0] = xk * kk_ref[...]
    a_out[0] = a
    g_out[0] = gl


def rwkv_prep(proj, prev, mu_p, w0, a0, k_k, k_a, w2p, a2p, g2p, *, b, t, tm, row_prev=False):
    proj3 = proj.reshape(b, t, D_IN_PAD)
    nt = t // tm
    prev_spec = (pl.BlockSpec((1, tm, R_COLS), lambda bi, ti: (bi, ti, 0)) if row_prev
                 else pl.BlockSpec((1, 1, R_COLS), lambda bi, ti: (bi, 0, 0)))
    row = lambda blk, w: pl.BlockSpec((1, tm, w), lambda bi, ti: (bi, ti, blk))
    vec = lambda w: pl.BlockSpec((1, w), lambda bi, ti: (0, 0))
    mat = pl.BlockSpec((LORA_ALL, D_RWKV), lambda bi, ti: (0, 0))
    out = pl.BlockSpec((1, tm, D_RWKV), lambda bi, ti: (bi, ti, 0))
    shp = jax.ShapeDtypeStruct((b, t, D_RWKV), jnp.float32)
    return pl.pallas_call(
        functools.partial(_rwkv_prep_kernel, row_prev=row_prev),
        grid=(b, nt),
        in_specs=[row(C_R // 1024, 1024), row(C_K // 1024, 1024), row(C_V // 1024, 1024),
                  row(C_LORA // 256, 256),
                  prev_spec,
                  vec(R_COLS), vec(D_RWKV), vec(D_RWKV), vec(D_RWKV), vec(D_RWKV),
                  mat, mat, mat],
        out_specs=[out] * 7,
        out_shape=[shp] * 7,
        scratch_shapes=[pltpu.VMEM((1, R_COLS), jnp.float32)],
        compiler_params=_cp(("parallel", "arbitrary")),
        name="rwkv_prep",
    )(proj3, proj3, proj3, proj3, prev, mu_p.reshape(1, -1), w0.reshape(1, -1), a0.reshape(1, -1),
      k_k.reshape(1, -1), k_a.reshape(1, -1), w2p, a2p, g2p)


def _rwkv_scan_kernel(r_ref, w_ref, k_ref, kk_ref, a_ref, v_ref, s0_ref, rk_ref, lnw_ref, lnb_ref,
                      o_ref, sfin_ref, s_ref, kkn_ref, b_ref, y_ref, *, halved):
    tc = r_ref.shape[1]
    ni = v_ref.shape[2]

    @pl.when(pl.program_id(1) == 0)
    def _():
        s_ref[...] = s0_ref[0]

    kkraw = kk_ref[0]
    nrm = jnp.sqrt(jnp.sum(kkraw * kkraw, axis=1, keepdims=True))
    kkn = kkraw / jnp.maximum(nrm, 1e-12)
    kkn_ref[...] = kkn
    b_ref[...] = kkn * a_ref[0]

    def step(t, carry):
        w_t = w_ref[0, t]
        k_t = k_ref[0, t]
        r_t = r_ref[0, t]
        kk_t = kkn_ref[t]
        b_t = b_ref[t]
        for i in range(ni):
            s = s_ref[i]
            sa = -jnp.sum(s * kk_t, axis=0, keepdims=True)
            v_i = v_ref[0, t, pl.ds(i, 1), :]
            s = s * w_t + sa * b_t + v_i * k_t
            y_ref[t, pl.ds(i, 1), :] = jnp.sum(s * r_t, axis=0, keepdims=True)
            s_ref[i] = s
        return carry

    lax.fori_loop(0, tc, step, 0)

    y = y_ref[...]
    v = v_ref[0]

    def head_sum(z):
        s = jnp.sum(z, axis=1, keepdims=True)
        if halved:
            s = s + pltpu.roll(s, 64, axis=2)
        return s

    mean = head_sum(y) * (1.0 / HEAD_DIM_R)
    d = y - mean
    var = head_sum(d * d) * (1.0 / HEAD_DIM_R)
    yn = d * lax.rsqrt(var + LNX_EPS) * lnw_ref[0] + lnb_ref[0]
    bonus = jnp.sum(r_ref[0] * k_ref[0] * rk_ref[0], axis=1, keepdims=True)
    o_ref[0] = yn + bonus * v

    @pl.when(pl.program_id(1) == pl.num_programs(1) - 1)
    def _():
        sfin_ref[0] = s_ref[...]


def rwkv_scan(r, w, k, kk, a, v, s0, rk, lnw, lnb, *, tc, halved):
    g, t = r.shape[0], r.shape[1]
    ni = v.shape[2]
    col = pl.BlockSpec((1, tc, 64, 128), lambda gi, ti: (gi, ti, 0, 0))
    vspec = pl.BlockSpec((1, tc, ni, 128), lambda gi, ti: (gi, ti, 0, 0))
    sspec = pl.BlockSpec((1, ni, 64, 128), lambda gi, ti: (gi, 0, 0, 0))
    return pl.pallas_call(
        functools.partial(_rwkv_scan_kernel, halved=halved),
        grid=(g, t // tc),
        in_specs=[col, col, col, col, col, vspec, sspec,
                  pl.BlockSpec((1, 64, 128), lambda gi, ti: (gi, 0, 0)),
                  pl.BlockSpec((1, ni, 128), lambda gi, ti: (gi, 0, 0)),
                  pl.BlockSpec((1, ni, 128), lambda gi, ti: (gi, 0, 0))],
        out_specs=[vspec, sspec],
        out_shape=[jax.ShapeDtypeStruct((g, t, ni, 128), jnp.float32),
                   jax.ShapeDtypeStruct((g, ni, 64, 128), jnp.float32)],
        scratch_shapes=[pltpu.VMEM((ni, 64, 128), jnp.float32),
                        pltpu.VMEM((tc, 64, 128), jnp.float32),
                        pltpu.VMEM((tc, 64, 128), jnp.float32),
                        pltpu.VMEM((tc, ni, 128), jnp.float32)],
        compiler_params=_cp(("parallel", "arbitrary")),
        name="rwkv_scan",
    )(r, w, k, kk, a, v, s0, rk, lnw, lnb)


def _to_scan_layout(x, b, t, halved):
    z = x.reshape(b, t, H_R, 64).transpose(1, 3, 0, 2).reshape(t, 64, b * H_R)
    if halved:
        return jnp.concatenate([z, z], axis=-1)[None]
    g = (b * H_R) // 128
    return z.reshape(t, 64, g, 128).transpose(2, 0, 1, 3)


def _v_to_scan_layout(x, b, t, halved):
    if halved:
        return x.reshape(b, t, H_R, 2, 32).transpose(1, 4, 3, 0, 2).reshape(1, t, 32, 128)
    g = (b * H_R) // 128
    z = x.reshape(b, t, H_R, 64).transpose(1, 3, 0, 2).reshape(t, 64, g, 128)
    return z.transpose(2, 0, 1, 3)


def _vecparam_scan_layout(p, b, halved):
    z = jnp.broadcast_to(p.reshape(1, H_R, 64), (b, H_R, 64))
    if halved:
        return z.reshape(b, H_R, 2, 32).transpose(3, 2, 0, 1).reshape(1, 32, 128)
    g = (b * H_R) // 128
    return z.transpose(2, 0, 1).reshape(64, g, 128).transpose(1, 0, 2)


def _keyparam_scan_layout(p, b, halved):
    z = jnp.broadcast_to(p.reshape(1, H_R, 64), (b, H_R, 64)).transpose(2, 0, 1).reshape(64, b * H_R)
    if halved:
        return jnp.concatenate([z, z], axis=-1)[None]
    g = (b * H_R) // 128
    return z.reshape(64, g, 128).transpose(1, 0, 2)


def _from_scan_layout(o, b, t, halved):
    if halved:
        return o.reshape(t, 32, 2, b, H_R).transpose(3, 0, 4, 2, 1).reshape(b, t, D_RWKV)
    g = o.shape[0]
    return o.transpose(1, 2, 0, 3).reshape(t, 64, b, H_R).transpose(2, 0, 3, 1).reshape(b, t, D_RWKV)


def _state_to_scan_layout(s, b, halved):
    if halved:
        return s.reshape(b, H_R, 2, 32, 64).transpose(3, 4, 2, 0, 1).reshape(1, 32, 64, 128)
    g = (b * H_R) // 128
    return s.reshape(g, 128, 64, 64).transpose(0, 2, 3, 1)


def _state_from_scan_layout(s, b, halved):
    if halved:
        return s.reshape(32, 64, 2, b, H_R).transpose(3, 4, 2, 0, 1).reshape(b, H_R, 64, 64)
    return s.transpose(0, 3, 1, 2).reshape(b, H_R, 64, 64)


def rwkv_mix(proj, prev, wts, s0, *, b, t, tm, tc):
    halved = (b * H_R == 64)
    if t == 1:
        outs = rwkv_prep(proj, prev.reshape(1, b, R_COLS), wts["mu_p"], wts["w0"], wts["a0"], wts["k_k"],
                         wts["k_a"], wts["w2p"], wts["a2p"], wts["g2p"], b=1, t=b, tm=b, row_prev=True)
        r, w, k, v, kk, a, g = [z.reshape(b, 1, D_RWKV) for z in outs]
    else:
        r, w, k, v, kk, a, g = rwkv_prep(proj, prev, wts["mu_p"], wts["w0"], wts["a0"], wts["k_k"], wts["k_a"],
                                         wts["w2p"], wts["a2p"], wts["g2p"], b=b, t=t, tm=tm)
    tr = lambda z: _to_scan_layout(z, b, t, halved)
    o, sfin = rwkv_scan(tr(r), tr(w), tr(k), tr(kk), tr(a), _v_to_scan_layout(v, b, t, halved),
                        _state_to_scan_layout(s0, b, halved),
                        _keyparam_scan_layout(wts["r_k"], b, halved),
                        _vecparam_scan_layout(wts["lnx_w"], b, halved),
                        _vecparam_scan_layout(wts["lnx_b"], b, halved),
                        tc=tc, halved=halved)
    return _from_scan_layout(o, b, t, halved), g, _state_from_scan_layout(sfin, b, halved)


def _cmp_part_kernel(*refs, n_in, precise, prefetch):
    refs = refs[prefetch:]
    x_refs, w_ref, o_ref = refs[:n_in], refs[n_in], refs[n_in + 1]
    for e in range(2):
        for hp in range(2):
            acc = None
            for j in range(D_CMP):
                lo = j * 512 + e * 256 + hp * 128
                xs = [xr[0, :, lo:lo + 128] for xr in x_refs]
                x = xs[0] if n_in == 1 else jnp.concatenate(xs, axis=0)
                d = _hdot(x, w_ref[e, j]) if precise else _bdot(x, w_ref[e, j])
                acc = d if acc is None else acc + d
            c0 = e * 256 + hp * 128
            o_ref[0, :, c0:c0 + 128] = acc[:, :128]
            o_ref[0, :, 512 + c0:512 + c0 + 128] = acc[:, 128:]


def cmp_part_prompt(rows, w1x):
    b, nch, _ = rows.shape
    return pl.pallas_call(
        functools.partial(_cmp_part_kernel, n_in=1, precise=True, prefetch=0),
        grid=(b,),
        in_specs=[pl.BlockSpec((1, nch, 8192), lambda bi: (bi, 0, 0)),
                  pl.BlockSpec((2, D_CMP, 128, 256), lambda bi: (0, 0, 0, 0))],
        out_specs=pl.BlockSpec((1, nch, 1024), lambda bi: (bi, 0, 0)),
        out_shape=jax.ShapeDtypeStruct((b, nch, 1024), jnp.float32),
        compiler_params=_cp(("parallel",)),
        name="cmp_part_prompt",
    )(rows, w1x)


def _expand_w1(phi_w1):
    w1 = phi_w1.reshape(2, 2, D_CMP, HEAD_DIM, PHI_HIDDEN)
    eye = jnp.eye(2, dtype=phi_w1.dtype)
    w = jnp.einsum("ab,esjdf->ejadsbf", eye, w1)
    return w.reshape(2, D_CMP, 128, 256)


def _split2(x):
    hi = x.astype(jnp.bfloat16)
    return hi, (x - hi.astype(jnp.float32)).astype(jnp.bfloat16)


def _dot3(a, b):
    (ah, al), (bh, bl) = a, b
    d = lambda x, y: jnp.dot(x, y, preferred_element_type=jnp.float32)
    return d(ah, bh) + (d(ah, bl) + d(al, bh))


def _pe_term_kernel(pe_ref, w1t_ref, o_ref):
    for e in range(2):
        o_ref[e] = _hdot(pe_ref[e], w1t_ref[e])


def pe_term(pe8, w1t):
    return pl.pallas_call(
        _pe_term_kernel,
        out_shape=jax.ShapeDtypeStruct((2, 8, N_KV * PHI_HIDDEN), jnp.float32),
        name="pe_term",
    )(pe8, w1t)


def _cmp_hidden(part, pe_ref, e):
    n = part.shape[0]
    lo = e * 256
    h = part[:, lo:lo + 256] + pltpu.roll(part[:, 512 + lo:512 + lo + 256], n - 1, axis=0)
    return _silu(h + pe_ref[e, 0:1])


def _cmp_sel_kernel(part_ref, pe_ref, w2k_ref, w2vt_ref, q_ref, ovt_ref, gt_ref, o_ref, sel_ref,
                    kch_ref, kcl_ref, vct_ref, *, tq, n_blk):
    ti = pl.program_id(1)

    @pl.when(ti == 0)
    def _():
        part = part_ref[0]
        hk = _split2(_cmp_hidden(part, pe_ref, 0))
        hv = _cmp_hidden(part, pe_ref, 1)
        for kvh in range(N_KV):
            kch_ref[kvh], kcl_ref[kvh] = _split2(_dot3(hk, _split2(w2k_ref[kvh])))
            vct_ref[kvh] = _dot_nt(w2vt_ref[kvh], hv, HI).astype(jnp.bfloat16)

    nc = kch_ref.shape[1]
    nbp = -(-n_blk // 8) * 8
    k_sel = min(N_SELECT, n_blk)
    qpos = ti * tq + lax.broadcasted_iota(jnp.int32, (1, tq), 1)
    cidx = lax.broadcasted_iota(jnp.int32, (nc, 1), 0)
    cmask = (cidx * D_CMP + (L_CMP - 1)) <= qpos
    sidx = cidx[:nbp]
    qblk = qpos // L_SLC
    forced = (sidx == 0) | (sidx == qblk) | (sidx == qblk - 1)
    causal = sidx <= qblk
    pad = jnp.zeros((128 - nbp, tq), jnp.float32)
    need_rank = (ti + 1) * tq > k_sel * L_SLC

    cmask4 = jnp.concatenate([cmask.astype(jnp.float32)] * GQA, axis=1) > 0.5
    gate_t = _sigmoid(gt_ref[0][:, 0:128]).T
    for kvh in range(N_KV):
        qt = (q_ref[0, :, kvh * 256:(kvh + 1) * 256] * SCALE).T
        qt4 = _split2(jnp.concatenate([qt[g * 64:(g + 1) * 64] for g in range(GQA)], axis=1))
        st = _dot3((kch_ref[kvh], kcl_ref[kvh]), qt4)
        st = jnp.where(cmask4, st, NEG)
        m = jnp.max(st, axis=0, keepdims=True)
        ex = jnp.where(cmask4, jnp.exp(st - m), 0.0)
        pt = ex / jnp.maximum(jnp.sum(ex, axis=0, keepdims=True), 1e-30)
        ot = jnp.dot(vct_ref[kvh], pt.astype(jnp.bfloat16), preferred_element_type=jnp.float32)
        gate4 = jnp.concatenate([gate_t[kvh * GQA + g:kvh * GQA + g + 1] for g in range(GQA)], axis=1)
        o_ref[0, :, kvh * 256:(kvh + 1) * 256] = jnp.concatenate(
            [(ot * gate4)[:, g * tq:(g + 1) * tq] for g in range(GQA)], axis=0).T
        impt = pt[:, 0:tq]
        for g in range(1, GQA):
            impt = impt + pt[:, g * tq:(g + 1) * tq]

        @pl.when(need_rank)
        def _():
            ih, il = _split2(impt)
            ov = ovt_ref[...]
            imps = (jnp.dot(ov, ih, preferred_element_type=jnp.float32)
                    + jnp.dot(ov, il, preferred_element_type=jnp.float32))[:nbp]
            score = jnp.where(forced, -NEG, jnp.where(causal, imps, NEG))
            rank = jnp.zeros((nbp, tq), jnp.float32)
            for s2 in range(n_blk):
                row = score[s2:s2 + 1]
                beats = (row > score) | ((row == score) & (s2 < sidx))
                rank = rank + beats.astype(jnp.float32)
            seln = ((rank < k_sel) & causal).astype(jnp.float32)
            sel_ref[0, kvh] = jnp.concatenate([seln, pad], axis=0)

        @pl.when(jnp.logical_not(need_rank))
        def _():
            sel_ref[0, kvh] = jnp.concatenate([causal.astype(jnp.float32), pad], axis=0)


def cmp_select_prompt(part, pe, w2k, w2vt, proj3, ovt_bf16, *, tq):
    b, nc, _ = part.shape
    t = proj3.shape[1]
    c3 = lambda bi, ti: (0, 0, 0)
    return pl.pallas_call(
        functools.partial(_cmp_sel_kernel, tq=tq, n_blk=t // L_SLC),
        grid=(b, t // tq),
        in_specs=[pl.BlockSpec((1, nc, 1024), lambda bi, ti: (bi, 0, 0)),
                  pl.BlockSpec((2, 8, 256), c3),
                  pl.BlockSpec((N_KV, 256, 64), c3),
                  pl.BlockSpec((N_KV, 64, 256), c3),
                  pl.BlockSpec((1, tq, 1024), lambda bi, ti: (bi, ti, C_Q // 1024)),
                  pl.BlockSpec((128, nc), lambda bi, ti: (0, 0)),
                  pl.BlockSpec((1, tq, 256), lambda bi, ti: (bi, ti, C_GATE // 256))],
        out_specs=[pl.BlockSpec((1, tq, 1024), lambda bi, ti: (bi, ti, 0)),
                   pl.BlockSpec((1, N_KV, 128, tq), lambda bi, ti: (bi, 0, 0, ti))],
        out_shape=[jax.ShapeDtypeStruct((b, t, 1024), jnp.float32),
                   jax.ShapeDtypeStruct((b, N_KV, 128, t), jnp.float32)],
        scratch_shapes=[pltpu.VMEM((N_KV, nc, 64), jnp.bfloat16),
                        pltpu.VMEM((N_KV, nc, 64), jnp.bfloat16),
                        pltpu.VMEM((N_KV, 64, nc), jnp.bfloat16)],
        compiler_params=_cp(("parallel", "arbitrary")),
        name="cmp_select_prompt",
    )(part, pe, w2k, w2vt, proj3, ovt_bf16, proj3)


def _overlap_t(n_blocks_pad, nc, nc_pad):
    c_start = np.arange(nc_pad) * D_CMP
    s_start = np.arange(n_blocks_pad) * L_SLC
    ov = (c_start[None, :] < s_start[:, None] + L_SLC) & (c_start[None, :] + L_CMP > s_start[:, None])
    ov &= (np.arange(nc_pad) < nc)[None, :]
    return jnp.asarray(ov.astype(np.float32))


def _phi_weights(phi_pe, phi_w1, phi_w2):
    w1t = jnp.tile(phi_w1, (1, 1, N_KV))
    pe8 = jnp.broadcast_to(phi_pe.reshape(2, 1, L_CMP * HEAD_DIM), (2, 8, L_CMP * HEAD_DIM))
    z = jnp.zeros((N_KV, N_KV, PHI_HIDDEN, HEAD_DIM), jnp.float32)
    idx = jnp.arange(N_KV)
    w2k = z.at[idx, idx].set(phi_w2[0]).reshape(N_KV, 256, 64)
    w2vt = jnp.transpose(z.at[idx, idx].set(phi_w2[1]), (0, 3, 1, 2)).reshape(N_KV, 64, 256)
    eye = jnp.eye(N_KV, dtype=jnp.float32)
    w2bd = jnp.einsum("ab,efd->eafbd", eye, phi_w2).reshape(2, 256, 256)
    return w1t, pe8, w2k, w2vt, w2bd


def _gate_row(gt_ref, branch, kvh, tq):
    gt = _sigmoid(gt_ref[0][:, 0:128]).T
    col = lax.broadcasted_iota(jnp.int32, (128, 1), 0)
    rows = [jnp.sum(jnp.where(col == branch * N_HEADS + kvh * GQA + g, gt, 0.0), axis=0, keepdims=True)
            for g in range(GQA)]
    return jnp.concatenate(rows, axis=1)


def _heads_to_rows(o, tq):
    return jnp.concatenate([o[:, g * tq:(g + 1) * tq] for g in range(GQA)], axis=0).T


def _attend_t(k, vt, qt4, bias, tq):
    s = jnp.dot(k, qt4, preferred_element_type=jnp.float32) + jnp.concatenate([bias] * GQA, axis=1)
    s = s.astype(jnp.bfloat16)
    p = jnp.exp(s - jnp.max(s, axis=0, keepdims=True))
    vt1 = jnp.concatenate([vt, jnp.ones((16, vt.shape[1]), jnp.bfloat16)], axis=0)
    ol = jnp.dot(vt1, p, preferred_element_type=jnp.float32)
    return ol[:HEAD_DIM] / jnp.maximum(ol[HEAD_DIM:HEAD_DIM + 1], 1e-30)


SLC_KEY_STEP = 256
WIN_BLOCKS = WINDOW // 128 + 1


def _slc_win_kernel(q_ref, ks_ref, vs_ref, kw_ref, vw_ref, sel_ref, ex_ref, gt_ref, oc_ref, o_ref, acc_ref,
                    *, tq):
    kvh = pl.program_id(1)
    ti = pl.program_id(2)
    t = ks_ref.shape[2]
    qt = (q_ref[0] * SCALE).T.astype(jnp.bfloat16)
    qt4 = jnp.concatenate([qt[g * HEAD_DIM:(g + 1) * HEAD_DIM] for g in range(GQA)], axis=1)
    qpos = ti * tq + lax.broadcasted_iota(jnp.int32, (1, tq), 1)
    selt = sel_ref[0, 0].astype(jnp.bfloat16)

    n_var = -(-t // SLC_KEY_STEP)
    for c in range(n_var):
        nk = min((c + 1) * SLC_KEY_STEP, t)

        @pl.when((ti * tq) // SLC_KEY_STEP == c)
        def _():
            on = jnp.dot(ex_ref[0:nk], selt, preferred_element_type=jnp.float32)
            kpos = lax.broadcasted_iota(jnp.int32, (nk, 1), 0)
            bias = (on - 1.0) * (-NEG) + jnp.where(kpos <= qpos, 0.0, NEG)
            acc_ref[...] = _attend_t(ks_ref[0, 0, 0:nk], vs_ref[0, 0, :, 0:nk], qt4, bias, tq)

    nkb = kw_ref.shape[2]
    w0 = jnp.clip(ti - (WIN_BLOCKS - 1), 0, nkb - WIN_BLOCKS)
    kw = kw_ref[0, 0, pl.ds(w0, WIN_BLOCKS)].reshape(WIN_BLOCKS * 128, HEAD_DIM)
    vwt = jnp.concatenate([vw_ref[0, 0, w0 + i] for i in range(WIN_BLOCKS)], axis=1)
    diff = qpos - (w0 * 128 + lax.broadcasted_iota(jnp.int32, (WIN_BLOCKS * 128, 1), 0))
    o_win = _attend_t(kw, vwt, qt4, jnp.where((diff >= 0) & (diff < WINDOW), 0.0, NEG), tq)
    o = acc_ref[...] * _gate_row(gt_ref, 1, kvh, tq) + o_win * _gate_row(gt_ref, 2, kvh, tq)
    o_ref[0] = (_heads_to_rows(o, tq) + oc_ref[0]).astype(jnp.bfloat16)


def slc_win_prompt(proj3, ks, vst, kw, vwt, selt, o_cmp, *, tq):
    b, t, _ = proj3.shape
    nkb = t // 128
    expand = jnp.asarray((np.arange(t)[:, None] // L_SLC == np.arange(128)[None, :]).astype(np.float32),
                         jnp.bfloat16)
    i4 = lambda bi, hi, ti: (bi, hi, 0, 0)
    i5 = lambda bi, hi, ti: (bi, hi, 0, 0, 0)
    head_cols = pl.BlockSpec((1, tq, 256), lambda bi, hi, ti: (bi, ti, hi))
    return pl.pallas_call(
        functools.partial(_slc_win_kernel, tq=tq),
        grid=(b, N_KV, t // tq),
        in_specs=[head_cols,
                  pl.BlockSpec((1, 1, t, HEAD_DIM), i4),
                  pl.BlockSpec((1, 1, HEAD_DIM, t), i4),
                  pl.BlockSpec((1, 1, nkb, 128, HEAD_DIM), i5),
                  pl.BlockSpec((1, 1, nkb, HEAD_DIM, 128), i5),
                  pl.BlockSpec((1, 1, 128, tq), lambda bi, hi, ti: (bi, hi, 0, ti)),
                  pl.BlockSpec((t, 128), lambda bi, hi, ti: (0, 0)),
                  pl.BlockSpec((1, tq, 256), lambda bi, hi, ti: (bi, ti, C_GATE // 256)),
                  head_cols],
        out_specs=head_cols,
        out_shape=jax.ShapeDtypeStruct((b, t, 1024), jnp.bfloat16),
        scratch_shapes=[pltpu.VMEM((HEAD_DIM, GQA * tq), jnp.float32)],
        compiler_params=_cp(("parallel", "parallel", "arbitrary")),
        name="slc_win_prompt",
    )(proj3, ks, vst, kw, vwt, selt, expand, proj3, o_cmp)


def _kv_heads(kv_rows, b, t):
    kv = kv_rows.reshape(b, t, 2, N_KV, HEAD_DIM).astype(jnp.bfloat16)
    return kv[:, :, 0].transpose(0, 2, 1, 3), kv[:, :, 1].transpose(0, 2, 3, 1)


def _kv_blocks(kv_rows, b, t):
    k, vt = _kv_heads(kv_rows, b, t)
    return (k.reshape(b, N_KV, t // 128, 128, HEAD_DIM),
            vt.reshape(b, N_KV, HEAD_DIM, t // 128, 128).transpose(0, 1, 3, 2, 4))


def _out_proj_kernel(ro_ref, rg_ref, nsa_ref, x_ref, wo_ref, y_ref, a_ref):
    @pl.when(pl.program_id(1) == 0)
    def _():
        a_ref[:, 0:D_RWKV] = (ro_ref[...] * rg_ref[...]).astype(jnp.bfloat16)
        a_ref[:, D_RWKV:D_MODEL] = nsa_ref[...]

    y_ref[...] = x_ref[...] + jnp.dot(a_ref[...], wo_ref[...], preferred_element_type=jnp.float32)


def out_proj(ro, rg, nsa_bf16, x, wo_bf16, *, tm, tn):
    m = x.shape[0]
    row = lambda w: pl.BlockSpec((tm, w), lambda i, j: (i, 0))
    return pl.pallas_call(
        _out_proj_kernel,
        grid=(m // tm, D_MODEL // tn),
        in_specs=[row(D_RWKV), row(D_RWKV), row(D_NSA),
                  pl.BlockSpec((tm, tn), lambda i, j: (i, j)),
                  pl.BlockSpec((D_MODEL, tn), lambda i, j: (0, j))],
        out_specs=pl.BlockSpec((tm, tn), lambda i, j: (i, j)),
        out_shape=jax.ShapeDtypeStruct((m, D_MODEL), jnp.float32),
        scratch_shapes=[pltpu.VMEM((tm, D_MODEL), jnp.bfloat16)],
        compiler_params=_cp(("parallel", "arbitrary")),
        name="out_proj",
    )(ro, rg, nsa_bf16, x, wo_bf16)


def _nsa_combine_kernel(oc_ref, os_ref, ow_ref, gt_ref, o_ref):
    g = _sigmoid(gt_ref[...])
    o_ref[...] = (g[:, 0] * oc_ref[...] + g[:, 1] * os_ref[...] + g[:, 2] * ow_ref[...]).astype(jnp.bfloat16)


def nsa_combine_sample(o_cmp, o_slc, o_win, gate_logits):
    return pl.pallas_call(
        _nsa_combine_kernel,
        out_shape=jax.ShapeDtypeStruct(o_cmp.shape, jnp.bfloat16),
        name="nsa_combine_sample",
    )(o_cmp, o_slc, o_win, gate_logits)


def _ffn_kernel(x_ref, gf_ref, wg_ref, wu_ref, wd_ref, gl_ref, y_ref, h_ref, acc_ref):
    f = pl.program_id(1)

    @pl.when(f == 0)
    def _():
        x = x_ref[...]
        ms = jnp.mean(x * x, axis=-1, keepdims=True)
        h_ref[...] = (x * lax.rsqrt(ms + RMS_EPS) * gf_ref[...]).astype(jnp.bfloat16)
        acc_ref[...] = jnp.zeros_like(acc_ref)

    h = h_ref[...]
    gate = jnp.dot(h, wg_ref[...], preferred_element_type=jnp.float32)
    up = jnp.dot(h, wu_ref[...], preferred_element_type=jnp.float32)
    act = (_silu(gate) * up).astype(jnp.bfloat16)
    acc_ref[...] += jnp.dot(act, wd_ref[...], preferred_element_type=jnp.float32)

    @pl.when(f == pl.num_programs(1) - 1)
    def _():
        z = x_ref[...] + acc_ref[...]
        ms = jnp.mean(z * z, axis=-1, keepdims=True)
        y_ref[...] = z * lax.rsqrt(ms + RMS_EPS) * gl_ref[...]


def ffn_final(x, g_ffn, wg, wu, wd, g_final, *, tm, tf):
    m = x.shape[0]
    dff = wg.shape[1]
    return pl.pallas_call(
        _ffn_kernel,
        grid=(m // tm, dff // tf),
        in_specs=[pl.BlockSpec((tm, D_MODEL), lambda i, f: (i, 0)),
                  pl.BlockSpec((1, D_MODEL), lambda i, f: (0, 0)),
                  pl.BlockSpec((D_MODEL, tf), lambda i, f: (0, f)),
                  pl.BlockSpec((D_MODEL, tf), lambda i, f: (0, f)),
                  pl.BlockSpec((tf, D_MODEL), lambda i, f: (f, 0)),
                  pl.BlockSpec((1, D_MODEL), lambda i, f: (0, 0))],
        out_specs=pl.BlockSpec((tm, D_MODEL), lambda i, f: (i, 0)),
        out_shape=jax.ShapeDtypeStruct((m, D_MODEL), jnp.float32),
        scratch_shapes=[pltpu.VMEM((tm, D_MODEL), jnp.bfloat16), pltpu.VMEM((tm, D_MODEL), jnp.float32)],
        compiler_params=_cp(("parallel", "arbitrary")),
        name="ffn_final",
    )(x, g_ffn.reshape(1, -1), wg, wu, wd, g_final.reshape(1, -1))


def _cmp_part_paged_kernel(pt_ref, *refs, n_in):
    x_refs = refs[:n_in]
    perm_ref, w_ref, o_ref, xs_ref = refs[n_in:]
    cpp = x_refs[0].shape[-1] // D_CMP
    n_pair = 2 * (N_KV // 2)
    for p in range(n_in):
        x = x_refs[p][0].reshape(2 * N_KV * HEAD_DIM, x_refs[p].shape[-1])
        xp_all = jnp.dot(x.astype(jnp.bfloat16), perm_ref[...], preferred_element_type=jnp.float32)
        for ep in range(n_pair):
            xp = xp_all[ep * 128:(ep + 1) * 128].T
            for j in range(D_CMP):
                xs_ref[ep, j, p * cpp:(p + 1) * cpp, :] = xp[j * cpp:(j + 1) * cpp]
    for e in range(2):
        for hp in range(2):
            acc = None
            for j in range(0, D_CMP, 2):
                x2 = jnp.concatenate([xs_ref[e * 2 + hp, j], xs_ref[e * 2 + hp, j + 1]], axis=1)
                w2 = jnp.concatenate([w_ref[e, j], w_ref[e, j + 1]], axis=0)
                d = _bdot(x2, w2)
                acc = d if acc is None else acc + d
            c0 = e * 256 + hp * 128
            o_ref[0, :, c0:c0 + 128] = acc[:, :128]
            o_ref[0, :, 512 + c0:512 + c0 + 128] = acc[:, 128:]


def cmp_part_sample(pool_t, page_table, w1x_bf16, *, pp):
    b, n_pages = page_table.shape
    page = pool_t.shape[-1]
    cpp = page // D_CMP

    def xspec(k):
        return pl.BlockSpec((1, 2, N_KV, HEAD_DIM, page), lambda bi, gi, pt: (pt[bi, gi * pp + k], 0, 0, 0, 0))

    r = np.arange(page)
    perm = np.zeros((page, page), np.float32)
    perm[r, (r % D_CMP) * cpp + r // D_CMP] = 1.0
    grid_spec = pltpu.PrefetchScalarGridSpec(
        num_scalar_prefetch=1,
        grid=(b, n_pages // pp),
        in_specs=[xspec(k) for k in range(pp)]
        + [pl.BlockSpec((page, page), lambda bi, gi, pt: (0, 0)),
           pl.BlockSpec((2, D_CMP, 128, 256), lambda bi, gi, pt: (0, 0, 0, 0))],
        out_specs=pl.BlockSpec((1, cpp * pp, 1024), lambda bi, gi, pt: (bi, gi, 0)),
        scratch_shapes=[pltpu.VMEM((N_KV, D_CMP, pp * cpp, 128), jnp.float32)],
    )
    return pl.pallas_call(
        functools.partial(_cmp_part_paged_kernel, n_in=pp),
        grid_spec=grid_spec,
        out_shape=jax.ShapeDtypeStruct((b, n_pages * cpp, 1024), jnp.float32),
        compiler_params=_cp(("parallel", "arbitrary")),
        name="cmp_part_sample",
    )(page_table, *([pool_t] * pp), jnp.asarray(perm, jnp.bfloat16), w1x_bf16)


def _fold_heads(o_ext):
    row_kvh = lax.broadcasted_iota(jnp.int32, (N_HEADS, 1), 0) // GQA
    out = jnp.zeros((N_HEADS, HEAD_DIM), jnp.float32)
    for kvh in range(N_KV):
        out = out + jnp.where(row_kvh == kvh, o_ext[:, kvh * HEAD_DIM:(kvh + 1) * HEAD_DIM], 0.0)
    return out


def _cmp_attn_sample_kernel(part_ref, pe_ref, w2bd_ref, q_ref, g8_ref, ov_ref, o_ref, imps_ref, *, q_pos):
    part = part_ref[0]
    n = part.shape[0]
    kc = _split2(_dot3(_split2(_cmp_hidden(part, pe_ref, 0)), _split2(w2bd_ref[0])))
    vc = _bdot(_cmp_hidden(part, pe_ref, 1), w2bd_ref[1])
    qh, ql = _split2(q_ref[0])
    s = (_dot_nt(qh, kc[0]) + (_dot_nt(qh, kc[1]) + _dot_nt(ql, kc[0]))) * SCALE
    cidx = lax.broadcasted_iota(jnp.int32, (1, n), 1)
    cmask = (cidx * D_CMP + (L_CMP - 1)) <= q_pos
    s = jnp.where(cmask, s, NEG)
    m = jnp.max(s, axis=-1, keepdims=True)
    ex = jnp.where(cmask, jnp.exp(s - m), 0.0)
    p = ex / jnp.maximum(jnp.sum(ex, axis=-1, keepdims=True), 1e-30)
    o_ref[0] = _fold_heads(_bdot(p, vc))
    ih, il = _split2(_hdot(g8_ref[...], p))
    ov = ov_ref[...]
    imps_ref[0] = (jnp.dot(ih, ov, preferred_element_type=jnp.float32)
                   + jnp.dot(il, ov, preferred_element_type=jnp.float32))


def cmp_attn_sample(part, pe, w2bd, q16ext, g8, ov_bf16, *, q_pos):
    b, n, _ = part.shape
    nbp = ov_bf16.shape[1]
    c3 = lambda bi: (0, 0, 0)
    return pl.pallas_call(
        functools.partial(_cmp_attn_sample_kernel, q_pos=q_pos),
        grid=(b,),
        in_specs=[pl.BlockSpec((1, n, 1024), lambda bi: (bi, 0, 0)),
                  pl.BlockSpec((2, 8, 256), c3),
                  pl.BlockSpec((2, 256, 256), c3),
                  pl.BlockSpec((1, N_HEADS, 256), lambda bi: (bi, 0, 0)),
                  pl.BlockSpec((8, N_HEADS), lambda bi: (0, 0)),
                  pl.BlockSpec((n, nbp), lambda bi: (0, 0))],
        out_specs=[pl.BlockSpec((1, N_HEADS, HEAD_DIM), lambda bi: (bi, 0, 0)),
                   pl.BlockSpec((1, 8, nbp), lambda bi: (bi, 0, 0))],
        out_shape=[jax.ShapeDtypeStruct((b, N_HEADS, HEAD_DIM), jnp.float32),
                   jax.ShapeDtypeStruct((b, 8, nbp), jnp.float32)],
        compiler_params=_cp(("parallel",)),
        name="cmp_attn_sample",
    )(part, pe, w2bd, q16ext, g8, ov_bf16)


def _topk_kernel(sc_ref, idx_ref, *, n_blk, q_blk, k_sel):
    imps = sc_ref[...]
    nbp = imps.shape[0]
    sidx = lax.broadcasted_iota(jnp.int32, (nbp, 1), 0)
    sf = sidx.astype(jnp.float32)
    forced = (sidx == 0) | (sidx == q_blk) | (sidx == q_blk - 1)
    causal = (sidx <= q_blk) & (sidx < n_blk)
    score = jnp.where(forced, -NEG, jnp.where(causal, imps, NEG))
    rows = []
    for _ in range(k_sel):
        m = jnp.max(score, axis=0, keepdims=True)
        pick = jnp.min(jnp.where(score == m, sf, float(nbp)), axis=0, keepdims=True)
        rows.append(jnp.where(m > 0.5 * NEG, pick, -1.0))
        score = jnp.where(sf == pick, 2.0 * NEG, score)
    idx_ref[...] = jnp.concatenate(rows, axis=0).astype(jnp.int32)


def topk_blocks(scores, *, n_blk, q_blk, k_sel):
    nbp, lanes = scores.shape
    return pl.pallas_call(
        functools.partial(_topk_kernel, n_blk=n_blk, q_blk=q_blk, k_sel=k_sel),
        out_shape=jax.ShapeDtypeStruct((k_sel, lanes), jnp.int32),
        name="topk_blocks",
    )(scores)


def _slc_sample_kernel(idx_ref, pt_ref, *refs, n_sel, new_blk):
    blk_refs = refs[:N_KV * n_sel]
    q_ref, knew_ref, vnew_ref, o_ref = refs[N_KV * n_sel:]
    b = pl.program_id(0)
    page = blk_refs[0].shape[-1]
    lane = lax.broadcasted_iota(jnp.int32, (1, n_sel * page), 1)
    for kvh in range(N_KV):
        kts, vts = [], []
        live = lane < 0
        has_new = False
        for n in range(n_sel):
            blk = blk_refs[kvh * n_sel + n]
            kts.append(blk[0, 0, 0])
            vts.append(blk[0, 1, 0])
            s_n = idx_ref[(b * N_KV + kvh) * n_sel + n]
            lo = n * page + (s_n % (page // L_SLC)) * L_SLC
            live = live | ((lane >= lo) & (lane < lo + L_SLC) & (s_n >= 0) & (s_n != new_blk))
            has_new = jnp.logical_or(has_new, s_n == new_blk)
        kt = jnp.concatenate(kts, axis=1).astype(jnp.bfloat16)
        vt = jnp.concatenate(vts, axis=1).astype(jnp.bfloat16)
        q = q_ref[0, kvh]
        s = jnp.dot(q.astype(jnp.bfloat16), kt, preferred_element_type=jnp.float32) * SCALE
        s = jnp.where(live, s, NEG)
        s_new = jnp.where(has_new, jnp.sum(q * knew_ref[0, kvh], axis=-1, keepdims=True) * SCALE, NEG)
        m = jnp.maximum(jnp.max(s, axis=-1, keepdims=True), s_new)
        ex = jnp.where(live, jnp.exp(s - m), 0.0)
        ex_new = jnp.where(has_new, jnp.exp(s_new - m), 0.0)
        denom = jnp.maximum(jnp.sum(ex, axis=-1, keepdims=True) + ex_new, 1e-30)
        o_ref[0, kvh] = (_dot_nt(ex.astype(jnp.bfloat16), vt) + ex_new * vnew_ref[0, kvh]) / denom


def slc_sample(idx_flat, page_table, pool_t, q4, knew, vnew, *, n_sel, new_blk):
    b, n_pages = page_table.shape
    page = pool_t.shape[-1]
    bpp = page // L_SLC

    def bspec(kvh, n):
        def imap(bi, idx, pt):
            s = jnp.clip(idx[(bi * N_KV + kvh) * n_sel + n], 0, bpp * n_pages - 1)
            return (pt[bi, s // bpp], 0, kvh, 0, 0)
        return pl.BlockSpec((1, 2, 1, HEAD_DIM, page), imap)

    small = lambda r: pl.BlockSpec((1, N_KV, r, HEAD_DIM), lambda bi, idx, pt: (bi, 0, 0, 0))
    grid_spec = pltpu.PrefetchScalarGridSpec(
        num_scalar_prefetch=2,
        grid=(b,),
        in_specs=[bspec(kvh, n) for kvh in range(N_KV) for n in range(n_sel)] + [small(8), small(1), small(1)],
        out_specs=small(8),
    )
    return pl.pallas_call(
        functools.partial(_slc_sample_kernel, n_sel=n_sel, new_blk=new_blk),
        grid_spec=grid_spec,
        out_shape=jax.ShapeDtypeStruct((b, N_KV, 8, HEAD_DIM), jnp.float32),
        compiler_params=_cp(("arbitrary",)),
        name="slc_sample",
    )(idx_flat, page_table, *([pool_t] * (N_KV * n_sel)), q4, knew, vnew)


def _win_sample_kernel(w_ref, q_ref, o_ref):
    w = w_ref[0]
    s = _dot_nt(q_ref[0].astype(jnp.bfloat16), w[:, :KV_W].astype(jnp.bfloat16)) * SCALE
    m = jnp.max(s, axis=-1, keepdims=True)
    ex = jnp.exp(s - m)
    p = ex / jnp.maximum(jnp.sum(ex, axis=-1, keepdims=True), 1e-30)
    o_ref[0] = _fold_heads(_bdot(p, w[:, KV_W:]))


def win_sample(win_rows, q16ext):
    b, nk, _ = win_rows.shape
    return pl.pallas_call(
        _win_sample_kernel,
        grid=(b,),
        in_specs=[pl.BlockSpec((1, nk, 2 * KV_W), lambda bi: (bi, 0, 0)),
                  pl.BlockSpec((1, N_HEADS, KV_W), lambda bi: (bi, 0, 0))],
        out_specs=pl.BlockSpec((1, N_HEADS, HEAD_DIM), lambda bi: (bi, 0, 0)),
        out_shape=jax.ShapeDtypeStruct((b, N_HEADS, HEAD_DIM), jnp.float32),
        compiler_params=_cp(("parallel",)),
        name="win_sample",
    )(win_rows, q16ext)


def _prep_weights(w_in, mu, w2, a2, g2):
    o3 = 3 * D_RWKV
    nsa0 = R_COLS
    cols = [w_in[:, nsa0:nsa0 + D_NSA],
            w_in[:, 0:o3],
            w_in[:, nsa0 + D_NSA:nsa0 + D_NSA + 6 * KV_W],
            w_in[:, o3:R_COLS],
            w_in[:, nsa0 + D_NSA + 6 * KV_W:],
            jnp.zeros((D_MODEL, D_IN_PAD - C_GATE - 3 * N_HEADS), w_in.dtype)]
    w_in_p = jnp.concatenate(cols, axis=1).astype(jnp.bfloat16)
    z = jnp.zeros((LORA_ALL, D_RWKV), jnp.float32)
    return dict(w_in_p=w_in_p, mu_p=mu,
                w2p=z.at[0:LORA_W].set(w2), a2p=z.at[LORA_W:LORA_W + LORA_A].set(a2),
                g2p=z.at[LORA_W + LORA_A:].set(g2))


def _layer_weights(layer, g_mix, w_in, mu, w0, w2, a0, a2, g2, k_k, k_a, r_k, lnx_w, lnx_b, phi_pe, phi_w1,
                   phi_w2, w_o, g_ffn, w_gate, w_up, w_down):
    w = _prep_weights(w_in[layer], mu[layer], w2[layer], a2[layer], g2[layer])
    w1t, pe8, w2k, w2vt, w2bd = _phi_weights(phi_pe[layer], phi_w1[layer], phi_w2[layer])
    w1x = _expand_w1(phi_w1[layer])
    w.update(g_mix=g_mix[layer], w0=w0[layer], a0=a0[layer], k_k=k_k[layer], k_a=k_a[layer], r_k=r_k[layer],
             lnx_w=lnx_w[layer], lnx_b=lnx_b[layer], w1x=w1x, w1x_bf16=w1x.astype(jnp.bfloat16),
             pe=pe_term(pe8, w1t), w2k=w2k, w2vt=w2vt, w2bd=w2bd,
             wo=w_o[layer].astype(jnp.bfloat16), g_ffn=g_ffn[layer],
             wg=w_gate[layer].astype(jnp.bfloat16), wu=w_up[layer].astype(jnp.bfloat16),
             wd=w_down[layer].astype(jnp.bfloat16))
    return w


def _pick(n, pref):
    while n % pref:
        pref //= 2
    return pref


def _prompt_layer(x_prompt, W, g_final):
    B, T, _ = x_prompt.shape
    M = B * T
    x2 = x_prompt.reshape(M, D_MODEL)
    proj = proj_matmul(x2, W["g_mix"], W["w_in_p"], apply_norm=True, tm=_pick(M, 1024), tn=1024)
    proj3 = proj.reshape(B, T, D_IN_PAD)
    ro, rg, wkv = rwkv_mix(proj, jnp.zeros((B, 1, R_COLS), jnp.float32), W,
                           jnp.zeros((B, H_R, 64, 64), jnp.float32), b=B, t=T, tm=_pick(T, 256), tc=_pick(T, 64))
    cmp_kv = proj3[..., C_CMP:C_CMP + 2 * KV_W]
    slc_kv = proj3[..., C_SLC:C_SLC + 2 * KV_W]
    win_kv = proj3[..., C_WIN:C_WIN + 2 * KV_W]
    nch = T // D_CMP
    part = cmp_part_prompt(cmp_kv.reshape(B, nch, D_CMP * 2 * KV_W), W["w1x"])
    ovt = _overlap_t(128, nch - 1, nch).astype(jnp.bfloat16)
    o_cmp, sel = cmp_select_prompt(part, W["pe"], W["w2k"], W["w2vt"], proj3, ovt, tq=128)
    ks, vs = _kv_heads(slc_kv, B, T)
    kw, vw = _kv_blocks(win_kv, B, T)
    nsa = slc_win_prompt(proj3, ks, vs, kw, vw, sel, o_cmp, tq=128)
    r2 = lambda z: z.reshape(M, 1024)
    x1 = out_proj(r2(ro), r2(rg), r2(nsa), x2, W["wo"], tm=_pick(M, 512), tn=1024)
    y = ffn_final(x1, W["g_ffn"], W["wg"], W["wu"], W["wd"], g_final, tm=_pick(M, 512), tf=512)
    kv5 = lambda z: z.reshape(B, T, 2, N_KV, HEAD_DIM)
    nwin = min(WINDOW, T)
    return y.reshape(B, T, D_MODEL), kv5(cmp_kv), kv5(slc_kv), kv5(win_kv)[:, T - nwin:], wkv


def _sample_layer(x_sample, xn_s, cache_cmp, cache_slc, cache_win, wkv0, shift, page_table, W, g_final):
    b = x_sample.shape[0]
    n_pool, page = cache_cmp.shape[0], cache_cmp.shape[1]
    n_pages = page_table.shape[1]
    past_len = n_pages * page
    n_rows = -(-(past_len + 1) // L_SLC) * L_SLC
    n_blk = n_rows // L_SLC
    q_blk = past_len // L_SLC
    nbp = -(-n_blk // 128) * 128

    rows = jnp.concatenate([xn_s, shift], axis=0)
    proj = proj_matmul(rows, W["g_mix"], W["w_in_p"], apply_norm=False, tm=2 * b, tn=1024)
    prev = jnp.concatenate([proj[b:, C_R:C_R + 3 * D_RWKV], proj[b:, C_LORA:C_LORA + LORA_ALL]], axis=1)
    ro, rg, wkv = rwkv_mix(proj[:b], prev, W, wkv0, b=b, t=1, tm=1, tc=1)

    q = proj[:b, C_Q:C_Q + D_NSA].reshape(b, N_KV, GQA, 1, HEAD_DIM)
    eye = jnp.eye(N_KV, dtype=jnp.float32).reshape(1, N_KV, 1, N_KV, 1)
    qext = (q * eye).reshape(b, N_KV, GQA, KV_W)
    q16ext = qext.reshape(b, N_HEADS, KV_W)
    cmp_new = proj[:b, C_CMP:C_CMP + 2 * KV_W]
    slc_new = proj[:b, C_SLC:C_SLC + 2 * KV_W]
    win_new = proj[:b, C_WIN:C_WIN + 2 * KV_W]

    part = cmp_part_sample(cache_cmp.transpose(0, 2, 3, 4, 1), page_table, W["w1x_bf16"], pp=_pick(n_pages, 32))
    nc_rows = part.shape[1]
    ov = _overlap_t(nbp, n_rows // D_CMP - L_CMP // D_CMP + 1, nc_rows)[:n_blk].T
    ov = jnp.pad(ov, ((0, 0), (0, nbp - n_blk)))
    g8 = jnp.asarray((np.arange(8)[:, None] == np.arange(N_HEADS)[None, :] // GQA).astype(np.float32))
    o_cmp, imps = cmp_attn_sample(part, W["pe"], W["w2bd"], q16ext, g8, ov.astype(jnp.bfloat16), q_pos=past_len)
    scores = imps[:, :N_KV].transpose(2, 0, 1).reshape(nbp, b * N_KV)
    k_sel = min(N_SELECT, n_blk)
    idx = topk_blocks(scores, n_blk=n_blk, q_blk=q_blk, k_sel=k_sel)
    q4 = jnp.pad(q.reshape(b, N_KV, GQA, HEAD_DIM), ((0, 0), (0, 0), (0, 8 - GQA), (0, 0)))
    slc_new5 = slc_new.reshape(b, 2, N_KV, 1, HEAD_DIM)
    o_slc = slc_sample(idx.T.reshape(-1), page_table, cache_slc.transpose(0, 2, 3, 4, 1), q4,
                       slc_new5[:, 0], slc_new5[:, 1], n_sel=k_sel, new_blk=q_blk)
    o_slc = o_slc[:, :, :GQA].reshape(b, D_NSA)

    n_buf = cache_win.shape[1]
    win_all = jnp.concatenate([cache_win, win_new.reshape(b, 1, 2, N_KV, HEAD_DIM)], axis=1)
    win_keep = win_all[:, n_buf + 1 - min(WINDOW, n_buf + 1):]
    o_win = win_sample(win_keep.reshape(b, -1, 2 * KV_W), q16ext)

    x2 = x_sample.reshape(b, D_MODEL)
    gate_logits = proj[:b, C_GATE:C_GATE + 3 * N_HEADS].reshape(b, 3, N_HEADS, 1)
    nsa = nsa_combine_sample(o_cmp, o_slc.reshape(b, N_HEADS, HEAD_DIM), o_win, gate_logits)
    x1 = out_proj(ro.reshape(b, -1), rg.reshape(b, -1), nsa.reshape(b, D_NSA), x2, W["wo"], tm=b, tn=1024)
    y = ffn_final(x1, W["g_ffn"], W["wg"], W["wu"], W["wd"], g_final, tm=b, tf=512)
    kv5 = lambda z: z.reshape(b, 1, 2, N_KV, HEAD_DIM)
    return y.reshape(b, 1, D_MODEL), kv5(cmp_new), kv5(slc_new), win_keep, wkv


def kernel(x_prompt, x_sample, cache_cmp_kv, cache_slc_kv, cache_win_kv, state_wkv, state_shift, page_table, g_mix, w_in, mu, w0, w2, a0, a2, g2, k_k, k_a, r_k, lnx_w, lnx_b, phi_pe, phi_w1, phi_w2, w_o, g_ffn, w_gate, w_up, w_down, g_final):
    W = _layer_weights(0, g_mix, w_in, mu, w0, w2, a0, a2, g2, k_k, k_a, r_k, lnx_w, lnx_b, phi_pe, phi_w1,
                       phi_w2, w_o, g_ffn, w_gate, w_up, w_down)
    bp, bs = x_prompt.shape[0], x_sample.shape[0]
    small = jnp.concatenate([x_sample.reshape(bs, D_MODEL), x_prompt[:, -1]], axis=0)
    pad = (-small.shape[0]) % 8
    xn_small = rmsnorm_rows(jnp.pad(small, ((0, pad), (0, 0))), W["g_mix"])
    xn_s, shift_p = xn_small[:bs], xn_small[bs:bs + bp]

    y_p, cmp_p, slc_p, win_p, wkv_p = _prompt_layer(x_prompt, W, g_final)
    y_s, cmp_s, slc_s, win_s, wkv_s = _sample_layer(x_sample, xn_s, cache_cmp_kv[0], cache_slc_kv[0],
                                                    cache_win_kv[0], state_wkv[0], state_shift[0], page_table,
                                                    W, g_final)
    return (y_p, y_s, cmp_p[None], slc_p[None], win_p[None], wkv_p[None], shift_p[None],
            cmp_s[None], slc_s[None], win_s[None], wkv_s[None], xn_s[None])
```

```python
import functools

import jax
import jax.numpy as jnp
import numpy as np
from jax import lax
from jax.experimental import pallas as pl
from jax.experimental.pallas import tpu as pltpu

D_MODEL = 2048
D_RWKV = 1024
D_NSA = 1024
HEAD_DIM_R = 64
H_R = 16
LORA_W = 96
LORA_A = 96
LORA_G = 64
LORA_ALL = LORA_W + LORA_A + LORA_G
HEAD_DIM = 64
N_HEADS = 16
N_KV = 4
GQA = 4
KV_W = 256
L_CMP = 32
D_CMP = 16
L_SLC = 64
N_SELECT = 16
WINDOW = 512
PHI_HIDDEN = 64
R_COLS = 3 * D_RWKV + LORA_ALL
RMS_EPS = 1e-6
LNX_EPS = 64e-5
SCALE = HEAD_DIM ** -0.5
NEG = -1e30

C_Q, C_R, C_K, C_V = 0, 1024, 2048, 3072
C_CMP, C_SLC, C_WIN = 4096, 4608, 5120
C_LORA, C_GATE = 5632, 5888
D_IN_PAD = 6144

VMEM_LIMIT = 48 * 1024 * 1024
HI = lax.Precision.HIGHEST


def _cp(sem, limit=VMEM_LIMIT):
    return pltpu.CompilerParams(dimension_semantics=sem, vmem_limit_bytes=limit)


def _bdot(a, b):
    return jnp.dot(a.astype(jnp.bfloat16), b.astype(jnp.bfloat16), preferred_element_type=jnp.float32)


def _hdot(a, b):
    return jnp.dot(a, b, precision=HI, preferred_element_type=jnp.float32)


def _dot_nt(a, b, precision=None):
    return lax.dot_general(a, b, (((1,), (1,)), ((), ())), precision=precision,
                           preferred_element_type=jnp.float32)


def _sigmoid(x):
    return 1.0 / (1.0 + jnp.exp(-x))


def _silu(x):
    return x * _sigmoid(x)


def _rmsnorm_rows_kernel(x_ref, g_ref, o_ref):
    x = x_ref[...]
    ms = jnp.mean(x * x, axis=-1, keepdims=True)
    o_ref[...] = x * lax.rsqrt(ms + RMS_EPS) * g_ref[...]


def rmsnorm_rows(x, g):
    m, d = x.shape
    return pl.pallas_call(
        _rmsnorm_rows_kernel,
        out_shape=jax.ShapeDtypeStruct((m, d), jnp.float32),
        name="rmsnorm_rows",
    )(x, g.reshape(1, d))


def _proj_kernel(x_ref, g_ref, w_ref, o_ref, xn_ref, *, apply_norm):
    @pl.when(pl.program_id(1) == 0)
    def _():
        x = x_ref[...]
        if apply_norm:
            ms = jnp.mean(x * x, axis=-1, keepdims=True)
            x = x * lax.rsqrt(ms + RMS_EPS) * g_ref[...]
        xn_ref[...] = x.astype(jnp.bfloat16)

    o_ref[...] = jnp.dot(xn_ref[...], w_ref[...], preferred_element_type=jnp.float32)


def proj_matmul(x, g, w_bf16, *, apply_norm, tm, tn):
    m, d = x.shape
    n = w_bf16.shape[1]
    return pl.pallas_call(
        functools.partial(_proj_kernel, apply_norm=apply_norm),
        grid=(m // tm, n // tn),
        in_specs=[
            pl.BlockSpec((tm, d), lambda i, j: (i, 0)),
            pl.BlockSpec((1, d), lambda i, j: (0, 0)),
            pl.BlockSpec((d, tn), lambda i, j: (0, j)),
        ],
        out_specs=pl.BlockSpec((tm, tn), lambda i, j: (i, j)),
        out_shape=jax.ShapeDtypeStruct((m, n), jnp.float32),
        scratch_shapes=[pltpu.VMEM((tm, d), jnp.bfloat16)],
        compiler_params=_cp(("parallel", "arbitrary")),
        name="proj_matmul",
    )(x, g.reshape(1, d), w_bf16)


def _rwkv_prep_kernel(pr_ref, pk_ref, pv_ref, pl_ref, prev_ref, mu_ref, w0_ref, a0_ref, kk_ref, ka_ref,
                      w2_ref, a2_ref, g2_ref,
                      r_out, w_out, k_out, v_out, kk_out, a_out, g_out, carry_ref, *, row_prev):
    tm = pr_ref.shape[1]
    pr, pk, pv, plo = pr_ref[0], pk_ref[0], pv_ref[0], pl_ref[0]
    if row_prev:
        def shift_mix(p, lo, hi):
            return p + (prev_ref[0, :, lo:hi] - p) * mu_ref[:, lo:hi]
    else:
        @pl.when(pl.program_id(1) == 0)
        def _():
            carry_ref[...] = prev_ref[0]

        row0 = lax.broadcasted_iota(jnp.int32, (tm, 1), 0) == 0

        def shift_mix(p, lo, hi):
            prev = pltpu.roll(p, 1, axis=0)
            prev = jnp.where(row0, carry_ref[:, lo:hi], prev)
            return p + (prev - p) * mu_ref[:, lo:hi]

    xr = shift_mix(pr, 0, 1024)
    xk = shift_mix(pk, 1024, 2048)
    xv = shift_mix(pv, 2048, 3072)
    xl = shift_mix(plo, 3072, 3328)
    if not row_prev:
        last = tm - 1
        carry_ref[:, 0:1024] = pr[last:last + 1]
        carry_ref[:, 1024:2048] = pk[last:last + 1]
        carry_ref[:, 2048:3072] = pv[last:last + 1]
        carry_ref[:, 3072:3328] = plo[last:last + 1]

    wl = _dot3(_split2(jnp.tanh(xl)), _split2(w2_ref[...]))
    al = _dot3(_split2(xl), _split2(a2_ref[...]))
    gl = _dot3(_split2(_sigmoid(xl)), _split2(g2_ref[...]))
    z = -(w0_ref[...] + wl)
    softplus = jnp.maximum(z, 0.0) + jnp.log(1.0 + jnp.exp(-jnp.abs(z)))
    w_log = -softplus - 0.5
    decay = jnp.exp(-jnp.exp(w_log))
    a = _sigmoid(a0_ref[...] + al)
    r_out[0] = xr
    w_out[0] = decay
    k_out[0] = xk * (1.0 + (a - 1.0) * ka_ref[...])
    v_out[0] = xv
    kk_out[0] = xk * kk_ref[...]
    a_out[0] = a
    g_out[0] = gl


def rwkv_prep(proj, prev, mu_p, w0, a0, k_k, k_a, w2p, a2p, g2p, *, b, t, tm, row_prev=False):
    proj3 = proj.reshape(b, t, D_IN_PAD)
    nt = t // tm
    prev_spec = (pl.BlockSpec((1, tm, R_COLS), lambda bi, ti: (bi, ti, 0)) if row_prev
                 else pl.BlockSpec((1, 1, R_COLS), lambda bi, ti: (bi, 0, 0)))
    row = lambda blk, w: pl.BlockSpec((1, tm, w), lambda bi, ti: (bi, ti, blk))
    vec = lambda w: pl.BlockSpec((1, w), lambda bi, ti: (0, 0))
    mat = pl.BlockSpec((LORA_ALL, D_RWKV), lambda bi, ti: (0, 0))
    out = pl.BlockSpec((1, tm, D_RWKV), lambda bi, ti: (bi, ti, 0))
    shp = jax.ShapeDtypeStruct((b, t, D_RWKV), jnp.float32)
    return pl.pallas_call(
        functools.partial(_rwkv_prep_kernel, row_prev=row_prev),
        grid=(b, nt),
        in_specs=[row(C_R // 1024, 1024), row(C_K // 1024, 1024), row(C_V // 1024, 1024),
                  row(C_LORA // 256, 256),
                  prev_spec,
                  vec(R_COLS), vec(D_RWKV), vec(D_RWKV), vec(D_RWKV), vec(D_RWKV),
                  mat, mat, mat],
        out_specs=[out] * 7,
        out_shape=[shp] * 7,
        scratch_shapes=[pltpu.VMEM((1, R_COLS), jnp.float32)],
        compiler_params=_cp(("parallel", "arbitrary")),
        name="rwkv_prep",
    )(proj3, proj3, proj3, proj3, prev, mu_p.reshape(1, -1), w0.reshape(1, -1), a0.reshape(1, -1),
      k_k.reshape(1, -1), k_a.reshape(1, -1), w2p, a2p, g2p)


def _rwkv_scan_kernel(r_ref, w_ref, k_ref, kk_ref, a_ref, v_ref, s0_ref, rk_ref, lnw_ref, lnb_ref,
                      o_ref, sfin_ref, s_ref, kkn_ref, b_ref, y_ref, *, halved):
    tc = r_ref.shape[1]
    ni = v_ref.shape[2]

    @pl.when(pl.program_id(1) == 0)
    def _():
        s_ref[...] = s0_ref[0]

    kkraw = kk_ref[0]
    nrm = jnp.sqrt(jnp.sum(kkraw * kkraw, axis=1, keepdims=True))
    kkn = kkraw / jnp.maximum(nrm, 1e-12)
    kkn_ref[...] = kkn
    b_ref[...] = kkn * a_ref[0]

    def step(t, carry):
        w_t = w_ref[0, t]
        k_t = k_ref[0, t]
        r_t = r_ref[0, t]
        kk_t = kkn_ref[t]
        b_t = b_ref[t]
        for i in range(ni):
            s = s_ref[i]
            sa = -jnp.sum(s * kk_t, axis=0, keepdims=True)
            v_i = v_ref[0, t, pl.ds(i, 1), :]
            s = s * w_t + sa * b_t + v_i * k_t
            y_ref[t, pl.ds(i, 1), :] = jnp.sum(s * r_t, axis=0, keepdims=True)
            s_ref[i] = s
        return carry

    lax.fori_loop(0, tc, step, 0)

    y = y_ref[...]
    v = v_ref[0]

    def head_sum(z):
        s = jnp.sum(z, axis=1, keepdims=True)
        if halved:
            s = s + pltpu.roll(s, 64, axis=2)
        return s

    mean = head_sum(y) * (1.0 / HEAD_DIM_R)
    d = y - mean
    var = head_sum(d * d) * (1.0 / HEAD_DIM_R)
    yn = d * lax.rsqrt(var + LNX_EPS) * lnw_ref[0] + lnb_ref[0]
    bonus = jnp.sum(r_ref[0] * k_ref[0] * rk_ref[0], axis=1, keepdims=True)
    o_ref[0] = yn + bonus * v

    @pl.when(pl.program_id(1) == pl.num_programs(1) - 1)
    def _():
        sfin_ref[0] = s_ref[...]


def rwkv_scan(r, w, k, kk, a, v, s0, rk, lnw, lnb, *, tc, halved):
    g, t = r.shape[0], r.shape[1]
    ni = v.shape[2]
    col = pl.BlockSpec((1, tc, 64, 128), lambda gi, ti: (gi, ti, 0, 0))
    vspec = pl.BlockSpec((1, tc, ni, 128), lambda gi, ti: (gi, ti, 0, 0))
    sspec = pl.BlockSpec((1, ni, 64, 128), lambda gi, ti: (gi, 0, 0, 0))
    return pl.pallas_call(
        functools.partial(_rwkv_scan_kernel, halved=halved),
        grid=(g, t // tc),
        in_specs=[col, col, col, col, col, vspec, sspec,
                  pl.BlockSpec((1, 64, 128), lambda gi, ti: (gi, 0, 0)),
                  pl.BlockSpec((1, ni, 128), lambda gi, ti: (gi, 0, 0)),
                  pl.BlockSpec((1, ni, 128), lambda gi, ti: (gi, 0, 0))],
        out_specs=[vspec, sspec],
        out_shape=[jax.ShapeDtypeStruct((g, t, ni, 128), jnp.float32),
                   jax.ShapeDtypeStruct((g, ni, 64, 128), jnp.float32)],
        scratch_shapes=[pltpu.VMEM((ni, 64, 128), jnp.float32),
                        pltpu.VMEM((tc, 64, 128), jnp.float32),
                        pltpu.VMEM((tc, 64, 128), jnp.float32),
                        pltpu.VMEM((tc, ni, 128), jnp.float32)],
        compiler_params=_cp(("parallel", "arbitrary")),
        name="rwkv_scan",
    )(r, w, k, kk, a, v, s0, rk, lnw, lnb)


def _to_scan_layout(x, b, t, halved):
    z = x.reshape(b, t, H_R, 64).transpose(1, 3, 0, 2).reshape(t, 64, b * H_R)
    if halved:
        return jnp.concatenate([z, z], axis=-1)[None]
    g = (b * H_R) // 128
    return z.reshape(t, 64, g, 128).transpose(2, 0, 1, 3)


def _v_to_scan_layout(x, b, t, halved):
    if halved:
        return x.reshape(b, t, H_R, 2, 32).transpose(1, 4, 3, 0, 2).reshape(1, t, 32, 128)
    g = (b * H_R) // 128
    z = x.reshape(b, t, H_R, 64).transpose(1, 3, 0, 2).reshape(t, 64, g, 128)
    return z.transpose(2, 0, 1, 3)


def _vecparam_scan_layout(p, b, halved):
    z = jnp.broadcast_to(p.reshape(1, H_R, 64), (b, H_R, 64))
    if halved:
        return z.reshape(b, H_R, 2, 32).transpose(3, 2, 0, 1).reshape(1, 32, 128)
    g = (b * H_R) // 128
    return z.transpose(2, 0, 1).reshape(64, g, 128).transpose(1, 0, 2)


def _keyparam_scan_layout(p, b, halved):
    z = jnp.broadcast_to(p.reshape(1, H_R, 64), (b, H_R, 64)).transpose(2, 0, 1).reshape(64, b * H_R)
    if halved:
        return jnp.concatenate([z, z], axis=-1)[None]
    g = (b * H_R) // 128
    return z.reshape(64, g, 128).transpose(1, 0, 2)


def _from_scan_layout(o, b, t, halved):
    if halved:
        return o.reshape(t, 32, 2, b, H_R).transpose(3, 0, 4, 2, 1).reshape(b, t, D_RWKV)
    g = o.shape[0]
    return o.transpose(1, 2, 0, 3).reshape(t, 64, b, H_R).transpose(2, 0, 3, 1).reshape(b, t, D_RWKV)


def _state_to_scan_layout(s, b, halved):
    if halved:
        return s.reshape(b, H_R, 2, 32, 64).transpose(3, 4, 2, 0, 1).reshape(1, 32, 64, 128)
    g = (b * H_R) // 128
    return s.reshape(g, 128, 64, 64).transpose(0, 2, 3, 1)


def _state_from_scan_layout(s, b, halved):
    if halved:
        return s.reshape(32, 64, 2, b, H_R).transpose(3, 4, 2, 0, 1).reshape(b, H_R, 64, 64)
    return s.transpose(0, 3, 1, 2).reshape(b, H_R, 64, 64)


def rwkv_mix(proj, prev, wts, s0, *, b, t, tm, tc):
    halved = (b * H_R == 64)
    if t == 1:
        outs = rwkv_prep(proj, prev.reshape(1, b, R_COLS), wts["mu_p"], wts["w0"], wts["a0"], wts["k_k"],
                         wts["k_a"], wts["w2p"], wts["a2p"], wts["g2p"], b=1, t=b, tm=b, row_prev=True)
        r, w, k, v, kk, a, g = [z.reshape(b, 1, D_RWKV) for z in outs]
    else:
        r, w, k, v, kk, a, g = rwkv_prep(proj, prev, wts["mu_p"], wts["w0"], wts["a0"], wts["k_k"], wts["k_a"],
                                         wts["w2p"], wts["a2p"], wts["g2p"], b=b, t=t, tm=tm)
    tr = lambda z: _to_scan_layout(z, b, t, halved)
    o, sfin = rwkv_scan(tr(r), tr(w), tr(k), tr(kk), tr(a), _v_to_scan_layout(v, b, t, halved),
                        _state_to_scan_layout(s0, b, halved),
                        _keyparam_scan_layout(wts["r_k"], b, halved),
                        _vecparam_scan_layout(wts["lnx_w"], b, halved),
                        _vecparam_scan_layout(wts["lnx_b"], b, halved),
                        tc=tc, halved=halved)
    return _from_scan_layout(o, b, t, halved), g, _state_from_scan_layout(sfin, b, halved)


def _cmp_part_kernel(*refs, n_in, precise, prefetch):
    refs = refs[prefetch:]
    x_refs, w_ref, o_ref = refs[:n_in], refs[n_in], refs[n_in + 1]
    for e in range(2):
        for hp in range(2):
            acc = None
            for j in range(D_CMP):
                lo = j * 512 + e * 256 + hp * 128
                xs = [xr[0, :, lo:lo + 128] for xr in x_refs]
                x = xs[0] if n_in == 1 else jnp.concatenate(xs, axis=0)
                d = _hdot(x, w_ref[e, j]) if precise else _bdot(x, w_ref[e, j])
                acc = d if acc is None else acc + d
            c0 = e * 256 + hp * 128
            o_ref[0, :, c0:c0 + 128] = acc[:, :128]
            o_ref[0, :, 512 + c0:512 + c0 + 128] = acc[:, 128:]


def cmp_part_prompt(rows, w1x):
    b, nch, _ = rows.shape
    return pl.pallas_call(
        functools.partial(_cmp_part_kernel, n_in=1, precise=True, prefetch=0),
        grid=(b,),
        in_specs=[pl.BlockSpec((1, nch, 8192), lambda bi: (bi, 0, 0)),
                  pl.BlockSpec((2, D_CMP, 128, 256), lambda bi: (0, 0, 0, 0))],
        out_specs=pl.BlockSpec((1, nch, 1024), lambda bi: (bi, 0, 0)),
        out_shape=jax.ShapeDtypeStruct((b, nch, 1024), jnp.float32),
        compiler_params=_cp(("parallel",)),
        name="cmp_part_prompt",
    )(rows, w1x)


def _expand_w1(phi_w1):
    w1 = phi_w1.reshape(2, 2, D_CMP, HEAD_DIM, PHI_HIDDEN)
    eye = jnp.eye(2, dtype=phi_w1.dtype)
    w = jnp.einsum("ab,esjdf->ejadsbf", eye, w1)
    return w.reshape(2, D_CMP, 128, 256)


def _split2(x):
    hi = x.astype(jnp.bfloat16)
    return hi, (x - hi.astype(jnp.float32)).astype(jnp.bfloat16)


def _dot3(a, b):
    (ah, al), (bh, bl) = a, b
    d = lambda x, y: jnp.dot(x, y, preferred_element_type=jnp.float32)
    return d(ah, bh) + (d(ah, bl) + d(al, bh))


def _pe_term_kernel(pe_ref, w1t_ref, o_ref):
    for e in range(2):
        o_ref[e] = _hdot(pe_ref[e], w1t_ref[e])


def pe_term(pe8, w1t):
    return pl.pallas_call(
        _pe_term_kernel,
        out_shape=jax.ShapeDtypeStruct((2, 8, N_KV * PHI_HIDDEN), jnp.float32),
        name="pe_term",
    )(pe8, w1t)


def _cmp_hidden(part, pe_ref, e):
    n = part.shape[0]
    lo = e * 256
    h = part[:, lo:lo + 256] + pltpu.roll(part[:, 512 + lo:512 + lo + 256], n - 1, axis=0)
    return _silu(h + pe_ref[e, 0:1])


def _cmp_sel_kernel(part_ref, pe_ref, w2k_ref, w2vt_ref, q_ref, ovt_ref, gt_ref, o_ref, sel_ref,
                    kch_ref, kcl_ref, vct_ref, *, tq, n_blk):
    ti = pl.program_id(1)

    @pl.when(ti == 0)
    def _():
        part = part_ref[0]
        hk = _split2(_cmp_hidden(part, pe_ref, 0))
        hv = _cmp_hidden(part, pe_ref, 1)
        for kvh in range(N_KV):
            kch_ref[kvh], kcl_ref[kvh] = _split2(_dot3(hk, _split2(w2k_ref[kvh])))
            vct_ref[kvh] = _dot_nt(w2vt_ref[kvh], hv, HI).astype(jnp.bfloat16)

    nc = kch_ref.shape[1]
    nbp = -(-n_blk // 8) * 8
    k_sel = min(N_SELECT, n_blk)
    qpos = ti * tq + lax.broadcasted_iota(jnp.int32, (1, tq), 1)
    cidx = lax.broadcasted_iota(jnp.int32, (nc, 1), 0)
    cmask = (cidx * D_CMP + (L_CMP - 1)) <= qpos
    sidx = cidx[:nbp]
    qblk = qpos // L_SLC
    forced = (sidx == 0) | (sidx == qblk) | (sidx == qblk - 1)
    causal = sidx <= qblk
    pad = jnp.zeros((128 - nbp, tq), jnp.float32)
    need_rank = (ti + 1) * tq > k_sel * L_SLC

    cmask4 = jnp.concatenate([cmask.astype(jnp.float32)] * GQA, axis=1) > 0.5
    gate_t = _sigmoid(gt_ref[0][:, 0:128]).T
    for kvh in range(N_KV):
        qt = (q_ref[0, :, kvh * 256:(kvh + 1) * 256] * SCALE).T
        qt4 = _split2(jnp.concatenate([qt[g * 64:(g + 1) * 64] for g in range(GQA)], axis=1))
        st = _dot3((kch_ref[kvh], kcl_ref[kvh]), qt4)
        st = jnp.where(cmask4, st, NEG)
        m = jnp.max(st, axis=0, keepdims=True)
        ex = jnp.where(cmask4, jnp.exp(st - m), 0.0)
        pt = ex / jnp.maximum(jnp.sum(ex, axis=0, keepdims=True), 1e-30)
        ot = jnp.dot(vct_ref[kvh], pt.astype(jnp.bfloat16), preferred_element_type=jnp.float32)
        gate4 = jnp.concatenate([gate_t[kvh * GQA + g:kvh * GQA + g + 1] for g in range(GQA)], axis=1)
        o_ref[0, :, kvh * 256:(kvh + 1) * 256] = jnp.concatenate(
            [(ot * gate4)[:, g * tq:(g + 1) * tq] for g in range(GQA)], axis=0).T
        impt = pt[:, 0:tq]
        for g in range(1, GQA):
            impt = impt + pt[:, g * tq:(g + 1) * tq]

        @pl.when(need_rank)
        def _():
            ih, il = _split2(impt)
            ov = ovt_ref[...]
            imps = (jnp.dot(ov, ih, preferred_element_type=jnp.float32)
                    + jnp.dot(ov, il, preferred_element_type=jnp.float32))[:nbp]
            score = jnp.where(forced, -NEG, jnp.where(causal, imps, NEG))
            rank = jnp.zeros((nbp, tq), jnp.float32)
            for s2 in range(n_blk):
                row = score[s2:s2 + 1]
                beats = (row > score) | ((row == score) & (s2 < sidx))
                rank = rank + beats.astype(jnp.float32)
            seln = ((rank < k_sel) & causal).astype(jnp.float32)
            sel_ref[0, kvh] = jnp.concatenate([seln, pad], axis=0)

        @pl.when(jnp.logical_not(need_rank))
        def _():
            sel_ref[0, kvh] = jnp.concatenate([causal.astype(jnp.float32), pad], axis=0)


def cmp_select_prompt(part, pe, w2k, w2vt, proj3, ovt_bf16, *, tq):
    b, nc, _ = part.shape
    t = proj3.shape[1]
    c3 = lambda bi, ti: (0, 0, 0)
    return pl.pallas_call(
        functools.partial(_cmp_sel_kernel, tq=tq, n_blk=t // L_SLC),
        grid=(b, t // tq),
        in_specs=[pl.BlockSpec((1, nc, 1024), lambda bi, ti: (bi, 0, 0)),
                  pl.BlockSpec((2, 8, 256), c3),
                  pl.BlockSpec((N_KV, 256, 64), c3),
                  pl.BlockSpec((N_KV, 64, 256), c3),
                  pl.BlockSpec((1, tq, 1024), lambda bi, ti: (bi, ti, C_Q // 1024)),
                  pl.BlockSpec((128, nc), lambda bi, ti: (0, 0)),
                  pl.BlockSpec((1, tq, 256), lambda bi, ti: (bi, ti, C_GATE // 256))],
        out_specs=[pl.BlockSpec((1, tq, 1024), lambda bi, ti: (bi, ti, 0)),
                   pl.BlockSpec((1, N_KV, 128, tq), lambda bi, ti: (bi, 0, 0, ti))],
        out_shape=[jax.ShapeDtypeStruct((b, t, 1024), jnp.float32),
                   jax.ShapeDtypeStruct((b, N_KV, 128, t), jnp.float32)],
        scratch_shapes=[pltpu.VMEM((N_KV, nc, 64), jnp.bfloat16),
                        pltpu.VMEM((N_KV, nc, 64), jnp.bfloat16),
                        pltpu.VMEM((N_KV, 64, nc), jnp.bfloat16)],
        compiler_params=_cp(("parallel", "arbitrary")),
        name="cmp_select_prompt",
    )(part, pe, w2k, w2vt, proj3, ovt_bf16, proj3)


def _overlap_t(n_blocks_pad, nc, nc_pad):
    c_start = np.arange(nc_pad) * D_CMP
    s_start = np.arange(n_blocks_pad) * L_SLC
    ov = (c_start[None, :] < s_start[:, None] + L_SLC) & (c_start[None, :] + L_CMP > s_start[:, None])
    ov &= (np.arange(nc_pad) < nc)[None, :]
    return jnp.asarray(ov.astype(np.float32))


def _phi_weights(phi_pe, phi_w1, phi_w2):
    w1t = jnp.tile(phi_w1, (1, 1, N_KV))
    pe8 = jnp.broadcast_to(phi_pe.reshape(2, 1, L_CMP * HEAD_DIM), (2, 8, L_CMP * HEAD_DIM))
    z = jnp.zeros((N_KV, N_KV, PHI_HIDDEN, HEAD_DIM), jnp.float32)
    idx = jnp.arange(N_KV)
    w2k = z.at[idx, idx].set(phi_w2[0]).reshape(N_KV, 256, 64)
    w2vt = jnp.transpose(z.at[idx, idx].set(phi_w2[1]), (0, 3, 1, 2)).reshape(N_KV, 64, 256)
    eye = jnp.eye(N_KV, dtype=jnp.float32)
    w2bd = jnp.einsum("ab,efd->eafbd", eye, phi_w2).reshape(2, 256, 256)
    return w1t, pe8, w2k, w2vt, w2bd


def _gate_row(gt_ref, branch, kvh, tq):
    gt = _sigmoid(gt_ref[0][:, 0:128]).T
    col = lax.broadcasted_iota(jnp.int32, (128, 1), 0)
    rows = [jnp.sum(jnp.where(col == branch * N_HEADS + kvh * GQA + g, gt, 0.0), axis=0, keepdims=True)
            for g in range(GQA)]
    return jnp.concatenate(rows, axis=1)


def _heads_to_rows(o, tq):
    return jnp.concatenate([o[:, g * tq:(g + 1) * tq] for g in range(GQA)], axis=0).T


def _attend_t(k, vt, qt4, bias, tq):
    s = jnp.dot(k, qt4, preferred_element_type=jnp.float32) + jnp.concatenate([bias] * GQA, axis=1)
    s = s.astype(jnp.bfloat16)
    p = jnp.exp(s - jnp.max(s, axis=0, keepdims=True))
    vt1 = jnp.concatenate([vt, jnp.ones((16, vt.shape[1]), jnp.bfloat16)], axis=0)
    ol = jnp.dot(vt1, p, preferred_element_type=jnp.float32)
    return ol[:HEAD_DIM] / jnp.maximum(ol[HEAD_DIM:HEAD_DIM + 1], 1e-30)


SLC_KEY_STEP = 256
WIN_BLOCKS = WINDOW // 128 + 1


def _slc_win_kernel(q_ref, ks_ref, vs_ref, kw_ref, vw_ref, sel_ref, ex_ref, gt_ref, oc_ref, o_ref, acc_ref,
                    *, tq):
    kvh = pl.program_id(1)
    ti = pl.program_id(2)
    t = ks_ref.shape[2]
    qt = (q_ref[0] * SCALE).T.astype(jnp.bfloat16)
    qt4 = jnp.concatenate([qt[g * HEAD_DIM:(g + 1) * HEAD_DIM] for g in range(GQA)], axis=1)
    qpos = ti * tq + lax.broadcasted_iota(jnp.int32, (1, tq), 1)
    selt = sel_ref[0, 0].astype(jnp.bfloat16)

    n_var = -(-t // SLC_KEY_STEP)
    for c in range(n_var):
        nk = min((c + 1) * SLC_KEY_STEP, t)

        @pl.when((ti * tq) // SLC_KEY_STEP == c)
        def _():
            on = jnp.dot(ex_ref[0:nk], selt, preferred_element_type=jnp.float32)
            kpos = lax.broadcasted_iota(jnp.int32, (nk, 1), 0)
            bias = (on - 1.0) * (-NEG) + jnp.where(kpos <= qpos, 0.0, NEG)
            acc_ref[...] = _attend_t(ks_ref[0, 0, 0:nk], vs_ref[0, 0, :, 0:nk], qt4, bias, tq)

    nkb = kw_ref.shape[2]
    w0 = jnp.clip(ti - (WIN_BLOCKS - 1), 0, nkb - WIN_BLOCKS)
    kw = kw_ref[0, 0, pl.ds(w0, WIN_BLOCKS)].reshape(WIN_BLOCKS * 128, HEAD_DIM)
    vwt = jnp.concatenate([vw_ref[0, 0, w0 + i] for i in range(WIN_BLOCKS)], axis=1)
    diff = qpos - (w0 * 128 + lax.broadcasted_iota(jnp.int32, (WIN_BLOCKS * 128, 1), 0))
    o_win = _attend_t(kw, vwt, qt4, jnp.where((diff >= 0) & (diff < WINDOW), 0.0, NEG), tq)
    o = acc_ref[...] * _gate_row(gt_ref, 1, kvh, tq) + o_win * _gate_row(gt_ref, 2, kvh, tq)
    o_ref[0] = (_heads_to_rows(o, tq) + oc_ref[0]).astype(jnp.bfloat16)


def slc_win_prompt(proj3, ks, vst, kw, vwt, selt, o_cmp, *, tq):
    b, t, _ = proj3.shape
    nkb = t // 128
    expand = jnp.asarray((np.arange(t)[:, None] // L_SLC == np.arange(128)[None, :]).astype(np.float32),
                         jnp.bfloat16)
    i4 = lambda bi, hi, ti: (bi, hi, 0, 0)
    i5 = lambda bi, hi, ti: (bi, hi, 0, 0, 0)
    head_cols = pl.BlockSpec((1, tq, 256), lambda bi, hi, ti: (bi, ti, hi))
    return pl.pallas_call(
        functools.partial(_slc_win_kernel, tq=tq),
        grid=(b, N_KV, t // tq),
        in_specs=[head_cols,
                  pl.BlockSpec((1, 1, t, HEAD_DIM), i4),
                  pl.BlockSpec((1, 1, HEAD_DIM, t), i4),
                  pl.BlockSpec((1, 1, nkb, 128, HEAD_DIM), i5),
                  pl.BlockSpec((1, 1, nkb, HEAD_DIM, 128), i5),
                  pl.BlockSpec((1, 1, 128, tq), lambda bi, hi, ti: (bi, hi, 0, ti)),
                  pl.BlockSpec((t, 128), lambda bi, hi, ti: (0, 0)),
                  pl.BlockSpec((1, tq, 256), lambda bi, hi, ti: (bi, ti, C_GATE // 256)),
                  head_cols],
        out_specs=head_cols,
        out_shape=jax.ShapeDtypeStruct((b, t, 1024), jnp.bfloat16),
        scratch_shapes=[pltpu.VMEM((HEAD_DIM, GQA * tq), jnp.float32)],
        compiler_params=_cp(("parallel", "parallel", "arbitrary")),
        name="slc_win_prompt",
    )(proj3, ks, vst, kw, vwt, selt, expand, proj3, o_cmp)


def _kv_heads(kv_rows, b, t):
    kv = kv_rows.reshape(b, t, 2, N_KV, HEAD_DIM).astype(jnp.bfloat16)
    return kv[:, :, 0].transpose(0, 2, 1, 3), kv[:, :, 1].transpose(0, 2, 3, 1)


def _kv_blocks(kv_rows, b, t):
    k, vt = _kv_heads(kv_rows, b, t)
    return (k.reshape(b, N_KV, t // 128, 128, HEAD_DIM),
            vt.reshape(b, N_KV, HEAD_DIM, t // 128, 128).transpose(0, 1, 3, 2, 4))


def _out_proj_kernel(ro_ref, rg_ref, nsa_ref, x_ref, wo_ref, y_ref, a_ref):
    @pl.when(pl.program_id(1) == 0)
    def _():
        a_ref[:, 0:D_RWKV] = (ro_ref[...] * rg_ref[...]).astype(jnp.bfloat16)
        a_ref[:, D_RWKV:D_MODEL] = nsa_ref[...]

    y_ref[...] = x_ref[...] + jnp.dot(a_ref[...], wo_ref[...], preferred_element_type=jnp.float32)


def out_proj(ro, rg, nsa_bf16, x, wo_bf16, *, tm, tn):
    m = x.shape[0]
    row = lambda w: pl.BlockSpec((tm, w), lambda i, j: (i, 0))
    return pl.pallas_call(
        _out_proj_kernel,
        grid=(m // tm, D_MODEL // tn),
        in_specs=[row(D_RWKV), row(D_RWKV), row(D_NSA),
                  pl.BlockSpec((tm, tn), lambda i, j: (i, j)),
                  pl.BlockSpec((D_MODEL, tn), lambda i, j: (0, j))],
        out_specs=pl.BlockSpec((tm, tn), lambda i, j: (i, j)),
        out_shape=jax.ShapeDtypeStruct((m, D_MODEL), jnp.float32),
        scratch_shapes=[pltpu.VMEM((tm, D_MODEL), jnp.bfloat16)],
        compiler_params=_cp(("parallel", "arbitrary")),
        name="out_proj",
    )(ro, rg, nsa_bf16, x, wo_bf16)


def _nsa_combine_kernel(oc_ref, os_ref, ow_ref, gt_ref, o_ref):
    g = _sigmoid(gt_ref[...])
    o_ref[...] = (g[:, 0] * oc_ref[...] + g[:, 1] * os_ref[...] + g[:, 2] * ow_ref[...]).astype(jnp.bfloat16)


def nsa_combine_sample(o_cmp, o_slc, o_win, gate_logits):
    return pl.pallas_call(
        _nsa_combine_kernel,
        out_shape=jax.ShapeDtypeStruct(o_cmp.shape, jnp.bfloat16),
        name="nsa_combine_sample",
    )(o_cmp, o_slc, o_win, gate_logits)


def _ffn_kernel(x_ref, gf_ref, wg_ref, wu_ref, wd_ref, gl_ref, y_ref, h_ref, acc_ref):
    f = pl.program_id(1)

    @pl.when(f == 0)
    def _():
        x = x_ref[...]
        ms = jnp.mean(x * x, axis=-1, keepdims=True)
        h_ref[...] = (x * lax.rsqrt(ms + RMS_EPS) * gf_ref[...]).astype(jnp.bfloat16)
        acc_ref[...] = jnp.zeros_like(acc_ref)

    h = h_ref[...]
    gate = jnp.dot(h, wg_ref[...], preferred_element_type=jnp.float32)
    up = jnp.dot(h, wu_ref[...], preferred_element_type=jnp.float32)
    act = (_silu(gate) * up).astype(jnp.bfloat16)
    acc_ref[...] += jnp.dot(act, wd_ref[...], preferred_element_type=jnp.float32)

    @pl.when(f == pl.num_programs(1) - 1)
    def _():
        z = x_ref[...] + acc_ref[...]
        ms = jnp.mean(z * z, axis=-1, keepdims=True)
        y_ref[...] = z * lax.rsqrt(ms + RMS_EPS) * gl_ref[...]


def ffn_final(x, g_ffn, wg, wu, wd, g_final, *, tm, tf):
    m = x.shape[0]
    dff = wg.shape[1]
    return pl.pallas_call(
        _ffn_kernel,
        grid=(m // tm, dff // tf),
        in_specs=[pl.BlockSpec((tm, D_MODEL), lambda i, f: (i, 0)),
                  pl.BlockSpec((1, D_MODEL), lambda i, f: (0, 0)),
                  pl.BlockSpec((D_MODEL, tf), lambda i, f: (0, f)),
                  pl.BlockSpec((D_MODEL, tf), lambda i, f: (0, f)),
                  pl.BlockSpec((tf, D_MODEL), lambda i, f: (f, 0)),
                  pl.BlockSpec((1, D_MODEL), lambda i, f: (0, 0))],
        out_specs=pl.BlockSpec((tm, D_MODEL), lambda i, f: (i, 0)),
        out_shape=jax.ShapeDtypeStruct((m, D_MODEL), jnp.float32),
        scratch_shapes=[pltpu.VMEM((tm, D_MODEL), jnp.bfloat16), pltpu.VMEM((tm, D_MODEL), jnp.float32)],
        compiler_params=_cp(("parallel", "arbitrary")),
        name="ffn_final",
    )(x, g_ffn.reshape(1, -1), wg, wu, wd, g_final.reshape(1, -1))


def _cmp_part_paged_kernel(pt_ref, *refs, n_in):
    x_refs = refs[:n_in]
    perm_ref, w_ref, o_ref, xs_ref = refs[n_in:]
    cpp = x_refs[0].shape[-1] // D_CMP
    n_pair = 2 * (N_KV // 2)
    for p in range(n_in):
        x = x_refs[p][0].reshape(2 * N_KV * HEAD_DIM, x_refs[p].shape[-1])
        xp_all = jnp.dot(x.astype(jnp.bfloat16), perm_ref[...], preferred_element_type=jnp.float32)
        for ep in range(n_pair):
            xp = xp_all[ep * 128:(ep + 1) * 128].T
            for j in range(D_CMP):
                xs_ref[ep, j, p * cpp:(p + 1) * cpp, :] = xp[j * cpp:(j + 1) * cpp]
    for e in range(2):
        for hp in range(2):
            acc = None
            for j in range(0, D_CMP, 2):
                x2 = jnp.concatenate([xs_ref[e * 2 + hp, j], xs_ref[e * 2 + hp, j + 1]], axis=1)
                w2 = jnp.concatenate([w_ref[e, j], w_ref[e, j + 1]], axis=0)
                d = _bdot(x2, w2)
                acc = d if acc is None else acc + d
            c0 = e * 256 + hp * 128
            o_ref[0, :, c0:c0 + 128] = acc[:, :128]
            o_ref[0, :, 512 + c0:512 + c0 + 128] = acc[:, 128:]


def cmp_part_sample(pool_t, page_table, w1x_bf16, *, pp):
    b, n_pages = page_table.shape
    page = pool_t.shape[-1]
    cpp = page // D_CMP

    def xspec(k):
        return pl.BlockSpec((1, 2, N_KV, HEAD_DIM, page), lambda bi, gi, pt: (pt[bi, gi * pp + k], 0, 0, 0, 0))

    r = np.arange(page)
    perm = np.zeros((page, page), np.float32)
    perm[r, (r % D_CMP) * cpp + r // D_CMP] = 1.0
    grid_spec = pltpu.PrefetchScalarGridSpec(
        num_scalar_prefetch=1,
        grid=(b, n_pages // pp),
        in_specs=[xspec(k) for k in range(pp)]
        + [pl.BlockSpec((page, page), lambda bi, gi, pt: (0, 0)),
           pl.BlockSpec((2, D_CMP, 128, 256), lambda bi, gi, pt: (0, 0, 0, 0))],
        out_specs=pl.BlockSpec((1, cpp * pp, 1024), lambda bi, gi, pt: (bi, gi, 0)),
        scratch_shapes=[pltpu.VMEM((N_KV, D_CMP, pp * cpp, 128), jnp.float32)],
    )
    return pl.pallas_call(
        functools.partial(_cmp_part_paged_kernel, n_in=pp),
        grid_spec=grid_spec,
        out_shape=jax.ShapeDtypeStruct((b, n_pages * cpp, 1024), jnp.float32),
        compiler_params=_cp(("parallel", "arbitrary")),
        name="cmp_part_sample",
    )(page_table, *([pool_t] * pp), jnp.asarray(perm, jnp.bfloat16), w1x_bf16)


def _fold_heads(o_ext):
    row_kvh = lax.broadcasted_iota(jnp.int32, (N_HEADS, 1), 0) // GQA
    out = jnp.zeros((N_HEADS, HEAD_DIM), jnp.float32)
    for kvh in range(N_KV):
        out = out + jnp.where(row_kvh == kvh, o_ext[:, kvh * HEAD_DIM:(kvh + 1) * HEAD_DIM], 0.0)
    return out


def _cmp_attn_sample_kernel(part_ref, pe_ref, w2bd_ref, q_ref, g8_ref, ov_ref, o_ref, imps_ref, *, q_pos):
    part = part_ref[0]
    n = part.shape[0]
    kc = _split2(_dot3(_split2(_cmp_hidden(part, pe_ref, 0)), _split2(w2bd_ref[0])))
    vc = _bdot(_cmp_hidden(part, pe_ref, 1), w2bd_ref[1])
    qh, ql = _split2(q_ref[0])
    s = (_dot_nt(qh, kc[0]) + (_dot_nt(qh, kc[1]) + _dot_nt(ql, kc[0]))) * SCALE
    cidx = lax.broadcasted_iota(jnp.int32, (1, n), 1)
    cmask = (cidx * D_CMP + (L_CMP - 1)) <= q_pos
    s = jnp.where(cmask, s, NEG)
    m = jnp.max(s, axis=-1, keepdims=True)
    ex = jnp.where(cmask, jnp.exp(s - m), 0.0)
    p = ex / jnp.maximum(jnp.sum(ex, axis=-1, keepdims=True), 1e-30)
    o_ref[0] = _fold_heads(_bdot(p, vc))
    ih, il = _split2(_hdot(g8_ref[...], p))
    ov = ov_ref[...]
    imps_ref[0] = (jnp.dot(ih, ov, preferred_element_type=jnp.float32)
                   + jnp.dot(il, ov, preferred_element_type=jnp.float32))


def cmp_attn_sample(part, pe, w2bd, q16ext, g8, ov_bf16, *, q_pos):
    b, n, _ = part.shape
    nbp = ov_bf16.shape[1]
    c3 = lambda bi: (0, 0, 0)
    return pl.pallas_call(
        functools.partial(_cmp_attn_sample_kernel, q_pos=q_pos),
        grid=(b,),
        in_specs=[pl.BlockSpec((1, n, 1024), lambda bi: (bi, 0, 0)),
                  pl.BlockSpec((2, 8, 256), c3),
                  pl.BlockSpec((2, 256, 256), c3),
                  pl.BlockSpec((1, N_HEADS, 256), lambda bi: (bi, 0, 0)),
                  pl.BlockSpec((8, N_HEADS), lambda bi: (0, 0)),
                  pl.BlockSpec((n, nbp), lambda bi: (0, 0))],
        out_specs=[pl.BlockSpec((1, N_HEADS, HEAD_DIM), lambda bi: (bi, 0, 0)),
                   pl.BlockSpec((1, 8, nbp), lambda bi: (bi, 0, 0))],
        out_shape=[jax.ShapeDtypeStruct((b, N_HEADS, HEAD_DIM), jnp.float32),
                   jax.ShapeDtypeStruct((b, 8, nbp), jnp.float32)],
        compiler_params=_cp(("parallel",)),
        name="cmp_attn_sample",
    )(part, pe, w2bd, q16ext, g8, ov_bf16)


def _topk_kernel(sc_ref, idx_ref, *, n_blk, q_blk, k_sel):
    imps = sc_ref[...]
    nbp = imps.shape[0]
    sidx = lax.broadcasted_iota(jnp.int32, (nbp, 1), 0)
    sf = sidx.astype(jnp.float32)
    forced = (sidx == 0) | (sidx == q_blk) | (sidx == q_blk - 1)
    causal = (sidx <= q_blk) & (sidx < n_blk)
    score = jnp.where(forced, -NEG, jnp.where(causal, imps, NEG))
    rows = []
    for _ in range(k_sel):
        m = jnp.max(score, axis=0, keepdims=True)
        pick = jnp.min(jnp.where(score == m, sf, float(nbp)), axis=0, keepdims=True)
        rows.append(jnp.where(m > 0.5 * NEG, pick, -1.0))
        score = jnp.where(sf == pick, 2.0 * NEG, score)
    idx_ref[...] = jnp.concatenate(rows, axis=0).astype(jnp.int32)


def topk_blocks(scores, *, n_blk, q_blk, k_sel):
    nbp, lanes = scores.shape
    return pl.pallas_call(
        functools.partial(_topk_kernel, n_blk=n_blk, q_blk=q_blk, k_sel=k_sel),
        out_shape=jax.ShapeDtypeStruct((k_sel, lanes), jnp.int32),
        name="topk_blocks",
    )(scores)


def _slc_sample_kernel(idx_ref, page_ref, *refs, n_sel, new_blk):
    blk_refs = refs[:N_KV * n_sel]
    q_ref, knew_ref, vnew_ref, o_ref = refs[N_KV * n_sel:]
    b = pl.program_id(0)
    page = blk_refs[0].shape[-1]
    half_of_lane = lax.broadcasted_iota(jnp.int32, (1, page), 1) // L_SLC
    for kvh in range(N_KV):
        kts, vts, biases = [], [], []
        has_new = False
        for n in range(n_sel):
            blk = blk_refs[kvh * n_sel + n]
            kts.append(blk[0, 0, 0])
            vts.append(blk[0, 1, 0])
            s_n = idx_ref[(b * N_KV + kvh) * n_sel + n]
            live = (half_of_lane == s_n % (page // L_SLC)) & ((s_n >= 0) & (s_n != new_blk))
            biases.append(jnp.where(live, 0.0, NEG))
            has_new = jnp.logical_or(has_new, s_n == new_blk)
        kt = jnp.concatenate(kts, axis=1).astype(jnp.bfloat16)
        vt = jnp.concatenate(vts, axis=1).astype(jnp.bfloat16)
        q = q_ref[0, kvh]
        s = (jnp.dot(q.astype(jnp.bfloat16), kt, preferred_element_type=jnp.float32) * SCALE
             + jnp.concatenate(biases, axis=1))
        s_new = jnp.where(has_new, jnp.sum(q * knew_ref[0, kvh], axis=-1, keepdims=True) * SCALE, NEG)
        m = jnp.maximum(jnp.max(s, axis=-1, keepdims=True), s_new)
        ex = jnp.exp(s - m)
        ex_new = jnp.where(has_new, jnp.exp(s_new - m), 0.0)
        denom = jnp.maximum(jnp.sum(ex, axis=-1, keepdims=True) + ex_new, 1e-30)
        o_ref[0, kvh] = (_dot_nt(ex.astype(jnp.bfloat16), vt) + ex_new * vnew_ref[0, kvh]) / denom


def slc_sample(idx_flat, page_table, pool_t, q4, knew, vnew, *, n_sel, new_blk):
    b, n_pages = page_table.shape
    page = pool_t.shape[-1]
    bpp = page // L_SLC
    pick_page = jnp.clip(idx_flat.reshape(b, N_KV * n_sel), 0, bpp * n_pages - 1) // bpp
    page_flat = jnp.take_along_axis(page_table, pick_page, axis=1).reshape(-1)

    def bspec(kvh, n):
        return pl.BlockSpec((1, 2, 1, HEAD_DIM, page),
                            lambda bi, idx, pg: (pg[(bi * N_KV + kvh) * n_sel + n], 0, kvh, 0, 0))

    small = lambda r: pl.BlockSpec((1, N_KV, r, HEAD_DIM), lambda bi, idx, pt: (bi, 0, 0, 0))
    grid_spec = pltpu.PrefetchScalarGridSpec(
        num_scalar_prefetch=2,
        grid=(b,),
        in_specs=[bspec(kvh, n) for kvh in range(N_KV) for n in range(n_sel)] + [small(8), small(1), small(1)],
        out_specs=small(8),
    )
    return pl.pallas_call(
        functools.partial(_slc_sample_kernel, n_sel=n_sel, new_blk=new_blk),
        grid_spec=grid_spec,
        out_shape=jax.ShapeDtypeStruct((b, N_KV, 8, HEAD_DIM), jnp.float32),
        compiler_params=_cp(("arbitrary",)),
        name="slc_sample",
    )(idx_flat, page_flat, *([pool_t] * (N_KV * n_sel)), q4, knew, vnew)


def _win_sample_kernel(w_ref, q_ref, o_ref):
    w = w_ref[0]
    s = _dot_nt(q_ref[0].astype(jnp.bfloat16), w[:, :KV_W].astype(jnp.bfloat16)) * SCALE
    m = jnp.max(s, axis=-1, keepdims=True)
    ex = jnp.exp(s - m)
    p = ex / jnp.maximum(jnp.sum(ex, axis=-1, keepdims=True), 1e-30)
    o_ref[0] = _fold_heads(_bdot(p, w[:, KV_W:]))


def win_sample(win_rows, q16ext):
    b, nk, _ = win_rows.shape
    return pl.pallas_call(
        _win_sample_kernel,
        grid=(b,),
        in_specs=[pl.BlockSpec((1, nk, 2 * KV_W), lambda bi: (bi, 0, 0)),
                  pl.BlockSpec((1, N_HEADS, KV_W), lambda bi: (bi, 0, 0))],
        out_specs=pl.BlockSpec((1, N_HEADS, HEAD_DIM), lambda bi: (bi, 0, 0)),
        out_shape=jax.ShapeDtypeStruct((b, N_HEADS, HEAD_DIM), jnp.float32),
        compiler_params=_cp(("parallel",)),
        name="win_sample",
    )(win_rows, q16ext)


def _prep_weights(w_in, mu, w2, a2, g2):
    o3 = 3 * D_RWKV
    nsa0 = R_COLS
    cols = [w_in[:, nsa0:nsa0 + D_NSA],
            w_in[:, 0:o3],
            w_in[:, nsa0 + D_NSA:nsa0 + D_NSA + 6 * KV_W],
            w_in[:, o3:R_COLS],
            w_in[:, nsa0 + D_NSA + 6 * KV_W:],
            jnp.zeros((D_MODEL, D_IN_PAD - C_GATE - 3 * N_HEADS), w_in.dtype)]
    w_in_p = jnp.concatenate(cols, axis=1).astype(jnp.bfloat16)
    z = jnp.zeros((LORA_ALL, D_RWKV), jnp.float32)
    return dict(w_in_p=w_in_p, mu_p=mu,
                w2p=z.at[0:LORA_W].set(w2), a2p=z.at[LORA_W:LORA_W + LORA_A].set(a2),
                g2p=z.at[LORA_W + LORA_A:].set(g2))


def _layer_weights(layer, g_mix, w_in, mu, w0, w2, a0, a2, g2, k_k, k_a, r_k, lnx_w, lnx_b, phi_pe, phi_w1,
                   phi_w2, w_o, g_ffn, w_gate, w_up, w_down):
    w = _prep_weights(w_in[layer], mu[layer], w2[layer], a2[layer], g2[layer])
    w1t, pe8, w2k, w2vt, w2bd = _phi_weights(phi_pe[layer], phi_w1[layer], phi_w2[layer])
    w1x = _expand_w1(phi_w1[layer])
    w.update(g_mix=g_mix[layer], w0=w0[layer], a0=a0[layer], k_k=k_k[layer], k_a=k_a[layer], r_k=r_k[layer],
             lnx_w=lnx_w[layer], lnx_b=lnx_b[layer], w1x=w1x, w1x_bf16=w1x.astype(jnp.bfloat16),
             pe=pe_term(pe8, w1t), w2k=w2k, w2vt=w2vt, w2bd=w2bd,
             wo=w_o[layer].astype(jnp.bfloat16), g_ffn=g_ffn[layer],
             wg=w_gate[layer].astype(jnp.bfloat16), wu=w_up[layer].astype(jnp.bfloat16),
             wd=w_down[layer].astype(jnp.bfloat16))
    return w


def _pick(n, pref):
    while n % pref:
        pref //= 2
    return pref


def _prompt_layer(x_prompt, W, g_final):
    B, T, _ = x_prompt.shape
    M = B * T
    x2 = x_prompt.reshape(M, D_MODEL)
    proj = proj_matmul(x2, W["g_mix"], W["w_in_p"], apply_norm=True, tm=_pick(M, 1024), tn=1024)
    proj3 = proj.reshape(B, T, D_IN_PAD)
    ro, rg, wkv = rwkv_mix(proj, jnp.zeros((B, 1, R_COLS), jnp.float32), W,
                           jnp.zeros((B, H_R, 64, 64), jnp.float32), b=B, t=T, tm=_pick(T, 256), tc=_pick(T, 64))
    cmp_kv = proj3[..., C_CMP:C_CMP + 2 * KV_W]
    slc_kv = proj3[..., C_SLC:C_SLC + 2 * KV_W]
    win_kv = proj3[..., C_WIN:C_WIN + 2 * KV_W]
    nch = T // D_CMP
    part = cmp_part_prompt(cmp_kv.reshape(B, nch, D_CMP * 2 * KV_W), W["w1x"])
    ovt = _overlap_t(128, nch - 1, nch).astype(jnp.bfloat16)
    o_cmp, sel = cmp_select_prompt(part, W["pe"], W["w2k"], W["w2vt"], proj3, ovt, tq=128)
    ks, vs = _kv_heads(slc_kv, B, T)
    kw, vw = _kv_blocks(win_kv, B, T)
    nsa = slc_win_prompt(proj3, ks, vs, kw, vw, sel, o_cmp, tq=128)
    r2 = lambda z: z.reshape(M, 1024)
    x1 = out_proj(r2(ro), r2(rg), r2(nsa), x2, W["wo"], tm=_pick(M, 512), tn=1024)
    y = ffn_final(x1, W["g_ffn"], W["wg"], W["wu"], W["wd"], g_final, tm=_pick(M, 512), tf=512)
    kv5 = lambda z: z.reshape(B, T, 2, N_KV, HEAD_DIM)
    nwin = min(WINDOW, T)
    return y.reshape(B, T, D_MODEL), kv5(cmp_kv), kv5(slc_kv), kv5(win_kv)[:, T - nwin:], wkv


def _sample_layer(x_sample, xn_s, cache_cmp, cache_slc, cache_win, wkv0, shift, page_table, W, g_final):
    b = x_sample.shape[0]
    n_pool, page = cache_cmp.shape[0], cache_cmp.shape[1]
    n_pages = page_table.shape[1]
    past_len = n_pages * page
    n_rows = -(-(past_len + 1) // L_SLC) * L_SLC
    n_blk = n_rows // L_SLC
    q_blk = past_len // L_SLC
    nbp = -(-n_blk // 128) * 128

    rows = jnp.concatenate([xn_s, shift], axis=0)
    proj = proj_matmul(rows, W["g_mix"], W["w_in_p"], apply_norm=False, tm=2 * b, tn=1024)
    prev = jnp.concatenate([proj[b:, C_R:C_R + 3 * D_RWKV], proj[b:, C_LORA:C_LORA + LORA_ALL]], axis=1)
    ro, rg, wkv = rwkv_mix(proj[:b], prev, W, wkv0, b=b, t=1, tm=1, tc=1)

    q = proj[:b, C_Q:C_Q + D_NSA].reshape(b, N_KV, GQA, 1, HEAD_DIM)
    eye = jnp.eye(N_KV, dtype=jnp.float32).reshape(1, N_KV, 1, N_KV, 1)
    qext = (q * eye).reshape(b, N_KV, GQA, KV_W)
    q16ext = qext.reshape(b, N_HEADS, KV_W)
    cmp_new = proj[:b, C_CMP:C_CMP + 2 * KV_W]
    slc_new = proj[:b, C_SLC:C_SLC + 2 * KV_W]
    win_new = proj[:b, C_WIN:C_WIN + 2 * KV_W]

    part = cmp_part_sample(cache_cmp.transpose(0, 2, 3, 4, 1), page_table, W["w1x_bf16"], pp=_pick(n_pages, 32))
    nc_rows = part.shape[1]
    ov = _overlap_t(nbp, n_rows // D_CMP - L_CMP // D_CMP + 1, nc_rows)[:n_blk].T
    ov = jnp.pad(ov, ((0, 0), (0, nbp - n_blk)))
    g8 = jnp.asarray((np.arange(8)[:, None] == np.arange(N_HEADS)[None, :] // GQA).astype(np.float32))
    o_cmp, imps = cmp_attn_sample(part, W["pe"], W["w2bd"], q16ext, g8, ov.astype(jnp.bfloat16), q_pos=past_len)
    scores = imps[:, :N_KV].transpose(2, 0, 1).reshape(nbp, b * N_KV)
    k_sel = min(N_SELECT, n_blk)
    idx = topk_blocks(scores, n_blk=n_blk, q_blk=q_blk, k_sel=k_sel)
    q4 = jnp.pad(q.reshape(b, N_KV, GQA, HEAD_DIM), ((0, 0), (0, 0), (0, 8 - GQA), (0, 0)))
    slc_new5 = slc_new.reshape(b, 2, N_KV, 1, HEAD_DIM)
    o_slc = slc_sample(idx.T.reshape(-1), page_table, cache_slc.transpose(0, 2, 3, 4, 1), q4,
                       slc_new5[:, 0], slc_new5[:, 1], n_sel=k_sel, new_blk=q_blk)
    o_slc = o_slc[:, :, :GQA].reshape(b, D_NSA)

    n_buf = cache_win.shape[1]
    win_all = jnp.concatenate([cache_win, win_new.reshape(b, 1, 2, N_KV, HEAD_DIM)], axis=1)
    win_keep = win_all[:, n_buf + 1 - min(WINDOW, n_buf + 1):]
    o_win = win_sample(win_keep.reshape(b, -1, 2 * KV_W), q16ext)

    x2 = x_sample.reshape(b, D_MODEL)
    gate_logits = proj[:b, C_GATE:C_GATE + 3 * N_HEADS].reshape(b, 3, N_HEADS, 1)
    nsa = nsa_combine_sample(o_cmp, o_slc.reshape(b, N_HEADS, HEAD_DIM), o_win, gate_logits)
    x1 = out_proj(ro.reshape(b, -1), rg.reshape(b, -1), nsa.reshape(b, D_NSA), x2, W["wo"], tm=b, tn=1024)
    y = ffn_final(x1, W["g_ffn"], W["wg"], W["wu"], W["wd"], g_final, tm=b, tf=512)
    kv5 = lambda z: z.reshape(b, 1, 2, N_KV, HEAD_DIM)
    return y.reshape(b, 1, D_MODEL), kv5(cmp_new), kv5(slc_new), win_keep, wkv


def kernel(x_prompt, x_sample, cache_cmp_kv, cache_slc_kv, cache_win_kv, state_wkv, state_shift, page_table, g_mix, w_in, mu, w0, w2, a0, a2, g2, k_k, k_a, r_k, lnx_w, lnx_b, phi_pe, phi_w1, phi_w2, w_o, g_ffn, w_gate, w_up, w_down, g_final):
    W = _layer_weights(0, g_mix, w_in, mu, w0, w2, a0, a2, g2, k_k, k_a, r_k, lnx_w, lnx_b, phi_pe, phi_w1,
                       phi_w2, w_o, g_ffn, w_gate, w_up, w_down)
    bp, bs = x_prompt.shape[0], x_sample.shape[0]
    small = jnp.concatenate([x_sample.reshape(bs, D_MODEL), x_prompt[:, -1]], axis=0)
    pad = (-small.shape[0]) % 8
    xn_small = rmsnorm_rows(jnp.pad(small, ((0, pad), (0, 0))), W["g_mix"])
    xn_s, shift_p = xn_small[:bs], xn_small[bs:bs + bp]

    y_p, cmp_p, slc_p, win_p, wkv_p = _prompt_layer(x_prompt, W, g_final)
    y_s, cmp_s, slc_s, win_s, wkv_s = _sample_layer(x_sample, xn_s, cache_cmp_kv[0], cache_slc_kv[0],
                                                    cache_win_kv[0], state_wkv[0], state_shift[0], page_table,
                                                    W, g_final)
    return (y_p, y_s, cmp_p[None], slc_p[None], win_p[None], wkv_p[None], shift_p[None],
            cmp_s[None], slc_s[None], win_s[None], wkv_s[None], xn_s[None])
```

```python
import functools

import jax
import jax.numpy as jnp
import numpy as np
from jax import lax
from jax.experimental import pallas as pl
from jax.experimental.pallas import tpu as pltpu

D_MODEL = 2048
D_RWKV = 1024
D_NSA = 1024
HEAD_DIM_R = 64
H_R = 16
LORA_W = 96
LORA_A = 96
LORA_G = 64
LORA_ALL = LORA_W + LORA_A + LORA_G
HEAD_DIM = 64
N_HEADS = 16
N_KV = 4
GQA = 4
KV_W = 256
L_CMP = 32
D_CMP = 16
L_SLC = 64
N_SELECT = 16
WINDOW = 512
PHI_HIDDEN = 64
R_COLS = 3 * D_RWKV + LORA_ALL
RMS_EPS = 1e-6
LNX_EPS = 64e-5
SCALE = HEAD_DIM ** -0.5
NEG = -1e30

C_Q, C_R, C_K, C_V = 0, 1024, 2048, 3072
C_CMP, C_SLC, C_WIN = 4096, 4608, 5120
C_LORA, C_GATE = 5632, 5888
D_IN_PAD = 6144

VMEM_LIMIT = 48 * 1024 * 1024
HI = lax.Precision.HIGHEST


def _cp(sem, limit=VMEM_LIMIT):
    return pltpu.CompilerParams(dimension_semantics=sem, vmem_limit_bytes=limit)


def _bdot(a, b):
    return jnp.dot(a.astype(jnp.bfloat16), b.astype(jnp.bfloat16), preferred_element_type=jnp.float32)


def _hdot(a, b):
    return jnp.dot(a, b, precision=HI, preferred_element_type=jnp.float32)


def _dot_nt(a, b, precision=None):
    return lax.dot_general(a, b, (((1,), (1,)), ((), ())), precision=precision,
                           preferred_element_type=jnp.float32)


def _sigmoid(x):
    return 1.0 / (1.0 + jnp.exp(-x))


def _silu(x):
    return x * _sigmoid(x)


def _rmsnorm_rows_kernel(x_ref, g_ref, o_ref):
    x = x_ref[...]
    ms = jnp.mean(x * x, axis=-1, keepdims=True)
    o_ref[...] = x * lax.rsqrt(ms + RMS_EPS) * g_ref[...]


def rmsnorm_rows(x, g):
    m, d = x.shape
    return pl.pallas_call(
        _rmsnorm_rows_kernel,
        out_shape=jax.ShapeDtypeStruct((m, d), jnp.float32),
        name="rmsnorm_rows",
    )(x, g.reshape(1, d))


def _proj_kernel(x_ref, g_ref, w_ref, o_ref, xn_ref, *, apply_norm):
    @pl.when(pl.program_id(1) == 0)
    def _():
        x = x_ref[...]
        if apply_norm:
            ms = jnp.mean(x * x, axis=-1, keepdims=True)
            x = x * lax.rsqrt(ms + RMS_EPS) * g_ref[...]
        xn_ref[...] = x.astype(jnp.bfloat16)

    o_ref[...] = jnp.dot(xn_ref[...], w_ref[...], preferred_element_type=jnp.float32)


def proj_matmul(x, g, w_bf16, *, apply_norm, tm, tn):
    m, d = x.shape
    n = w_bf16.shape[1]
    return pl.pallas_call(
        functools.partial(_proj_kernel, apply_norm=apply_norm),
        grid=(m // tm, n // tn),
        in_specs=[
            pl.BlockSpec((tm, d), lambda i, j: (i, 0)),
            pl.BlockSpec((1, d), lambda i, j: (0, 0)),
            pl.BlockSpec((d, tn), lambda i, j: (0, j)),
        ],
        out_specs=pl.BlockSpec((tm, tn), lambda i, j: (i, j)),
        out_shape=jax.ShapeDtypeStruct((m, n), jnp.float32),
        scratch_shapes=[pltpu.VMEM((tm, d), jnp.bfloat16)],
        compiler_params=_cp(("parallel", "arbitrary")),
        name="proj_matmul",
    )(x, g.reshape(1, d), w_bf16)


def _rwkv_prep_kernel(pr_ref, pk_ref, pv_ref, pl_ref, prev_ref, mu_ref, w0_ref, a0_ref, kk_ref, ka_ref,
                      w2_ref, a2_ref, g2_ref,
                      r_out, w_out, k_out, v_out, kk_out, a_out, g_out, carry_ref, *, row_prev):
    tm = pr_ref.shape[1]
    pr, pk, pv, plo = pr_ref[0], pk_ref[0], pv_ref[0], pl_ref[0]
    if row_prev:
        def shift_mix(p, lo, hi):
            return p + (prev_ref[0, :, lo:hi] - p) * mu_ref[:, lo:hi]
    else:
        @pl.when(pl.program_id(1) == 0)
        def _():
            carry_ref[...] = prev_ref[0]

        row0 = lax.broadcasted_iota(jnp.int32, (tm, 1), 0) == 0

        def shift_mix(p, lo, hi):
            prev = pltpu.roll(p, 1, axis=0)
            prev = jnp.where(row0, carry_ref[:, lo:hi], prev)
            return p + (prev - p) * mu_ref[:, lo:hi]

    xr = shift_mix(pr, 0, 1024)
    xk = shift_mix(pk, 1024, 2048)
    xv = shift_mix(pv, 2048, 3072)
    xl = shift_mix(plo, 3072, 3328)
    if not row_prev:
        last = tm - 1
        carry_ref[:, 0:1024] = pr[last:last + 1]
        carry_ref[:, 1024:2048] = pk[last:last + 1]
        carry_ref[:, 2048:3072] = pv[last:last + 1]
        carry_ref[:, 3072:3328] = plo[last:last + 1]

    wl = _dot3(_split2(jnp.tanh(xl)), _split2(w2_ref[...]))
    al = _dot3(_split2(xl), _split2(a2_ref[...]))
    gl = _dot3(_split2(_sigmoid(xl)), _split2(g2_ref[...]))
    z = -(w0_ref[...] + wl)
    softplus = jnp.maximum(z, 0.0) + jnp.log(1.0 + jnp.exp(-jnp.abs(z)))
    w_log = -softplus - 0.5
    decay = jnp.exp(-jnp.exp(w_log))
    a = _sigmoid(a0_ref[...] + al)
    r_out[0] = xr
    w_out[0] = decay
    k_out[0] = xk * (1.0 + (a - 1.0) * ka_ref[...])
    v_out[0] = xv
    kk_out[0] = xk * kk_ref[...]
    a_out[0] = a
    g_out[0] = gl


def rwkv_prep(proj, prev, mu_p, w0, a0, k_k, k_a, w2p, a2p, g2p, *, b, t, tm, row_prev=False):
    proj3 = proj.reshape(b, t, D_IN_PAD)
    nt = t // tm
    prev_spec = (pl.BlockSpec((1, tm, R_COLS), lambda bi, ti: (bi, ti, 0)) if row_prev
                 else pl.BlockSpec((1, 1, R_COLS), lambda bi, ti: (bi, 0, 0)))
    row = lambda blk, w: pl.BlockSpec((1, tm, w), lambda bi, ti: (bi, ti, blk))
    vec = lambda w: pl.BlockSpec((1, w), lambda bi, ti: (0, 0))
    mat = pl.BlockSpec((LORA_ALL, D_RWKV), lambda bi, ti: (0, 0))
    out = pl.BlockSpec((1, tm, D_RWKV), lambda bi, ti: (bi, ti, 0))
    shp = jax.ShapeDtypeStruct((b, t, D_RWKV), jnp.float32)
    return pl.pallas_call(
        functools.partial(_rwkv_prep_kernel, row_prev=row_prev),
        grid=(b, nt),
        in_specs=[row(C_R // 1024, 1024), row(C_K // 1024, 1024), row(C_V // 1024, 1024),
                  row(C_LORA // 256, 256),
                  prev_spec,
                  vec(R_COLS), vec(D_RWKV), vec(D_RWKV), vec(D_RWKV), vec(D_RWKV),
                  mat, mat, mat],
        out_specs=[out] * 7,
        out_shape=[shp] * 7,
        scratch_shapes=[pltpu.VMEM((1, R_COLS), jnp.float32)],
        compiler_params=_cp(("parallel", "arbitrary")),
        name="rwkv_prep",
    )(proj3, proj3, proj3, proj3, prev, mu_p.reshape(1, -1), w0.reshape(1, -1), a0.reshape(1, -1),
      k_k.reshape(1, -1), k_a.reshape(1, -1), w2p, a2p, g2p)


def _rwkv_scan_kernel(r_ref, w_ref, k_ref, kk_ref, a_ref, v_ref, s0_ref, rk_ref, lnw_ref, lnb_ref,
                      o_ref, sfin_ref, s_ref, kkn_ref, b_ref, y_ref, *, halved):
    tc = r_ref.shape[1]
    ni = v_ref.shape[2]

    @pl.when(pl.program_id(1) == 0)
    def _():
        s_ref[...] = s0_ref[0]

    kkraw = kk_ref[0]
    nrm = jnp.sqrt(jnp.sum(kkraw * kkraw, axis=1, keepdims=True))
    kkn = kkraw / jnp.maximum(nrm, 1e-12)
    kkn_ref[...] = kkn
    b_ref[...] = kkn * a_ref[0]

    def step(t, carry):
        w_t = w_ref[0, t]
        k_t = k_ref[0, t]
        r_t = r_ref[0, t]
        kk_t = kkn_ref[t]
        b_t = b_ref[t]
        for i in range(ni):
            s = s_ref[i]
            sa = -jnp.sum(s * kk_t, axis=0, keepdims=True)
            v_i = v_ref[0, t, pl.ds(i, 1), :]
            s = s * w_t + sa * b_t + v_i * k_t
            y_ref[t, pl.ds(i, 1), :] = jnp.sum(s * r_t, axis=0, keepdims=True)
            s_ref[i] = s
        return carry

    lax.fori_loop(0, tc, step, 0)

    y = y_ref[...]
    v = v_ref[0]

    def head_sum(z):
        s = jnp.sum(z, axis=1, keepdims=True)
        if halved:
            s = s + pltpu.roll(s, 64, axis=2)
        return s

    mean = head_sum(y) * (1.0 / HEAD_DIM_R)
    d = y - mean
    var = head_sum(d * d) * (1.0 / HEAD_DIM_R)
    yn = d * lax.rsqrt(var + LNX_EPS) * lnw_ref[0] + lnb_ref[0]
    bonus = jnp.sum(r_ref[0] * k_ref[0] * rk_ref[0], axis=1, keepdims=True)
    o_ref[0] = yn + bonus * v

    @pl.when(pl.program_id(1) == pl.num_programs(1) - 1)
    def _():
        sfin_ref[0] = s_ref[...]


def rwkv_scan(r, w, k, kk, a, v, s0, rk, lnw, lnb, *, tc, halved):
    g, t = r.shape[0], r.shape[1]
    ni = v.shape[2]
    col = pl.BlockSpec((1, tc, 64, 128), lambda gi, ti: (gi, ti, 0, 0))
    vspec = pl.BlockSpec((1, tc, ni, 128), lambda gi, ti: (gi, ti, 0, 0))
    sspec = pl.BlockSpec((1, ni, 64, 128), lambda gi, ti: (gi, 0, 0, 0))
    return pl.pallas_call(
        functools.partial(_rwkv_scan_kernel, halved=halved),
        grid=(g, t // tc),
        in_specs=[col, col, col, col, col, vspec, sspec,
                  pl.BlockSpec((1, 64, 128), lambda gi, ti: (gi, 0, 0)),
                  pl.BlockSpec((1, ni, 128), lambda gi, ti: (gi, 0, 0)),
                  pl.BlockSpec((1, ni, 128), lambda gi, ti: (gi, 0, 0))],
        out_specs=[vspec, sspec],
        out_shape=[jax.ShapeDtypeStruct((g, t, ni, 128), jnp.float32),
                   jax.ShapeDtypeStruct((g, ni, 64, 128), jnp.float32)],
        scratch_shapes=[pltpu.VMEM((ni, 64, 128), jnp.float32),
                        pltpu.VMEM((tc, 64, 128), jnp.float32),
                        pltpu.VMEM((tc, 64, 128), jnp.float32),
                        pltpu.VMEM((tc, ni, 128), jnp.float32)],
        compiler_params=_cp(("parallel", "arbitrary")),
        name="rwkv_scan",
    )(r, w, k, kk, a, v, s0, rk, lnw, lnb)


def _to_scan_layout(x, b, t, halved):
    z = x.reshape(b, t, H_R, 64).transpose(1, 3, 0, 2).reshape(t, 64, b * H_R)
    if halved:
        return jnp.concatenate([z, z], axis=-1)[None]
    g = (b * H_R) // 128
    return z.reshape(t, 64, g, 128).transpose(2, 0, 1, 3)


def _v_to_scan_layout(x, b, t, halved):
    if halved:
        return x.reshape(b, t, H_R, 2, 32).transpose(1, 4, 3, 0, 2).reshape(1, t, 32, 128)
    g = (b * H_R) // 128
    z = x.reshape(b, t, H_R, 64).transpose(1, 3, 0, 2).reshape(t, 64, g, 128)
    return z.transpose(2, 0, 1, 3)


def _vecparam_scan_layout(p, b, halved):
    z = jnp.broadcast_to(p.reshape(1, H_R, 64), (b, H_R, 64))
    if halved:
        return z.reshape(b, H_R, 2, 32).transpose(3, 2, 0, 1).reshape(1, 32, 128)
    g = (b * H_R) // 128
    return z.transpose(2, 0, 1).reshape(64, g, 128).transpose(1, 0, 2)


def _keyparam_scan_layout(p, b, halved):
    z = jnp.broadcast_to(p.reshape(1, H_R, 64), (b, H_R, 64)).transpose(2, 0, 1).reshape(64, b * H_R)
    if halved:
        return jnp.concatenate([z, z], axis=-1)[None]
    g = (b * H_R) // 128
    return z.reshape(64, g, 128).transpose(1, 0, 2)


def _from_scan_layout(o, b, t, halved):
    if halved:
        return o.reshape(t, 32, 2, b, H_R).transpose(3, 0, 4, 2, 1).reshape(b, t, D_RWKV)
    g = o.shape[0]
    return o.transpose(1, 2, 0, 3).reshape(t, 64, b, H_R).transpose(2, 0, 3, 1).reshape(b, t, D_RWKV)


def _state_to_scan_layout(s, b, halved):
    if halved:
        return s.reshape(b, H_R, 2, 32, 64).transpose(3, 4, 2, 0, 1).reshape(1, 32, 64, 128)
    g = (b * H_R) // 128
    return s.reshape(g, 128, 64, 64).transpose(0, 2, 3, 1)


def _state_from_scan_layout(s, b, halved):
    if halved:
        return s.reshape(32, 64, 2, b, H_R).transpose(3, 4, 2, 0, 1).reshape(b, H_R, 64, 64)
    return s.transpose(0, 3, 1, 2).reshape(b, H_R, 64, 64)


def rwkv_mix(proj, prev, wts, s0, *, b, t, tm, tc):
    halved = (b * H_R == 64)
    if t == 1:
        outs = rwkv_prep(proj, prev.reshape(1, b, R_COLS), wts["mu_p"], wts["w0"], wts["a0"], wts["k_k"],
                         wts["k_a"], wts["w2p"], wts["a2p"], wts["g2p"], b=1, t=b, tm=b, row_prev=True)
        r, w, k, v, kk, a, g = [z.reshape(b, 1, D_RWKV) for z in outs]
    else:
        r, w, k, v, kk, a, g = rwkv_prep(proj, prev, wts["mu_p"], wts["w0"], wts["a0"], wts["k_k"], wts["k_a"],
                                         wts["w2p"], wts["a2p"], wts["g2p"], b=b, t=t, tm=tm)
    tr = lambda z: _to_scan_layout(z, b, t, halved)
    o, sfin = rwkv_scan(tr(r), tr(w), tr(k), tr(kk), tr(a), _v_to_scan_layout(v, b, t, halved),
                        _state_to_scan_layout(s0, b, halved),
                        _keyparam_scan_layout(wts["r_k"], b, halved),
                        _vecparam_scan_layout(wts["lnx_w"], b, halved),
                        _vecparam_scan_layout(wts["lnx_b"], b, halved),
                        tc=tc, halved=halved)
    return _from_scan_layout(o, b, t, halved), g, _state_from_scan_layout(sfin, b, halved)


def _cmp_part_kernel(*refs, n_in, precise, prefetch):
    refs = refs[prefetch:]
    x_refs, w_ref, o_ref = refs[:n_in], refs[n_in], refs[n_in + 1]
    for e in range(2):
        for hp in range(2):
            acc = None
            for j in range(D_CMP):
                lo = j * 512 + e * 256 + hp * 128
                xs = [xr[0, :, lo:lo + 128] for xr in x_refs]
                x = xs[0] if n_in == 1 else jnp.concatenate(xs, axis=0)
                d = _hdot(x, w_ref[e, j]) if precise else _bdot(x, w_ref[e, j])
                acc = d if acc is None else acc + d
            c0 = e * 256 + hp * 128
            o_ref[0, :, c0:c0 + 128] = acc[:, :128]
            o_ref[0, :, 512 + c0:512 + c0 + 128] = acc[:, 128:]


def cmp_part_prompt(rows, w1x):
    b, nch, _ = rows.shape
    return pl.pallas_call(
        functools.partial(_cmp_part_kernel, n_in=1, precise=True, prefetch=0),
        grid=(b,),
        in_specs=[pl.BlockSpec((1, nch, 8192), lambda bi: (bi, 0, 0)),
                  pl.BlockSpec((2, D_CMP, 128, 256), lambda bi: (0, 0, 0, 0))],
        out_specs=pl.BlockSpec((1, nch, 1024), lambda bi: (bi, 0, 0)),
        out_shape=jax.ShapeDtypeStruct((b, nch, 1024), jnp.float32),
        compiler_params=_cp(("parallel",)),
        name="cmp_part_prompt",
    )(rows, w1x)


def _expand_w1(phi_w1):
    w1 = phi_w1.reshape(2, 2, D_CMP, HEAD_DIM, PHI_HIDDEN)
    eye = jnp.eye(2, dtype=phi_w1.dtype)
    w = jnp.einsum("ab,esjdf->ejadsbf", eye, w1)
    return w.reshape(2, D_CMP, 128, 256)


def _split2(x):
    hi = x.astype(jnp.bfloat16)
    return hi, (x - hi.astype(jnp.float32)).astype(jnp.bfloat16)


def _dot3(a, b):
    (ah, al), (bh, bl) = a, b
    d = lambda x, y: jnp.dot(x, y, preferred_element_type=jnp.float32)
    return d(ah, bh) + (d(ah, bl) + d(al, bh))


def _pe_term_kernel(pe_ref, w1t_ref, o_ref):
    for e in range(2):
        o_ref[e] = _hdot(pe_ref[e], w1t_ref[e])


def pe_term(pe8, w1t):
    return pl.pallas_call(
        _pe_term_kernel,
        out_shape=jax.ShapeDtypeStruct((2, 8, N_KV * PHI_HIDDEN), jnp.float32),
        name="pe_term",
    )(pe8, w1t)


def _cmp_hidden(part, pe_ref, e):
    n = part.shape[0]
    lo = e * 256
    h = part[:, lo:lo + 256] + pltpu.roll(part[:, 512 + lo:512 + lo + 256], n - 1, axis=0)
    return _silu(h + pe_ref[e, 0:1])


def _cmp_sel_kernel(part_ref, pe_ref, w2k_ref, w2vt_ref, q_ref, ovt_ref, gt_ref, o_ref, sel_ref,
                    kch_ref, kcl_ref, vct_ref, *, tq, n_blk):
    ti = pl.program_id(1)

    @pl.when(ti == 0)
    def _():
        part = part_ref[0]
        hk = _split2(_cmp_hidden(part, pe_ref, 0))
        hv = _cmp_hidden(part, pe_ref, 1)
        for kvh in range(N_KV):
            kch_ref[kvh], kcl_ref[kvh] = _split2(_dot3(hk, _split2(w2k_ref[kvh])))
            vct_ref[kvh] = _dot_nt(w2vt_ref[kvh], hv, HI).astype(jnp.bfloat16)

    nc = kch_ref.shape[1]
    nbp = -(-n_blk // 8) * 8
    k_sel = min(N_SELECT, n_blk)
    qpos = ti * tq + lax.broadcasted_iota(jnp.int32, (1, tq), 1)
    cidx = lax.broadcasted_iota(jnp.int32, (nc, 1), 0)
    cmask = (cidx * D_CMP + (L_CMP - 1)) <= qpos
    sidx = cidx[:nbp]
    qblk = qpos // L_SLC
    forced = (sidx == 0) | (sidx == qblk) | (sidx == qblk - 1)
    causal = sidx <= qblk
    pad = jnp.zeros((128 - nbp, tq), jnp.float32)
    need_rank = (ti + 1) * tq > k_sel * L_SLC

    cmask4 = jnp.concatenate([cmask.astype(jnp.float32)] * GQA, axis=1) > 0.5
    gate_t = _sigmoid(gt_ref[0][:, 0:128]).T
    imp_t = []
    for kvh in range(N_KV):
        qt = (q_ref[0, :, kvh * 256:(kvh + 1) * 256] * SCALE).T
        qt4 = _split2(jnp.concatenate([qt[g * 64:(g + 1) * 64] for g in range(GQA)], axis=1))
        st = _dot3((kch_ref[kvh], kcl_ref[kvh]), qt4)
        st = jnp.where(cmask4, st, NEG)
        m = jnp.max(st, axis=0, keepdims=True)
        ex = jnp.where(cmask4, jnp.exp(st - m), 0.0)
        pt = ex / jnp.maximum(jnp.sum(ex, axis=0, keepdims=True), 1e-30)
        ot = jnp.dot(vct_ref[kvh], pt.astype(jnp.bfloat16), preferred_element_type=jnp.float32)
        gate4 = jnp.concatenate([gate_t[kvh * GQA + g:kvh * GQA + g + 1] for g in range(GQA)], axis=1)
        o_ref[0, :, kvh * 256:(kvh + 1) * 256] = jnp.concatenate(
            [(ot * gate4)[:, g * tq:(g + 1) * tq] for g in range(GQA)], axis=0).T
        impt = pt[:, 0:tq]
        for g in range(1, GQA):
            impt = impt + pt[:, g * tq:(g + 1) * tq]
        imp_t.append(impt)

    @pl.when(need_rank)
    def _():
        ih, il = _split2(jnp.concatenate(imp_t, axis=1))
        ov = ovt_ref[...]
        imps = (jnp.dot(ov, ih, preferred_element_type=jnp.float32)
                + jnp.dot(ov, il, preferred_element_type=jnp.float32))[:nbp]
        tile4 = lambda z: jnp.concatenate([z.astype(jnp.float32)] * N_KV, axis=1) > 0.5
        forced4, causal4 = tile4(forced), tile4(causal)
        score = jnp.where(forced4, -NEG, jnp.where(causal4, imps, NEG))
        rank = jnp.zeros((nbp, N_KV * tq), jnp.float32)
        for s2 in range(n_blk):
            row = score[s2:s2 + 1]
            beats = (row > score) | ((row == score) & (s2 < sidx))
            rank = rank + beats.astype(jnp.float32)
        seln = ((rank < k_sel) & causal4).astype(jnp.float32)
        for kvh in range(N_KV):
            sel_ref[0, kvh] = jnp.concatenate([seln[:, kvh * tq:(kvh + 1) * tq], pad], axis=0)

    @pl.when(jnp.logical_not(need_rank))
    def _():
        keep = jnp.concatenate([causal.astype(jnp.float32), pad], axis=0)
        for kvh in range(N_KV):
            sel_ref[0, kvh] = keep


def cmp_select_prompt(part, pe, w2k, w2vt, proj3, ovt_bf16, *, tq):
    b, nc, _ = part.shape
    t = proj3.shape[1]
    c3 = lambda bi, ti: (0, 0, 0)
    return pl.pallas_call(
        functools.partial(_cmp_sel_kernel, tq=tq, n_blk=t // L_SLC),
        grid=(b, t // tq),
        in_specs=[pl.BlockSpec((1, nc, 1024), lambda bi, ti: (bi, 0, 0)),
                  pl.BlockSpec((2, 8, 256), c3),
                  pl.BlockSpec((N_KV, 256, 64), c3),
                  pl.BlockSpec((N_KV, 64, 256), c3),
                  pl.BlockSpec((1, tq, 1024), lambda bi, ti: (bi, ti, C_Q // 1024)),
                  pl.BlockSpec((128, nc), lambda bi, ti: (0, 0)),
                  pl.BlockSpec((1, tq, 256), lambda bi, ti: (bi, ti, C_GATE // 256))],
        out_specs=[pl.BlockSpec((1, tq, 1024), lambda bi, ti: (bi, ti, 0)),
                   pl.BlockSpec((1, N_KV, 128, tq), lambda bi, ti: (bi, 0, 0, ti))],
        out_shape=[jax.ShapeDtypeStruct((b, t, 1024), jnp.float32),
                   jax.ShapeDtypeStruct((b, N_KV, 128, t), jnp.float32)],
        scratch_shapes=[pltpu.VMEM((N_KV, nc, 64), jnp.bfloat16),
                        pltpu.VMEM((N_KV, nc, 64), jnp.bfloat16),
                        pltpu.VMEM((N_KV, 64, nc), jnp.bfloat16)],
        compiler_params=_cp(("parallel", "arbitrary")),
        name="cmp_select_prompt",
    )(part, pe, w2k, w2vt, proj3, ovt_bf16, proj3)


def _overlap_t(n_blocks_pad, nc, nc_pad):
    c_start = np.arange(nc_pad) * D_CMP
    s_start = np.arange(n_blocks_pad) * L_SLC
    ov = (c_start[None, :] < s_start[:, None] + L_SLC) & (c_start[None, :] + L_CMP > s_start[:, None])
    ov &= (np.arange(nc_pad) < nc)[None, :]
    return jnp.asarray(ov.astype(np.float32))


def _phi_weights(phi_pe, phi_w1, phi_w2):
    w1t = jnp.tile(phi_w1, (1, 1, N_KV))
    pe8 = jnp.broadcast_to(phi_pe.reshape(2, 1, L_CMP * HEAD_DIM), (2, 8, L_CMP * HEAD_DIM))
    z = jnp.zeros((N_KV, N_KV, PHI_HIDDEN, HEAD_DIM), jnp.float32)
    idx = jnp.arange(N_KV)
    w2k = z.at[idx, idx].set(phi_w2[0]).reshape(N_KV, 256, 64)
    w2vt = jnp.transpose(z.at[idx, idx].set(phi_w2[1]), (0, 3, 1, 2)).reshape(N_KV, 64, 256)
    eye = jnp.eye(N_KV, dtype=jnp.float32)
    w2bd = jnp.einsum("ab,efd->eafbd", eye, phi_w2).reshape(2, 256, 256)
    return w1t, pe8, w2k, w2vt, w2bd


def _gate_row(gt_ref, branch, kvh, tq):
    gt = _sigmoid(gt_ref[0][:, 0:128]).T
    col = lax.broadcasted_iota(jnp.int32, (128, 1), 0)
    rows = [jnp.sum(jnp.where(col == branch * N_HEADS + kvh * GQA + g, gt, 0.0), axis=0, keepdims=True)
            for g in range(GQA)]
    return jnp.concatenate(rows, axis=1)


def _heads_to_rows(o, tq):
    return jnp.concatenate([o[:, g * tq:(g + 1) * tq] for g in range(GQA)], axis=0).T


def _attend_t(k, vt, qt4, bias, tq):
    s = jnp.dot(k, qt4, preferred_element_type=jnp.float32) + jnp.concatenate([bias] * GQA, axis=1)
    s = s.astype(jnp.bfloat16)
    p = jnp.exp(s - jnp.max(s, axis=0, keepdims=True))
    vt1 = jnp.concatenate([vt, jnp.ones((16, vt.shape[1]), jnp.bfloat16)], axis=0)
    ol = jnp.dot(vt1, p, preferred_element_type=jnp.float32)
    return ol[:HEAD_DIM] / jnp.maximum(ol[HEAD_DIM:HEAD_DIM + 1], 1e-30)


SLC_KEY_STEP = 256
WIN_BLOCKS = WINDOW // 128 + 1


def _slc_win_kernel(q_ref, ks_ref, vs_ref, kw_ref, vw_ref, sel_ref, ex_ref, gt_ref, oc_ref, o_ref, *, tq):
    kvh = pl.program_id(1)
    ti = pl.program_id(2)
    t = ks_ref.shape[2]
    qt = (q_ref[0] * SCALE).T.astype(jnp.bfloat16)
    qt4 = jnp.concatenate([qt[g * HEAD_DIM:(g + 1) * HEAD_DIM] for g in range(GQA)], axis=1)
    qpos = ti * tq + lax.broadcasted_iota(jnp.int32, (1, tq), 1)
    selt = sel_ref[0, 0].astype(jnp.bfloat16)

    def window():
        nkb = kw_ref.shape[2]
        w0 = jnp.clip(ti - (WIN_BLOCKS - 1), 0, nkb - WIN_BLOCKS)
        kw = kw_ref[0, 0, pl.ds(w0, WIN_BLOCKS)].reshape(WIN_BLOCKS * 128, HEAD_DIM)
        vwt = jnp.concatenate([vw_ref[0, 0, w0 + i] for i in range(WIN_BLOCKS)], axis=1)
        diff = qpos - (w0 * 128 + lax.broadcasted_iota(jnp.int32, (WIN_BLOCKS * 128, 1), 0))
        return _attend_t(kw, vwt, qt4, jnp.where((diff >= 0) & (diff < WINDOW), 0.0, NEG), tq)

    n_var = -(-t // SLC_KEY_STEP)
    for c in range(n_var):
        nk = min((c + 1) * SLC_KEY_STEP, t)

        @pl.when((ti * tq) // SLC_KEY_STEP == c)
        def _():
            on = jnp.dot(ex_ref[0:nk], selt, preferred_element_type=jnp.float32)
            kpos = lax.broadcasted_iota(jnp.int32, (nk, 1), 0)
            bias = (on - 1.0) * (-NEG) + jnp.where(kpos <= qpos, 0.0, NEG)
            o_slc = _attend_t(ks_ref[0, 0, 0:nk], vs_ref[0, 0, :, 0:nk], qt4, bias, tq)
            o = o_slc * _gate_row(gt_ref, 1, kvh, tq) + window() * _gate_row(gt_ref, 2, kvh, tq)
            o_ref[0] = (_heads_to_rows(o, tq) + oc_ref[0]).astype(jnp.bfloat16)


def slc_win_prompt(proj3, ks, vst, kw, vwt, selt, o_cmp, *, tq):
    b, t, _ = proj3.shape
    nkb = t // 128
    expand = jnp.asarray((np.arange(t)[:, None] // L_SLC == np.arange(128)[None, :]).astype(np.float32),
                         jnp.bfloat16)
    i4 = lambda bi, hi, ti: (bi, hi, 0, 0)
    i5 = lambda bi, hi, ti: (bi, hi, 0, 0, 0)
    head_cols = pl.BlockSpec((1, tq, 256), lambda bi, hi, ti: (bi, ti, hi))
    return pl.pallas_call(
        functools.partial(_slc_win_kernel, tq=tq),
        grid=(b, N_KV, t // tq),
        in_specs=[head_cols,
                  pl.BlockSpec((1, 1, t, HEAD_DIM), i4),
                  pl.BlockSpec((1, 1, HEAD_DIM, t), i4),
                  pl.BlockSpec((1, 1, nkb, 128, HEAD_DIM), i5),
                  pl.BlockSpec((1, 1, nkb, HEAD_DIM, 128), i5),
                  pl.BlockSpec((1, 1, 128, tq), lambda bi, hi, ti: (bi, hi, 0, ti)),
                  pl.BlockSpec((t, 128), lambda bi, hi, ti: (0, 0)),
                  pl.BlockSpec((1, tq, 256), lambda bi, hi, ti: (bi, ti, C_GATE // 256)),
                  head_cols],
        out_specs=head_cols,
        out_shape=jax.ShapeDtypeStruct((b, t, 1024), jnp.bfloat16),
        compiler_params=_cp(("parallel", "parallel", "arbitrary")),
        name="slc_win_prompt",
    )(proj3, ks, vst, kw, vwt, selt, expand, proj3, o_cmp)


def _kv_heads(kv_rows, b, t):
    kv = kv_rows.reshape(b, t, 2, N_KV, HEAD_DIM).astype(jnp.bfloat16)
    return kv[:, :, 0].transpose(0, 2, 1, 3), kv[:, :, 1].transpose(0, 2, 3, 1)


def _kv_blocks(kv_rows, b, t):
    k, vt = _kv_heads(kv_rows, b, t)
    return (k.reshape(b, N_KV, t // 128, 128, HEAD_DIM),
            vt.reshape(b, N_KV, HEAD_DIM, t // 128, 128).transpose(0, 1, 3, 2, 4))


def _out_proj_kernel(ro_ref, rg_ref, nsa_ref, x_ref, wo_ref, y_ref, a_ref):
    @pl.when(pl.program_id(1) == 0)
    def _():
        a_ref[:, 0:D_RWKV] = (ro_ref[...] * rg_ref[...]).astype(jnp.bfloat16)
        a_ref[:, D_RWKV:D_MODEL] = nsa_ref[...]

    y_ref[...] = x_ref[...] + jnp.dot(a_ref[...], wo_ref[...], preferred_element_type=jnp.float32)


def out_proj(ro, rg, nsa_bf16, x, wo_bf16, *, tm, tn):
    m = x.shape[0]
    row = lambda w: pl.BlockSpec((tm, w), lambda i, j: (i, 0))
    return pl.pallas_call(
        _out_proj_kernel,
        grid=(m // tm, D_MODEL // tn),
        in_specs=[row(D_RWKV), row(D_RWKV), row(D_NSA),
                  pl.BlockSpec((tm, tn), lambda i, j: (i, j)),
                  pl.BlockSpec((D_MODEL, tn), lambda i, j: (0, j))],
        out_specs=pl.BlockSpec((tm, tn), lambda i, j: (i, j)),
        out_shape=jax.ShapeDtypeStruct((m, D_MODEL), jnp.float32),
        scratch_shapes=[pltpu.VMEM((tm, D_MODEL), jnp.bfloat16)],
        compiler_params=_cp(("parallel", "arbitrary")),
        name="out_proj",
    )(ro, rg, nsa_bf16, x, wo_bf16)


def _nsa_combine_kernel(oc_ref, os_ref, ow_ref, gt_ref, o_ref):
    g = _sigmoid(gt_ref[...])
    o_ref[...] = (g[:, 0] * oc_ref[...] + g[:, 1] * os_ref[...] + g[:, 2] * ow_ref[...]).astype(jnp.bfloat16)


def nsa_combine_sample(o_cmp, o_slc, o_win, gate_logits):
    return pl.pallas_call(
        _nsa_combine_kernel,
        out_shape=jax.ShapeDtypeStruct(o_cmp.shape, jnp.bfloat16),
        name="nsa_combine_sample",
    )(o_cmp, o_slc, o_win, gate_logits)


def _ffn_kernel(x_ref, gf_ref, wg_ref, wu_ref, wd_ref, gl_ref, y_ref, h_ref, acc_ref):
    f = pl.program_id(1)

    @pl.when(f == 0)
    def _():
        x = x_ref[...]
        ms = jnp.mean(x * x, axis=-1, keepdims=True)
        h_ref[...] = (x * lax.rsqrt(ms + RMS_EPS) * gf_ref[...]).astype(jnp.bfloat16)
        acc_ref[...] = jnp.zeros_like(acc_ref)

    h = h_ref[...]
    gate = jnp.dot(h, wg_ref[...], preferred_element_type=jnp.float32)
    up = jnp.dot(h, wu_ref[...], preferred_element_type=jnp.float32)
    act = (_silu(gate) * up).astype(jnp.bfloat16)
    acc_ref[...] += jnp.dot(act, wd_ref[...], preferred_element_type=jnp.float32)

    @pl.when(f == pl.num_programs(1) - 1)
    def _():
        z = x_ref[...] + acc_ref[...]
        ms = jnp.mean(z * z, axis=-1, keepdims=True)
        y_ref[...] = z * lax.rsqrt(ms + RMS_EPS) * gl_ref[...]


def ffn_final(x, g_ffn, wg, wu, wd, g_final, *, tm, tf):
    m = x.shape[0]
    dff = wg.shape[1]
    return pl.pallas_call(
        _ffn_kernel,
        grid=(m // tm, dff // tf),
        in_specs=[pl.BlockSpec((tm, D_MODEL), lambda i, f: (i, 0)),
                  pl.BlockSpec((1, D_MODEL), lambda i, f: (0, 0)),
                  pl.BlockSpec((D_MODEL, tf), lambda i, f: (0, f)),
                  pl.BlockSpec((D_MODEL, tf), lambda i, f: (0, f)),
                  pl.BlockSpec((tf, D_MODEL), lambda i, f: (f, 0)),
                  pl.BlockSpec((1, D_MODEL), lambda i, f: (0, 0))],
        out_specs=pl.BlockSpec((tm, D_MODEL), lambda i, f: (i, 0)),
        out_shape=jax.ShapeDtypeStruct((m, D_MODEL), jnp.float32),
        scratch_shapes=[pltpu.VMEM((tm, D_MODEL), jnp.bfloat16), pltpu.VMEM((tm, D_MODEL), jnp.float32)],
        compiler_params=_cp(("parallel", "arbitrary")),
        name="ffn_final",
    )(x, g_ffn.reshape(1, -1), wg, wu, wd, g_final.reshape(1, -1))


def _cmp_part_paged_kernel(pt_ref, *refs, n_in):
    x_refs = refs[:n_in]
    perm_ref, w_ref, o_ref, xs_ref = refs[n_in:]
    cpp = x_refs[0].shape[-1] // D_CMP
    n_pair = 2 * (N_KV // 2)
    for p in range(n_in):
        x = x_refs[p][0].reshape(2 * N_KV * HEAD_DIM, x_refs[p].shape[-1])
        xp_all = jnp.dot(x.astype(jnp.bfloat16), perm_ref[...], preferred_element_type=jnp.float32)
        for ep in range(n_pair):
            xp = xp_all[ep * 128:(ep + 1) * 128].T
            for j in range(D_CMP):
                xs_ref[ep, j, p * cpp:(p + 1) * cpp, :] = xp[j * cpp:(j + 1) * cpp]
    for e in range(2):
        for hp in range(2):
            acc = None
            for j in range(0, D_CMP, 2):
                x2 = jnp.concatenate([xs_ref[e * 2 + hp, j], xs_ref[e * 2 + hp, j + 1]], axis=1)
                w2 = jnp.concatenate([w_ref[e, j], w_ref[e, j + 1]], axis=0)
                d = _bdot(x2, w2)
                acc = d if acc is None else acc + d
            c0 = e * 256 + hp * 128
            o_ref[0, :, c0:c0 + 128] = acc[:, :128]
            o_ref[0, :, 512 + c0:512 + c0 + 128] = acc[:, 128:]


def cmp_part_sample(pool_t, page_table, w1x_bf16, *, pp):
    b, n_pages = page_table.shape
    page = pool_t.shape[-1]
    cpp = page // D_CMP

    def xspec(k):
        return pl.BlockSpec((1, 2, N_KV, HEAD_DIM, page), lambda bi, gi, pt: (pt[bi, gi * pp + k], 0, 0, 0, 0))

    r = np.arange(page)
    perm = np.zeros((page, page), np.float32)
    perm[r, (r % D_CMP) * cpp + r // D_CMP] = 1.0
    grid_spec = pltpu.PrefetchScalarGridSpec(
        num_scalar_prefetch=1,
        grid=(b, n_pages // pp),
        in_specs=[xspec(k) for k in range(pp)]
        + [pl.BlockSpec((page, page), lambda bi, gi, pt: (0, 0)),
           pl.BlockSpec((2, D_CMP, 128, 256), lambda bi, gi, pt: (0, 0, 0, 0))],
        out_specs=pl.BlockSpec((1, cpp * pp, 1024), lambda bi, gi, pt: (bi, gi, 0)),
        scratch_shapes=[pltpu.VMEM((N_KV, D_CMP, pp * cpp, 128), jnp.float32)],
    )
    return pl.pallas_call(
        functools.partial(_cmp_part_paged_kernel, n_in=pp),
        grid_spec=grid_spec,
        out_shape=jax.ShapeDtypeStruct((b, n_pages * cpp, 1024), jnp.float32),
        compiler_params=_cp(("parallel", "arbitrary")),
        name="cmp_part_sample",
    )(page_table, *([pool_t] * pp), jnp.asarray(perm, jnp.bfloat16), w1x_bf16)


def _fold_heads(o_ext):
    row_kvh = lax.broadcasted_iota(jnp.int32, (N_HEADS, 1), 0) // GQA
    out = jnp.zeros((N_HEADS, HEAD_DIM), jnp.float32)
    for kvh in range(N_KV):
        out = out + jnp.where(row_kvh == kvh, o_ext[:, kvh * HEAD_DIM:(kvh + 1) * HEAD_DIM], 0.0)
    return out


def _cmp_attn_sample_kernel(part_ref, pe_ref, w2bd_ref, q_ref, g8_ref, ov_ref, o_ref, imps_ref, *, q_pos):
    part = part_ref[0]
    n = part.shape[0]
    kc = _split2(_dot3(_split2(_cmp_hidden(part, pe_ref, 0)), _split2(w2bd_ref[0])))
    vc = _bdot(_cmp_hidden(part, pe_ref, 1), w2bd_ref[1])
    qh, ql = _split2(q_ref[0])
    s = (_dot_nt(qh, kc[0]) + (_dot_nt(qh, kc[1]) + _dot_nt(ql, kc[0]))) * SCALE
    cidx = lax.broadcasted_iota(jnp.int32, (1, n), 1)
    cmask = (cidx * D_CMP + (L_CMP - 1)) <= q_pos
    s = jnp.where(cmask, s, NEG)
    m = jnp.max(s, axis=-1, keepdims=True)
    ex = jnp.where(cmask, jnp.exp(s - m), 0.0)
    p = ex / jnp.maximum(jnp.sum(ex, axis=-1, keepdims=True), 1e-30)
    o_ref[0] = _fold_heads(_bdot(p, vc))
    ih, il = _split2(_hdot(g8_ref[...], p))
    ov = ov_ref[...]
    imps_ref[0] = (jnp.dot(ih, ov, preferred_element_type=jnp.float32)
                   + jnp.dot(il, ov, preferred_element_type=jnp.float32))


def cmp_attn_sample(part, pe, w2bd, q16ext, g8, ov_bf16, *, q_pos):
    b, n, _ = part.shape
    nbp = ov_bf16.shape[1]
    c3 = lambda bi: (0, 0, 0)
    return pl.pallas_call(
        functools.partial(_cmp_attn_sample_kernel, q_pos=q_pos),
        grid=(b,),
        in_specs=[pl.BlockSpec((1, n, 1024), lambda bi: (bi, 0, 0)),
                  pl.BlockSpec((2, 8, 256), c3),
                  pl.BlockSpec((2, 256, 256), c3),
                  pl.BlockSpec((1, N_HEADS, 256), lambda bi: (bi, 0, 0)),
                  pl.BlockSpec((8, N_HEADS), lambda bi: (0, 0)),
                  pl.BlockSpec((n, nbp), lambda bi: (0, 0))],
        out_specs=[pl.BlockSpec((1, N_HEADS, HEAD_DIM), lambda bi: (bi, 0, 0)),
                   pl.BlockSpec((1, 8, nbp), lambda bi: (bi, 0, 0))],
        out_shape=[jax.ShapeDtypeStruct((b, N_HEADS, HEAD_DIM), jnp.float32),
                   jax.ShapeDtypeStruct((b, 8, nbp), jnp.float32)],
        compiler_params=_cp(("parallel",)),
        name="cmp_attn_sample",
    )(part, pe, w2bd, q16ext, g8, ov_bf16)


def _topk_kernel(sc_ref, idx_ref, *, n_blk, q_blk, k_sel):
    imps = sc_ref[...]
    nbp = imps.shape[0]
    sidx = lax.broadcasted_iota(jnp.int32, (nbp, 1), 0)
    sf = sidx.astype(jnp.float32)
    forced = (sidx == 0) | (sidx == q_blk) | (sidx == q_blk - 1)
    causal = (sidx <= q_blk) & (sidx < n_blk)
    score = jnp.where(forced, -NEG, jnp.where(causal, imps, NEG))
    rows = []
    for _ in range(k_sel):
        m = jnp.max(score, axis=0, keepdims=True)
        pick = jnp.min(jnp.where(score == m, sf, float(nbp)), axis=0, keepdims=True)
        rows.append(jnp.where(m > 0.5 * NEG, pick, -1.0))
        score = jnp.where(sf == pick, 2.0 * NEG, score)
    idx_ref[...] = jnp.concatenate(rows, axis=0).astype(jnp.int32)


def topk_blocks(scores, *, n_blk, q_blk, k_sel):
    nbp, lanes = scores.shape
    return pl.pallas_call(
        functools.partial(_topk_kernel, n_blk=n_blk, q_blk=q_blk, k_sel=k_sel),
        out_shape=jax.ShapeDtypeStruct((k_sel, lanes), jnp.int32),
        name="topk_blocks",
    )(scores)


def _slc_sample_kernel(idx_ref, page_ref, *refs, n_sel, new_blk):
    blk_refs = refs[:N_KV * n_sel]
    q_ref, knew_ref, vnew_ref, o_ref = refs[N_KV * n_sel:]
    b = pl.program_id(0)
    page = blk_refs[0].shape[-1]
    half_of_lane = lax.broadcasted_iota(jnp.int32, (1, page), 1) // L_SLC
    for kvh in range(N_KV):
        kts, vts, biases = [], [], []
        has_new = False
        for n in range(n_sel):
            blk = blk_refs[kvh * n_sel + n]
            kts.append(blk[0, 0, 0])
            vts.append(blk[0, 1, 0])
            s_n = idx_ref[(b * N_KV + kvh) * n_sel + n]
            live = (half_of_lane == s_n % (page // L_SLC)) & ((s_n >= 0) & (s_n != new_blk))
            biases.append(jnp.where(live, 0.0, NEG))
            has_new = jnp.logical_or(has_new, s_n == new_blk)
        kt = jnp.concatenate(kts, axis=1).astype(jnp.bfloat16)
        vt = jnp.concatenate(vts, axis=1).astype(jnp.bfloat16)
        q = q_ref[0, kvh]
        s = (jnp.dot(q.astype(jnp.bfloat16), kt, preferred_element_type=jnp.float32) * SCALE
             + jnp.concatenate(biases, axis=1))
        s_new = jnp.where(has_new, jnp.sum(q * knew_ref[0, kvh], axis=-1, keepdims=True) * SCALE, NEG)
        m = jnp.maximum(jnp.max(s, axis=-1, keepdims=True), s_new)
        ex = jnp.exp(s - m)
        ex_new = jnp.where(has_new, jnp.exp(s_new - m), 0.0)
        denom = jnp.maximum(jnp.sum(ex, axis=-1, keepdims=True) + ex_new, 1e-30)
        o_ref[0, kvh] = (_dot_nt(ex.astype(jnp.bfloat16), vt) + ex_new * vnew_ref[0, kvh]) / denom


def slc_sample(idx_flat, page_table, pool_t, q4, knew, vnew, *, n_sel, new_blk):
    b, n_pages = page_table.shape
    page = pool_t.shape[-1]
    bpp = page // L_SLC
    pick_page = jnp.clip(idx_flat.reshape(b, N_KV * n_sel), 0, bpp * n_pages - 1) // bpp
    page_flat = jnp.take_along_axis(page_table, pick_page, axis=1).reshape(-1)

    def bspec(kvh, n):
        return pl.BlockSpec((1, 2, 1, HEAD_DIM, page),
                            lambda bi, idx, pg: (pg[(bi * N_KV + kvh) * n_sel + n], 0, kvh, 0, 0))

    small = lambda r: pl.BlockSpec((1, N_KV, r, HEAD_DIM), lambda bi, idx, pt: (bi, 0, 0, 0))
    grid_spec = pltpu.PrefetchScalarGridSpec(
        num_scalar_prefetch=2,
        grid=(b,),
        in_specs=[bspec(kvh, n) for kvh in range(N_KV) for n in range(n_sel)] + [small(8), small(1), small(1)],
        out_specs=small(8),
    )
    return pl.pallas_call(
        functools.partial(_slc_sample_kernel, n_sel=n_sel, new_blk=new_blk),
        grid_spec=grid_spec,
        out_shape=jax.ShapeDtypeStruct((b, N_KV, 8, HEAD_DIM), jnp.float32),
        compiler_params=_cp(("arbitrary",)),
        name="slc_sample",
    )(idx_flat, page_flat, *([pool_t] * (N_KV * n_sel)), q4, knew, vnew)


def _win_sample_kernel(w_ref, q_ref, o_ref):
    w = w_ref[0]
    s = _dot_nt(q_ref[0].astype(jnp.bfloat16), w[:, :KV_W].astype(jnp.bfloat16)) * SCALE
    m = jnp.max(s, axis=-1, keepdims=True)
    ex = jnp.exp(s - m)
    p = ex / jnp.maximum(jnp.sum(ex, axis=-1, keepdims=True), 1e-30)
    o_ref[0] = _fold_heads(_bdot(p, w[:, KV_W:]))


def win_sample(win_rows, q16ext):
    b, nk, _ = win_rows.shape
    return pl.pallas_call(
        _win_sample_kernel,
        grid=(b,),
        in_specs=[pl.BlockSpec((1, nk, 2 * KV_W), lambda bi: (bi, 0, 0)),
                  pl.BlockSpec((1, N_HEADS, KV_W), lambda bi: (bi, 0, 0))],
        out_specs=pl.BlockSpec((1, N_HEADS, HEAD_DIM), lambda bi: (bi, 0, 0)),
        out_shape=jax.ShapeDtypeStruct((b, N_HEADS, HEAD_DIM), jnp.float32),
        compiler_params=_cp(("parallel",)),
        name="win_sample",
    )(win_rows, q16ext)


def _prep_weights(w_in, mu, w2, a2, g2):
    o3 = 3 * D_RWKV
    nsa0 = R_COLS
    cols = [w_in[:, nsa0:nsa0 + D_NSA],
            w_in[:, 0:o3],
            w_in[:, nsa0 + D_NSA:nsa0 + D_NSA + 6 * KV_W],
            w_in[:, o3:R_COLS],
            w_in[:, nsa0 + D_NSA + 6 * KV_W:],
            jnp.zeros((D_MODEL, D_IN_PAD - C_GATE - 3 * N_HEADS), w_in.dtype)]
    w_in_p = jnp.concatenate(cols, axis=1).astype(jnp.bfloat16)
    z = jnp.zeros((LORA_ALL, D_RWKV), jnp.float32)
    return dict(w_in_p=w_in_p, mu_p=mu,
                w2p=z.at[0:LORA_W].set(w2), a2p=z.at[LORA_W:LORA_W + LORA_A].set(a2),
                g2p=z.at[LORA_W + LORA_A:].set(g2))


def _layer_weights(layer, g_mix, w_in, mu, w0, w2, a0, a2, g2, k_k, k_a, r_k, lnx_w, lnx_b, phi_pe, phi_w1,
                   phi_w2, w_o, g_ffn, w_gate, w_up, w_down):
    w = _prep_weights(w_in[layer], mu[layer], w2[layer], a2[layer], g2[layer])
    w1t, pe8, w2k, w2vt, w2bd = _phi_weights(phi_pe[layer], phi_w1[layer], phi_w2[layer])
    w1x = _expand_w1(phi_w1[layer])
    w.update(g_mix=g_mix[layer], w0=w0[layer], a0=a0[layer], k_k=k_k[layer], k_a=k_a[layer], r_k=r_k[layer],
             lnx_w=lnx_w[layer], lnx_b=lnx_b[layer], w1x=w1x, w1x_bf16=w1x.astype(jnp.bfloat16),
             pe=pe_term(pe8, w1t), w2k=w2k, w2vt=w2vt, w2bd=w2bd,
             wo=w_o[layer].astype(jnp.bfloat16), g_ffn=g_ffn[layer],
             wg=w_gate[layer].astype(jnp.bfloat16), wu=w_up[layer].astype(jnp.bfloat16),
             wd=w_down[layer].astype(jnp.bfloat16))
    return w


def _pick(n, pref):
    while n % pref:
        pref //= 2
    return pref


def _prompt_layer(x_prompt, W, g_final):
    B, T, _ = x_prompt.shape
    M = B * T
    x2 = x_prompt.reshape(M, D_MODEL)
    proj = proj_matmul(x2, W["g_mix"], W["w_in_p"], apply_norm=True, tm=_pick(M, 1024), tn=1024)
    proj3 = proj.reshape(B, T, D_IN_PAD)
    ro, rg, wkv = rwkv_mix(proj, jnp.zeros((B, 1, R_COLS), jnp.float32), W,
                           jnp.zeros((B, H_R, 64, 64), jnp.float32), b=B, t=T, tm=_pick(T, 256), tc=_pick(T, 64))
    cmp_kv = proj3[..., C_CMP:C_CMP + 2 * KV_W]
    slc_kv = proj3[..., C_SLC:C_SLC + 2 * KV_W]
    win_kv = proj3[..., C_WIN:C_WIN + 2 * KV_W]
    nch = T // D_CMP
    part = cmp_part_prompt(cmp_kv.reshape(B, nch, D_CMP * 2 * KV_W), W["w1x"])
    ovt = _overlap_t(128, nch - 1, nch).astype(jnp.bfloat16)
    o_cmp, sel = cmp_select_prompt(part, W["pe"], W["w2k"], W["w2vt"], proj3, ovt, tq=128)
    ks, vs = _kv_heads(slc_kv, B, T)
    kw, vw = _kv_blocks(win_kv, B, T)
    nsa = slc_win_prompt(proj3, ks, vs, kw, vw, sel, o_cmp, tq=128)
    r2 = lambda z: z.reshape(M, 1024)
    x1 = out_proj(r2(ro), r2(rg), r2(nsa), x2, W["wo"], tm=_pick(M, 512), tn=1024)
    y = ffn_final(x1, W["g_ffn"], W["wg"], W["wu"], W["wd"], g_final, tm=_pick(M, 512), tf=512)
    kv5 = lambda z: z.reshape(B, T, 2, N_KV, HEAD_DIM)
    nwin = min(WINDOW, T)
    return y.reshape(B, T, D_MODEL), kv5(cmp_kv), kv5(slc_kv), kv5(win_kv)[:, T - nwin:], wkv


def _sample_layer(x_sample, xn_s, cache_cmp, cache_slc, cache_win, wkv0, shift, page_table, W, g_final):
    b = x_sample.shape[0]
    n_pool, page = cache_cmp.shape[0], cache_cmp.shape[1]
    n_pages = page_table.shape[1]
    past_len = n_pages * page
    n_rows = -(-(past_len + 1) // L_SLC) * L_SLC
    n_blk = n_rows // L_SLC
    q_blk = past_len // L_SLC
    nbp = -(-n_blk // 128) * 128

    rows = jnp.concatenate([xn_s, shift], axis=0)
    proj = proj_matmul(rows, W["g_mix"], W["w_in_p"], apply_norm=False, tm=2 * b, tn=1024)
    prev = jnp.concatenate([proj[b:, C_R:C_R + 3 * D_RWKV], proj[b:, C_LORA:C_LORA + LORA_ALL]], axis=1)
    ro, rg, wkv = rwkv_mix(proj[:b], prev, W, wkv0, b=b, t=1, tm=1, tc=1)

    q = proj[:b, C_Q:C_Q + D_NSA].reshape(b, N_KV, GQA, 1, HEAD_DIM)
    eye = jnp.eye(N_KV, dtype=jnp.float32).reshape(1, N_KV, 1, N_KV, 1)
    qext = (q * eye).reshape(b, N_KV, GQA, KV_W)
    q16ext = qext.reshape(b, N_HEADS, KV_W)
    cmp_new = proj[:b, C_CMP:C_CMP + 2 * KV_W]
    slc_new = proj[:b, C_SLC:C_SLC + 2 * KV_W]
    win_new = proj[:b, C_WIN:C_WIN + 2 * KV_W]

    part = cmp_part_sample(cache_cmp.transpose(0, 2, 3, 4, 1), page_table, W["w1x_bf16"], pp=_pick(n_pages, 32))
    nc_rows = part.shape[1]
    ov = _overlap_t(nbp, n_rows // D_CMP - L_CMP // D_CMP + 1, nc_rows)[:n_blk].T
    ov = jnp.pad(ov, ((0, 0), (0, nbp - n_blk)))
    g8 = jnp.asarray((np.arange(8)[:, None] == np.arange(N_HEADS)[None, :] // GQA).astype(np.float32))
    o_cmp, imps = cmp_attn_sample(part, W["pe"], W["w2bd"], q16ext, g8, ov.astype(jnp.bfloat16), q_pos=past_len)
    scores = imps[:, :N_KV].transpose(2, 0, 1).reshape(nbp, b * N_KV)
    k_sel = min(N_SELECT, n_blk)
    idx = topk_blocks(scores, n_blk=n_blk, q_blk=q_blk, k_sel=k_sel)
    q4 = jnp.pad(q.reshape(b, N_KV, GQA, HEAD_DIM), ((0, 0), (0, 0), (0, 8 - GQA), (0, 0)))
    slc_new5 = slc_new.reshape(b, 2, N_KV, 1, HEAD_DIM)
    o_slc = slc_sample(idx.T.reshape(-1), page_table, cache_slc.transpose(0, 2, 3, 4, 1), q4,
                       slc_new5[:, 0], slc_new5[:, 1], n_sel=k_sel, new_blk=q_blk)
    o_slc = o_slc[:, :, :GQA].reshape(b, D_NSA)

    n_buf = cache_win.shape[1]
    win_all = jnp.concatenate([cache_win, win_new.reshape(b, 1, 2, N_KV, HEAD_DIM)], axis=1)
    win_keep = win_all[:, n_buf + 1 - min(WINDOW, n_buf + 1):]
    o_win = win_sample(win_keep.reshape(b, -1, 2 * KV_W), q16ext)

    x2 = x_sample.reshape(b, D_MODEL)
    gate_logits = proj[:b, C_GATE:C_GATE + 3 * N_HEADS].reshape(b, 3, N_HEADS, 1)
    nsa = nsa_combine_sample(o_cmp, o_slc.reshape(b, N_HEADS, HEAD_DIM), o_win, gate_logits)
    x1 = out_proj(ro.reshape(b, -1), rg.reshape(b, -1), nsa.reshape(b, D_NSA), x2, W["wo"], tm=b, tn=1024)
    y = ffn_final(x1, W["g_ffn"], W["wg"], W["wu"], W["wd"], g_final, tm=b, tf=512)
    kv5 = lambda z: z.reshape(b, 1, 2, N_KV, HEAD_DIM)
    return y.reshape(b, 1, D_MODEL), kv5(cmp_new), kv5(slc_new), win_keep, wkv


def kernel(x_prompt, x_sample, cache_cmp_kv, cache_slc_kv, cache_win_kv, state_wkv, state_shift, page_table, g_mix, w_in, mu, w0, w2, a0, a2, g2, k_k, k_a, r_k, lnx_w, lnx_b, phi_pe, phi_w1, phi_w2, w_o, g_ffn, w_gate, w_up, w_down, g_final):
    W = _layer_weights(0, g_mix, w_in, mu, w0, w2, a0, a2, g2, k_k, k_a, r_k, lnx_w, lnx_b, phi_pe, phi_w1,
                       phi_w2, w_o, g_ffn, w_gate, w_up, w_down)
    bp, bs = x_prompt.shape[0], x_sample.shape[0]
    small = jnp.concatenate([x_sample.reshape(bs, D_MODEL), x_prompt[:, -1]], axis=0)
    pad = (-small.shape[0]) % 8
    xn_small = rmsnorm_rows(jnp.pad(small, ((0, pad), (0, 0))), W["g_mix"])
    xn_s, shift_p = xn_small[:bs], xn_small[bs:bs + bp]

    y_p, cmp_p, slc_p, win_p, wkv_p = _prompt_layer(x_prompt, W, g_final)
    y_s, cmp_s, slc_s, win_s, wkv_s = _sample_layer(x_sample, xn_s, cache_cmp_kv[0], cache_slc_kv[0],
                                                    cache_win_kv[0], state_wkv[0], state_shift[0], page_table,
                                                    W, g_final)
    return (y_p, y_s, cmp_p[None], slc_p[None], win_p[None], wkv_p[None], shift_p[None],
            cmp_s[None], slc_s[None], win_s[None], wkv_s[None], xn_s[None])
```

```python
import functools

import jax
import jax.numpy as jnp
import numpy as np
from jax import lax
from jax.experimental import pallas as pl
from jax.experimental.pallas import tpu as pltpu

D_MODEL = 2048
D_RWKV = 1024
D_NSA = 1024
HEAD_DIM_R = 64
H_R = 16
LORA_W = 96
LORA_A = 96
LORA_G = 64
LORA_ALL = LORA_W + LORA_A + LORA_G
HEAD_DIM = 64
N_HEADS = 16
N_KV = 4
GQA = 4
KV_W = 256
L_CMP = 32
D_CMP = 16
L_SLC = 64
N_SELECT = 16
WINDOW = 512
PHI_HIDDEN = 64
R_COLS = 3 * D_RWKV + LORA_ALL
RMS_EPS = 1e-6
LNX_EPS = 64e-5
SCALE = HEAD_DIM ** -0.5
NEG = -1e30

C_Q, C_R, C_K, C_V = 0, 1024, 2048, 3072
C_CMP, C_SLC, C_WIN = 4096, 4608, 5120
C_LORA, C_GATE = 5632, 5888
D_IN_PAD = 6144

VMEM_LIMIT = 48 * 1024 * 1024
HI = lax.Precision.HIGHEST


def _cp(sem, limit=VMEM_LIMIT):
    return pltpu.CompilerParams(dimension_semantics=sem, vmem_limit_bytes=limit)


def _bdot(a, b):
    return jnp.dot(a.astype(jnp.bfloat16), b.astype(jnp.bfloat16), preferred_element_type=jnp.float32)


def _hdot(a, b):
    return jnp.dot(a, b, precision=HI, preferred_element_type=jnp.float32)


def _dot_nt(a, b, precision=None):
    return lax.dot_general(a, b, (((1,), (1,)), ((), ())), precision=precision,
                           preferred_element_type=jnp.float32)


def _sigmoid(x):
    return 1.0 / (1.0 + jnp.exp(-x))


def _silu(x):
    return x * _sigmoid(x)


def _rmsnorm_rows_kernel(x_ref, g_ref, o_ref):
    x = x_ref[...]
    ms = jnp.mean(x * x, axis=-1, keepdims=True)
    o_ref[...] = x * lax.rsqrt(ms + RMS_EPS) * g_ref[...]


def rmsnorm_rows(x, g):
    m, d = x.shape
    return pl.pallas_call(
        _rmsnorm_rows_kernel,
        out_shape=jax.ShapeDtypeStruct((m, d), jnp.float32),
        name="rmsnorm_rows",
    )(x, g.reshape(1, d))


def _proj_kernel(x_ref, g_ref, w_ref, o_ref, xn_ref, *, apply_norm):
    @pl.when(pl.program_id(1) == 0)
    def _():
        x = x_ref[...]
        if apply_norm:
            ms = jnp.mean(x * x, axis=-1, keepdims=True)
            x = x * lax.rsqrt(ms + RMS_EPS) * g_ref[...]
        xn_ref[...] = x.astype(jnp.bfloat16)

    o_ref[...] = jnp.dot(xn_ref[...], w_ref[...], preferred_element_type=jnp.float32)


def proj_matmul(x, g, w_bf16, *, apply_norm, tm, tn):
    m, d = x.shape
    n = w_bf16.shape[1]
    return pl.pallas_call(
        functools.partial(_proj_kernel, apply_norm=apply_norm),
        grid=(m // tm, n // tn),
        in_specs=[
            pl.BlockSpec((tm, d), lambda i, j: (i, 0)),
            pl.BlockSpec((1, d), lambda i, j: (0, 0)),
            pl.BlockSpec((d, tn), lambda i, j: (0, j)),
        ],
        out_specs=pl.BlockSpec((tm, tn), lambda i, j: (i, j)),
        out_shape=jax.ShapeDtypeStruct((m, n), jnp.float32),
        scratch_shapes=[pltpu.VMEM((tm, d), jnp.bfloat16)],
        compiler_params=_cp(("parallel", "arbitrary")),
        name="proj_matmul",
    )(x, g.reshape(1, d), w_bf16)


def _rwkv_prep_kernel(pr_ref, pk_ref, pv_ref, pl_ref, prev_ref, mu_ref, w0_ref, a0_ref, kk_ref, ka_ref,
                      w2_ref, a2_ref, g2_ref,
                      r_out, w_out, k_out, v_out, kk_out, a_out, g_out, carry_ref, *, row_prev):
    tm = pr_ref.shape[1]
    pr, pk, pv, plo = pr_ref[0], pk_ref[0], pv_ref[0], pl_ref[0]
    if row_prev:
        def shift_mix(p, lo, hi):
            return p + (prev_ref[0, :, lo:hi] - p) * mu_ref[:, lo:hi]
    else:
        @pl.when(pl.program_id(1) == 0)
        def _():
            carry_ref[...] = prev_ref[0]

        row0 = lax.broadcasted_iota(jnp.int32, (tm, 1), 0) == 0

        def shift_mix(p, lo, hi):
            prev = pltpu.roll(p, 1, axis=0)
            prev = jnp.where(row0, carry_ref[:, lo:hi], prev)
            return p + (prev - p) * mu_ref[:, lo:hi]

    xr = shift_mix(pr, 0, 1024)
    xk = shift_mix(pk, 1024, 2048)
    xv = shift_mix(pv, 2048, 3072)
    xl = shift_mix(plo, 3072, 3328)
    if not row_prev:
        last = tm - 1
        carry_ref[:, 0:1024] = pr[last:last + 1]
        carry_ref[:, 1024:2048] = pk[last:last + 1]
        carry_ref[:, 2048:3072] = pv[last:last + 1]
        carry_ref[:, 3072:3328] = plo[last:last + 1]

    wl = _dot3(_split2(jnp.tanh(xl)), _split2(w2_ref[...]))
    al = _dot3(_split2(xl), _split2(a2_ref[...]))
    gl = _dot3(_split2(_sigmoid(xl)), _split2(g2_ref[...]))
    z = -(w0_ref[...] + wl)
    softplus = jnp.maximum(z, 0.0) + jnp.log(1.0 + jnp.exp(-jnp.abs(z)))
    w_log = -softplus - 0.5
    decay = jnp.exp(-jnp.exp(w_log))
    a = _sigmoid(a0_ref[...] + al)
    r_out[0] = xr
    w_out[0] = decay
    k_out[0] = xk * (1.0 + (a - 1.0) * ka_ref[...])
    v_out[0] = xv
    kk_out[0] = xk * kk_ref[...]
    a_out[0] = a
    g_out[0] = gl


def rwkv_prep(proj, prev, mu_p, w0, a0, k_k, k_a, w2p, a2p, g2p, *, b, t, tm, row_prev=False):
    proj3 = proj.reshape(b, t, D_IN_PAD)
    nt = t // tm
    prev_spec = (pl.BlockSpec((1, tm, R_COLS), lambda bi, ti: (bi, ti, 0)) if row_prev
                 else pl.BlockSpec((1, 1, R_COLS), lambda bi, ti: (bi, 0, 0)))
    row = lambda blk, w: pl.BlockSpec((1, tm, w), lambda bi, ti: (bi, ti, blk))
    vec = lambda w: pl.BlockSpec((1, w), lambda bi, ti: (0, 0))
    mat = pl.BlockSpec((LORA_ALL, D_RWKV), lambda bi, ti: (0, 0))
    out = pl.BlockSpec((1, tm, D_RWKV), lambda bi, ti: (bi, ti, 0))
    shp = jax.ShapeDtypeStruct((b, t, D_RWKV), jnp.float32)
    return pl.pallas_call(
        functools.partial(_rwkv_prep_kernel, row_prev=row_prev),
        grid=(b, nt),
        in_specs=[row(C_R // 1024, 1024), row(C_K // 1024, 1024), row(C_V // 1024, 1024),
                  row(C_LORA // 256, 256),
                  prev_spec,
                  vec(R_COLS), vec(D_RWKV), vec(D_RWKV), vec(D_RWKV), vec(D_RWKV),
                  mat, mat, mat],
        out_specs=[out] * 7,
        out_shape=[shp] * 7,
        scratch_shapes=[pltpu.VMEM((1, R_COLS), jnp.float32)],
        compiler_params=_cp(("parallel", "arbitrary")),
        name="rwkv_prep",
    )(proj3, proj3, proj3, proj3, prev, mu_p.reshape(1, -1), w0.reshape(1, -1), a0.reshape(1, -1),
      k_k.reshape(1, -1), k_a.reshape(1, -1), w2p, a2p, g2p)


def _rwkv_scan_kernel(r_ref, w_ref, k_ref, kk_ref, a_ref, v_ref, s0_ref, rk_ref, lnw_ref, lnb_ref,
                      o_ref, sfin_ref, s_ref, kkn_ref, b_ref, y_ref, *, halved):
    tc = r_ref.shape[1]
    ni = v_ref.shape[2]

    @pl.when(pl.program_id(1) == 0)
    def _():
        s_ref[...] = s0_ref[0]

    kkraw = kk_ref[0]
    nrm = jnp.sqrt(jnp.sum(kkraw * kkraw, axis=1, keepdims=True))
    kkn = kkraw / jnp.maximum(nrm, 1e-12)
    kkn_ref[...] = kkn
    b_ref[...] = kkn * a_ref[0]

    def step(t, carry):
        w_t = w_ref[0, t]
        k_t = k_ref[0, t]
        r_t = r_ref[0, t]
        kk_t = kkn_ref[t]
        b_t = b_ref[t]
        for i in range(ni):
            s = s_ref[i]
            sa = -jnp.sum(s * kk_t, axis=0, keepdims=True)
            v_i = v_ref[0, t, pl.ds(i, 1), :]
            s = s * w_t + sa * b_t + v_i * k_t
            y_ref[t, pl.ds(i, 1), :] = jnp.sum(s * r_t, axis=0, keepdims=True)
            s_ref[i] = s
        return carry

    lax.fori_loop(0, tc, step, 0)

    y = y_ref[...]
    v = v_ref[0]

    def head_sum(z):
        s = jnp.sum(z, axis=1, keepdims=True)
        if halved:
            s = s + pltpu.roll(s, 64, axis=2)
        return s

    mean = head_sum(y) * (1.0 / HEAD_DIM_R)
    d = y - mean
    var = head_sum(d * d) * (1.0 / HEAD_DIM_R)
    yn = d * lax.rsqrt(var + LNX_EPS) * lnw_ref[0] + lnb_ref[0]
    bonus = jnp.sum(r_ref[0] * k_ref[0] * rk_ref[0], axis=1, keepdims=True)
    o_ref[0] = yn + bonus * v

    @pl.when(pl.program_id(1) == pl.num_programs(1) - 1)
    def _():
        sfin_ref[0] = s_ref[...]


def rwkv_scan(r, w, k, kk, a, v, s0, rk, lnw, lnb, *, tc, halved):
    g, t = r.shape[0], r.shape[1]
    ni = v.shape[2]
    col = pl.BlockSpec((1, tc, 64, 128), lambda gi, ti: (gi, ti, 0, 0))
    vspec = pl.BlockSpec((1, tc, ni, 128), lambda gi, ti: (gi, ti, 0, 0))
    sspec = pl.BlockSpec((1, ni, 64, 128), lambda gi, ti: (gi, 0, 0, 0))
    return pl.pallas_call(
        functools.partial(_rwkv_scan_kernel, halved=halved),
        grid=(g, t // tc),
        in_specs=[col, col, col, col, col, vspec, sspec,
                  pl.BlockSpec((1, 64, 128), lambda gi, ti: (gi, 0, 0)),
                  pl.BlockSpec((1, ni, 128), lambda gi, ti: (gi, 0, 0)),
                  pl.BlockSpec((1, ni, 128), lambda gi, ti: (gi, 0, 0))],
        out_specs=[vspec, sspec],
        out_shape=[jax.ShapeDtypeStruct((g, t, ni, 128), jnp.float32),
                   jax.ShapeDtypeStruct((g, ni, 64, 128), jnp.float32)],
        scratch_shapes=[pltpu.VMEM((ni, 64, 128), jnp.float32),
                        pltpu.VMEM((tc, 64, 128), jnp.float32),
                        pltpu.VMEM((tc, 64, 128), jnp.float32),
                        pltpu.VMEM((tc, ni, 128), jnp.float32)],
        compiler_params=_cp(("parallel", "arbitrary")),
        name="rwkv_scan",
    )(r, w, k, kk, a, v, s0, rk, lnw, lnb)


def _to_scan_layout(x, b, t, halved):
    z = x.reshape(b, t, H_R, 64).transpose(1, 3, 0, 2).reshape(t, 64, b * H_R)
    if halved:
        return jnp.concatenate([z, z], axis=-1)[None]
    g = (b * H_R) // 128
    return z.reshape(t, 64, g, 128).transpose(2, 0, 1, 3)


def _v_to_scan_layout(x, b, t, halved):
    if halved:
        return x.reshape(b, t, H_R, 2, 32).transpose(1, 4, 3, 0, 2).reshape(1, t, 32, 128)
    g = (b * H_R) // 128
    z = x.reshape(b, t, H_R, 64).transpose(1, 3, 0, 2).reshape(t, 64, g, 128)
    return z.transpose(2, 0, 1, 3)


def _vecparam_scan_layout(p, b, halved):
    z = jnp.broadcast_to(p.reshape(1, H_R, 64), (b, H_R, 64))
    if halved:
        return z.reshape(b, H_R, 2, 32).transpose(3, 2, 0, 1).reshape(1, 32, 128)
    g = (b * H_R) // 128
    return z.transpose(2, 0, 1).reshape(64, g, 128).transpose(1, 0, 2)


def _keyparam_scan_layout(p, b, halved):
    z = jnp.broadcast_to(p.reshape(1, H_R, 64), (b, H_R, 64)).transpose(2, 0, 1).reshape(64, b * H_R)
    if halved:
        return jnp.concatenate([z, z], axis=-1)[None]
    g = (b * H_R) // 128
    return z.reshape(64, g, 128).transpose(1, 0, 2)


def _from_scan_layout(o, b, t, halved):
    if halved:
        return o.reshape(t, 32, 2, b, H_R).transpose(3, 0, 4, 2, 1).reshape(b, t, D_RWKV)
    g = o.shape[0]
    return o.transpose(1, 2, 0, 3).reshape(t, 64, b, H_R).transpose(2, 0, 3, 1).reshape(b, t, D_RWKV)


def _state_to_scan_layout(s, b, halved):
    if halved:
        return s.reshape(b, H_R, 2, 32, 64).transpose(3, 4, 2, 0, 1).reshape(1, 32, 64, 128)
    g = (b * H_R) // 128
    return s.reshape(g, 128, 64, 64).transpose(0, 2, 3, 1)


def _state_from_scan_layout(s, b, halved):
    if halved:
        return s.reshape(32, 64, 2, b, H_R).transpose(3, 4, 2, 0, 1).reshape(b, H_R, 64, 64)
    return s.transpose(0, 3, 1, 2).reshape(b, H_R, 64, 64)


def rwkv_mix(proj, prev, wts, s0, *, b, t, tm, tc):
    halved = (b * H_R == 64)
    if t == 1:
        outs = rwkv_prep(proj, prev.reshape(1, b, R_COLS), wts["mu_p"], wts["w0"], wts["a0"], wts["k_k"],
                         wts["k_a"], wts["w2p"], wts["a2p"], wts["g2p"], b=1, t=b, tm=b, row_prev=True)
        r, w, k, v, kk, a, g = [z.reshape(b, 1, D_RWKV) for z in outs]
    else:
        r, w, k, v, kk, a, g = rwkv_prep(proj, prev, wts["mu_p"], wts["w0"], wts["a0"], wts["k_k"], wts["k_a"],
                                         wts["w2p"], wts["a2p"], wts["g2p"], b=b, t=t, tm=tm)
    tr = lambda z: _to_scan_layout(z, b, t, halved)
    o, sfin = rwkv_scan(tr(r), tr(w), tr(k), tr(kk), tr(a), _v_to_scan_layout(v, b, t, halved),
                        _state_to_scan_layout(s0, b, halved),
                        _keyparam_scan_layout(wts["r_k"], b, halved),
                        _vecparam_scan_layout(wts["lnx_w"], b, halved),
                        _vecparam_scan_layout(wts["lnx_b"], b, halved),
                        tc=tc, halved=halved)
    return _from_scan_layout(o, b, t, halved), g, _state_from_scan_layout(sfin, b, halved)


def _cmp_part_kernel(*refs, n_in, precise, prefetch):
    refs = refs[prefetch:]
    x_refs, w_ref, o_ref = refs[:n_in], refs[n_in], refs[n_in + 1]
    for e in range(2):
        for hp in range(2):
            acc = None
            for j in range(D_CMP):
                lo = j * 512 + e * 256 + hp * 128
                xs = [xr[0, :, lo:lo + 128] for xr in x_refs]
                x = xs[0] if n_in == 1 else jnp.concatenate(xs, axis=0)
                d = _hdot(x, w_ref[e, j]) if precise else _bdot(x, w_ref[e, j])
                acc = d if acc is None else acc + d
            c0 = e * 256 + hp * 128
            o_ref[0, :, c0:c0 + 128] = acc[:, :128]
            o_ref[0, :, 512 + c0:512 + c0 + 128] = acc[:, 128:]


def cmp_part_prompt(rows, w1x):
    b, nch, _ = rows.shape
    return pl.pallas_call(
        functools.partial(_cmp_part_kernel, n_in=1, precise=True, prefetch=0),
        grid=(b,),
        in_specs=[pl.BlockSpec((1, nch, 8192), lambda bi: (bi, 0, 0)),
                  pl.BlockSpec((2, D_CMP, 128, 256), lambda bi: (0, 0, 0, 0))],
        out_specs=pl.BlockSpec((1, nch, 1024), lambda bi: (bi, 0, 0)),
        out_shape=jax.ShapeDtypeStruct((b, nch, 1024), jnp.float32),
        compiler_params=_cp(("parallel",)),
        name="cmp_part_prompt",
    )(rows, w1x)


def _expand_w1(phi_w1):
    w1 = phi_w1.reshape(2, 2, D_CMP, HEAD_DIM, PHI_HIDDEN)
    eye = jnp.eye(2, dtype=phi_w1.dtype)
    w = jnp.einsum("ab,esjdf->ejadsbf", eye, w1)
    return w.reshape(2, D_CMP, 128, 256)


def _split2(x):
    hi = x.astype(jnp.bfloat16)
    return hi, (x - hi.astype(jnp.float32)).astype(jnp.bfloat16)


def _dot3(a, b):
    (ah, al), (bh, bl) = a, b
    d = lambda x, y: jnp.dot(x, y, preferred_element_type=jnp.float32)
    return d(ah, bh) + (d(ah, bl) + d(al, bh))


def _pe_term_kernel(pe_ref, w1t_ref, o_ref):
    for e in range(2):
        o_ref[e] = _hdot(pe_ref[e], w1t_ref[e])


def pe_term(pe8, w1t):
    return pl.pallas_call(
        _pe_term_kernel,
        out_shape=jax.ShapeDtypeStruct((2, 8, N_KV * PHI_HIDDEN), jnp.float32),
        name="pe_term",
    )(pe8, w1t)


def _cmp_hidden(part, pe_ref, e):
    n = part.shape[0]
    lo = e * 256
    h = part[:, lo:lo + 256] + pltpu.roll(part[:, 512 + lo:512 + lo + 256], n - 1, axis=0)
    return _silu(h + pe_ref[e, 0:1])


def _cmp_sel_kernel(part_ref, pe_ref, w2k_ref, w2vt_ref, q_ref, ovt_ref, gt_ref, o_ref, sel_ref,
                    kch_ref, kcl_ref, vct_ref, *, tq, n_blk):
    ti = pl.program_id(1)

    @pl.when(ti == 0)
    def _():
        part = part_ref[0]
        hk = _split2(_cmp_hidden(part, pe_ref, 0))
        hv = _cmp_hidden(part, pe_ref, 1)
        for kvh in range(N_KV):
            kch_ref[kvh], kcl_ref[kvh] = _split2(_dot3(hk, _split2(w2k_ref[kvh])))
            vct_ref[kvh] = _dot_nt(w2vt_ref[kvh], hv, HI).astype(jnp.bfloat16)

    nc = kch_ref.shape[1]
    nbp = -(-n_blk // 8) * 8
    k_sel = min(N_SELECT, n_blk)
    qpos = ti * tq + lax.broadcasted_iota(jnp.int32, (1, tq), 1)
    cidx = lax.broadcasted_iota(jnp.int32, (nc, 1), 0)
    cmask = (cidx * D_CMP + (L_CMP - 1)) <= qpos
    sidx = cidx[:nbp]
    qblk = qpos // L_SLC
    forced = (sidx == 0) | (sidx == qblk) | (sidx == qblk - 1)
    causal = sidx <= qblk
    pad = jnp.zeros((128 - nbp, tq), jnp.float32)
    need_rank = (ti + 1) * tq > k_sel * L_SLC

    cmask4 = jnp.concatenate([cmask.astype(jnp.float32)] * GQA, axis=1) > 0.5
    gate_t = _sigmoid(gt_ref[0][:, 0:128]).T
    imp_t = []
    for kvh in range(N_KV):
        qt = (q_ref[0, :, kvh * 256:(kvh + 1) * 256] * SCALE).T
        qt4 = _split2(jnp.concatenate([qt[g * 64:(g + 1) * 64] for g in range(GQA)], axis=1))
        st = _dot3((kch_ref[kvh], kcl_ref[kvh]), qt4)
        st = jnp.where(cmask4, st, NEG)
        m = jnp.max(st, axis=0, keepdims=True)
        ex = jnp.where(cmask4, jnp.exp(st - m), 0.0)
        pt = ex / jnp.maximum(jnp.sum(ex, axis=0, keepdims=True), 1e-30)
        ot = jnp.dot(vct_ref[kvh], pt.astype(jnp.bfloat16), preferred_element_type=jnp.float32)
        gate4 = jnp.concatenate([gate_t[kvh * GQA + g:kvh * GQA + g + 1] for g in range(GQA)], axis=1)
        o_ref[0, :, kvh * 256:(kvh + 1) * 256] = jnp.concatenate(
            [(ot * gate4)[:, g * tq:(g + 1) * tq] for g in range(GQA)], axis=0).T
        impt = pt[:, 0:tq]
        for g in range(1, GQA):
            impt = impt + pt[:, g * tq:(g + 1) * tq]
        imp_t.append(impt)

    @pl.when(need_rank)
    def _():
        ih, il = _split2(jnp.concatenate(imp_t, axis=1))
        ov = ovt_ref[...]
        imps = (jnp.dot(ov, ih, preferred_element_type=jnp.float32)
                + jnp.dot(ov, il, preferred_element_type=jnp.float32))[:nbp]
        tile4 = lambda z: jnp.concatenate([z.astype(jnp.float32)] * N_KV, axis=1) > 0.5
        forced4, causal4 = tile4(forced), tile4(causal)
        score = jnp.where(forced4, -NEG, jnp.where(causal4, imps, NEG))
        rank = jnp.zeros((nbp, N_KV * tq), jnp.float32)
        for s2 in range(n_blk):
            row = score[s2:s2 + 1]
            beats = (row > score) | ((row == score) & (s2 < sidx))
            rank = rank + beats.astype(jnp.float32)
        seln = ((rank < k_sel) & causal4).astype(jnp.float32)
        for kvh in range(N_KV):
            sel_ref[0, kvh] = jnp.concatenate([seln[:, kvh * tq:(kvh + 1) * tq], pad], axis=0)

    @pl.when(jnp.logical_not(need_rank))
    def _():
        keep = jnp.concatenate([causal.astype(jnp.float32), pad], axis=0)
        for kvh in range(N_KV):
            sel_ref[0, kvh] = keep


def cmp_select_prompt(part, pe, w2k, w2vt, proj3, ovt_bf16, *, tq):
    b, nc, _ = part.shape
    t = proj3.shape[1]
    c3 = lambda bi, ti: (0, 0, 0)
    return pl.pallas_call(
        functools.partial(_cmp_sel_kernel, tq=tq, n_blk=t // L_SLC),
        grid=(b, t // tq),
        in_specs=[pl.BlockSpec((1, nc, 1024), lambda bi, ti: (bi, 0, 0)),
                  pl.BlockSpec((2, 8, 256), c3),
                  pl.BlockSpec((N_KV, 256, 64), c3),
                  pl.BlockSpec((N_KV, 64, 256), c3),
                  pl.BlockSpec((1, tq, 1024), lambda bi, ti: (bi, ti, C_Q // 1024)),
                  pl.BlockSpec((128, nc), lambda bi, ti: (0, 0)),
                  pl.BlockSpec((1, tq, 256), lambda bi, ti: (bi, ti, C_GATE // 256))],
        out_specs=[pl.BlockSpec((1, tq, 1024), lambda bi, ti: (bi, ti, 0)),
                   pl.BlockSpec((1, N_KV, 128, tq), lambda bi, ti: (bi, 0, 0, ti))],
        out_shape=[jax.ShapeDtypeStruct((b, t, 1024), jnp.float32),
                   jax.ShapeDtypeStruct((b, N_KV, 128, t), jnp.float32)],
        scratch_shapes=[pltpu.VMEM((N_KV, nc, 64), jnp.bfloat16),
                        pltpu.VMEM((N_KV, nc, 64), jnp.bfloat16),
                        pltpu.VMEM((N_KV, 64, nc), jnp.bfloat16)],
        compiler_params=_cp(("parallel", "arbitrary")),
        name="cmp_select_prompt",
    )(part, pe, w2k, w2vt, proj3, ovt_bf16, proj3)


def _overlap_t(n_blocks_pad, nc, nc_pad):
    c_start = np.arange(nc_pad) * D_CMP
    s_start = np.arange(n_blocks_pad) * L_SLC
    ov = (c_start[None, :] < s_start[:, None] + L_SLC) & (c_start[None, :] + L_CMP > s_start[:, None])
    ov &= (np.arange(nc_pad) < nc)[None, :]
    return jnp.asarray(ov.astype(np.float32))


def _phi_weights(phi_pe, phi_w1, phi_w2):
    w1t = jnp.tile(phi_w1, (1, 1, N_KV))
    pe8 = jnp.broadcast_to(phi_pe.reshape(2, 1, L_CMP * HEAD_DIM), (2, 8, L_CMP * HEAD_DIM))
    z = jnp.zeros((N_KV, N_KV, PHI_HIDDEN, HEAD_DIM), jnp.float32)
    idx = jnp.arange(N_KV)
    w2k = z.at[idx, idx].set(phi_w2[0]).reshape(N_KV, 256, 64)
    w2vt = jnp.transpose(z.at[idx, idx].set(phi_w2[1]), (0, 3, 1, 2)).reshape(N_KV, 64, 256)
    eye = jnp.eye(N_KV, dtype=jnp.float32)
    w2bd = jnp.einsum("ab,efd->eafbd", eye, phi_w2).reshape(2, 256, 256)
    return w1t, pe8, w2k, w2vt, w2bd


def _gate_row(gt_ref, branch, kvh, tq):
    gt = _sigmoid(gt_ref[0][:, 0:128]).T
    col = lax.broadcasted_iota(jnp.int32, (128, 1), 0)
    rows = [jnp.sum(jnp.where(col == branch * N_HEADS + kvh * GQA + g, gt, 0.0), axis=0, keepdims=True)
            for g in range(GQA)]
    return jnp.concatenate(rows, axis=1)


def _heads_to_rows(o, tq):
    return jnp.concatenate([o[:, g * tq:(g + 1) * tq] for g in range(GQA)], axis=0).T


def _attend_t(k, vt, qt4, bias, tq):
    s = jnp.dot(k, qt4, preferred_element_type=jnp.float32) + jnp.concatenate([bias] * GQA, axis=1)
    s = s.astype(jnp.bfloat16)
    p = jnp.exp(s - jnp.max(s, axis=0, keepdims=True))
    vt1 = jnp.concatenate([vt, jnp.ones((16, vt.shape[1]), jnp.bfloat16)], axis=0)
    ol = jnp.dot(vt1, p, preferred_element_type=jnp.float32)
    return ol[:HEAD_DIM] / jnp.maximum(ol[HEAD_DIM:HEAD_DIM + 1], 1e-30)


SLC_KEY_STEP = 256
WIN_BLOCKS = WINDOW // 128 + 1


def _slc_win_kernel(q_ref, ks_ref, vs_ref, kw_ref, vw_ref, sel_ref, ex_ref, gt_ref, oc_ref, o_ref, *, tq):
    kvh = pl.program_id(1)
    ti = pl.program_id(2)
    t = ks_ref.shape[2]
    qt = (q_ref[0] * SCALE).T.astype(jnp.bfloat16)
    qt4 = jnp.concatenate([qt[g * HEAD_DIM:(g + 1) * HEAD_DIM] for g in range(GQA)], axis=1)
    qpos = ti * tq + lax.broadcasted_iota(jnp.int32, (1, tq), 1)
    selt = sel_ref[0, 0].astype(jnp.bfloat16)

    def window():
        nkb = kw_ref.shape[2]
        w0 = jnp.clip(ti - (WIN_BLOCKS - 1), 0, nkb - WIN_BLOCKS)
        kw = kw_ref[0, 0, pl.ds(w0, WIN_BLOCKS)].reshape(WIN_BLOCKS * 128, HEAD_DIM)
        vwt = jnp.concatenate([vw_ref[0, 0, w0 + i] for i in range(WIN_BLOCKS)], axis=1)
        diff = qpos - (w0 * 128 + lax.broadcasted_iota(jnp.int32, (WIN_BLOCKS * 128, 1), 0))
        return _attend_t(kw, vwt, qt4, jnp.where((diff >= 0) & (diff < WINDOW), 0.0, NEG), tq)

    n_var = -(-t // SLC_KEY_STEP)
    for c in range(n_var):
        nk = min((c + 1) * SLC_KEY_STEP, t)

        @pl.when((ti * tq) // SLC_KEY_STEP == c)
        def _():
            on = jnp.dot(ex_ref[0:nk], selt, preferred_element_type=jnp.float32)
            kpos = lax.broadcasted_iota(jnp.int32, (nk, 1), 0)
            bias = (on - 1.0) * (-NEG) + jnp.where(kpos <= qpos, 0.0, NEG)
            o_slc = _attend_t(ks_ref[0, 0, 0:nk], vs_ref[0, 0, :, 0:nk], qt4, bias, tq)
            o = o_slc * _gate_row(gt_ref, 1, kvh, tq) + window() * _gate_row(gt_ref, 2, kvh, tq)
            o_ref[0] = (_heads_to_rows(o, tq) + oc_ref[0]).astype(jnp.bfloat16)


def slc_win_prompt(proj3, ks, vst, kw, vwt, selt, o_cmp, *, tq):
    b, t, _ = proj3.shape
    nkb = t // 128
    expand = jnp.asarray((np.arange(t)[:, None] // L_SLC == np.arange(128)[None, :]).astype(np.float32),
                         jnp.bfloat16)
    i4 = lambda bi, hi, ti: (bi, hi, 0, 0)
    i5 = lambda bi, hi, ti: (bi, hi, 0, 0, 0)
    head_cols = pl.BlockSpec((1, tq, 256), lambda bi, hi, ti: (bi, ti, hi))
    return pl.pallas_call(
        functools.partial(_slc_win_kernel, tq=tq),
        grid=(b, N_KV, t // tq),
        in_specs=[head_cols,
                  pl.BlockSpec((1, 1, t, HEAD_DIM), i4),
                  pl.BlockSpec((1, 1, HEAD_DIM, t), i4),
                  pl.BlockSpec((1, 1, nkb, 128, HEAD_DIM), i5),
                  pl.BlockSpec((1, 1, nkb, HEAD_DIM, 128), i5),
                  pl.BlockSpec((1, 1, 128, tq), lambda bi, hi, ti: (bi, hi, 0, ti)),
                  pl.BlockSpec((t, 128), lambda bi, hi, ti: (0, 0)),
                  pl.BlockSpec((1, tq, 256), lambda bi, hi, ti: (bi, ti, C_GATE // 256)),
                  head_cols],
        out_specs=head_cols,
        out_shape=jax.ShapeDtypeStruct((b, t, 1024), jnp.bfloat16),
        compiler_params=_cp(("parallel", "parallel", "arbitrary")),
        name="slc_win_prompt",
    )(proj3, ks, vst, kw, vwt, selt, expand, proj3, o_cmp)


def _kv_heads(kv_rows, b, t):
    kv = kv_rows.reshape(b, t, 2, N_KV, HEAD_DIM).astype(jnp.bfloat16)
    return kv[:, :, 0].transpose(0, 2, 1, 3), kv[:, :, 1].transpose(0, 2, 3, 1)


def _kv_blocks(kv_rows, b, t):
    k, vt = _kv_heads(kv_rows, b, t)
    return (k.reshape(b, N_KV, t // 128, 128, HEAD_DIM),
            vt.reshape(b, N_KV, HEAD_DIM, t // 128, 128).transpose(0, 1, 3, 2, 4))


def _out_proj_kernel(ro_ref, rg_ref, nsa_ref, x_ref, wo_ref, y_ref, a_ref):
    @pl.when(pl.program_id(1) == 0)
    def _():
        a_ref[:, 0:D_RWKV] = (ro_ref[...] * rg_ref[...]).astype(jnp.bfloat16)
        a_ref[:, D_RWKV:D_MODEL] = nsa_ref[...]

    y_ref[...] = x_ref[...] + jnp.dot(a_ref[...], wo_ref[...], preferred_element_type=jnp.float32)


def out_proj(ro, rg, nsa_bf16, x, wo_bf16, *, tm, tn):
    m = x.shape[0]
    row = lambda w: pl.BlockSpec((tm, w), lambda i, j: (i, 0))
    return pl.pallas_call(
        _out_proj_kernel,
        grid=(m // tm, D_MODEL // tn),
        in_specs=[row(D_RWKV), row(D_RWKV), row(D_NSA),
                  pl.BlockSpec((tm, tn), lambda i, j: (i, j)),
                  pl.BlockSpec((D_MODEL, tn), lambda i, j: (0, j))],
        out_specs=pl.BlockSpec((tm, tn), lambda i, j: (i, j)),
        out_shape=jax.ShapeDtypeStruct((m, D_MODEL), jnp.float32),
        scratch_shapes=[pltpu.VMEM((tm, D_MODEL), jnp.bfloat16)],
        compiler_params=_cp(("parallel", "arbitrary")),
        name="out_proj",
    )(ro, rg, nsa_bf16, x, wo_bf16)


def _nsa_combine_kernel(oc_ref, os_ref, ow_ref, gt_ref, o_ref):
    g = _sigmoid(gt_ref[...])
    o_ref[...] = (g[:, 0] * oc_ref[...] + g[:, 1] * os_ref[...] + g[:, 2] * ow_ref[...]).astype(jnp.bfloat16)


def nsa_combine_sample(o_cmp, o_slc, o_win, gate_logits):
    return pl.pallas_call(
        _nsa_combine_kernel,
        out_shape=jax.ShapeDtypeStruct(o_cmp.shape, jnp.bfloat16),
        name="nsa_combine_sample",
    )(o_cmp, o_slc, o_win, gate_logits)


def _ffn_kernel(x_ref, gf_ref, wg_ref, wu_ref, wd_ref, gl_ref, y_ref, h_ref, acc_ref):
    f = pl.program_id(1)

    @pl.when(f == 0)
    def _():
        x = x_ref[...]
        ms = jnp.mean(x * x, axis=-1, keepdims=True)
        h_ref[...] = (x * lax.rsqrt(ms + RMS_EPS) * gf_ref[...]).astype(jnp.bfloat16)
        acc_ref[...] = jnp.zeros_like(acc_ref)

    h = h_ref[...]
    gate = jnp.dot(h, wg_ref[...], preferred_element_type=jnp.float32)
    up = jnp.dot(h, wu_ref[...], preferred_element_type=jnp.float32)
    act = (_silu(gate) * up).astype(jnp.bfloat16)
    acc_ref[...] += jnp.dot(act, wd_ref[...], preferred_element_type=jnp.float32)

    @pl.when(f == pl.num_programs(1) - 1)
    def _():
        z = x_ref[...] + acc_ref[...]
        ms = jnp.mean(z * z, axis=-1, keepdims=True)
        y_ref[...] = z * lax.rsqrt(ms + RMS_EPS) * gl_ref[...]


def ffn_final(x, g_ffn, wg, wu, wd, g_final, *, tm, tf):
    m = x.shape[0]
    dff = wg.shape[1]
    return pl.pallas_call(
        _ffn_kernel,
        grid=(m // tm, dff // tf),
        in_specs=[pl.BlockSpec((tm, D_MODEL), lambda i, f: (i, 0)),
                  pl.BlockSpec((1, D_MODEL), lambda i, f: (0, 0)),
                  pl.BlockSpec((D_MODEL, tf), lambda i, f: (0, f)),
                  pl.BlockSpec((D_MODEL, tf), lambda i, f: (0, f)),
                  pl.BlockSpec((tf, D_MODEL), lambda i, f: (f, 0)),
                  pl.BlockSpec((1, D_MODEL), lambda i, f: (0, 0))],
        out_specs=pl.BlockSpec((tm, D_MODEL), lambda i, f: (i, 0)),
        out_shape=jax.ShapeDtypeStruct((m, D_MODEL), jnp.float32),
        scratch_shapes=[pltpu.VMEM((tm, D_MODEL), jnp.bfloat16), pltpu.VMEM((tm, D_MODEL), jnp.float32)],
        compiler_params=_cp(("parallel", "arbitrary")),
        name="ffn_final",
    )(x, g_ffn.reshape(1, -1), wg, wu, wd, g_final.reshape(1, -1))


def _cmp_part_paged_kernel(pt_ref, *refs, n_in):
    x_refs = refs[:n_in]
    perm_ref, w_ref, o_ref, xs_ref = refs[n_in:]
    cpp = x_refs[0].shape[-1] // D_CMP
    n_pair = 2 * (N_KV // 2)
    for p in range(n_in):
        x = x_refs[p][0].reshape(2 * N_KV * HEAD_DIM, x_refs[p].shape[-1])
        xp_all = jnp.dot(x.astype(jnp.bfloat16), perm_ref[...], preferred_element_type=jnp.float32)
        for ep in range(n_pair):
            xp = xp_all[ep * 128:(ep + 1) * 128].T
            for j in range(D_CMP):
                xs_ref[ep, j, p * cpp:(p + 1) * cpp, :] = xp[j * cpp:(j + 1) * cpp]
    for e in range(2):
        for hp in range(2):
            acc = None
            for j in range(0, D_CMP, 2):
                x2 = jnp.concatenate([xs_ref[e * 2 + hp, j], xs_ref[e * 2 + hp, j + 1]], axis=1)
                w2 = jnp.concatenate([w_ref[e, j], w_ref[e, j + 1]], axis=0)
                d = _bdot(x2, w2)
                acc = d if acc is None else acc + d
            c0 = e * 256 + hp * 128
            o_ref[0, :, c0:c0 + 128] = acc[:, :128]
            o_ref[0, :, 512 + c0:512 + c0 + 128] = acc[:, 128:]


def cmp_part_sample(pool_t, page_table, w1x_bf16, *, pp):
    b, n_pages = page_table.shape
    page = pool_t.shape[-1]
    cpp = page // D_CMP

    def xspec(k):
        return pl.BlockSpec((1, 2, N_KV, HEAD_DIM, page), lambda bi, gi, pt: (pt[bi, gi * pp + k], 0, 0, 0, 0))

    r = np.arange(page)
    perm = np.zeros((page, page), np.float32)
    perm[r, (r % D_CMP) * cpp + r // D_CMP] = 1.0
    grid_spec = pltpu.PrefetchScalarGridSpec(
        num_scalar_prefetch=1,
        grid=(b, n_pages // pp),
        in_specs=[xspec(k) for k in range(pp)]
        + [pl.BlockSpec((page, page), lambda bi, gi, pt: (0, 0)),
           pl.BlockSpec((2, D_CMP, 128, 256), lambda bi, gi, pt: (0, 0, 0, 0))],
        out_specs=pl.BlockSpec((1, cpp * pp, 1024), lambda bi, gi, pt: (bi, gi, 0)),
        scratch_shapes=[pltpu.VMEM((N_KV, D_CMP, pp * cpp, 128), jnp.float32)],
    )
    return pl.pallas_call(
        functools.partial(_cmp_part_paged_kernel, n_in=pp),
        grid_spec=grid_spec,
        out_shape=jax.ShapeDtypeStruct((b, n_pages * cpp, 1024), jnp.float32),
        compiler_params=_cp(("parallel", "arbitrary")),
        name="cmp_part_sample",
    )(page_table, *([pool_t] * pp), jnp.asarray(perm, jnp.bfloat16), w1x_bf16)


def _fold_heads(o_ext):
    row_kvh = lax.broadcasted_iota(jnp.int32, (N_HEADS, 1), 0) // GQA
    out = jnp.zeros((N_HEADS, HEAD_DIM), jnp.float32)
    for kvh in range(N_KV):
        out = out + jnp.where(row_kvh == kvh, o_ext[:, kvh * HEAD_DIM:(kvh + 1) * HEAD_DIM], 0.0)
    return out


def _cmp_attn_sample_kernel(part_ref, pe_ref, w2bd_ref, q_ref, g8_ref, ov_ref, o_ref, imps_ref, *, q_pos):
    part = part_ref[0]
    n = part.shape[0]
    kc = _split2(_dot3(_split2(_cmp_hidden(part, pe_ref, 0)), _split2(w2bd_ref[0])))
    vc = _bdot(_cmp_hidden(part, pe_ref, 1), w2bd_ref[1])
    qh, ql = _split2(q_ref[0])
    s = (_dot_nt(qh, kc[0]) + (_dot_nt(qh, kc[1]) + _dot_nt(ql, kc[0]))) * SCALE
    cidx = lax.broadcasted_iota(jnp.int32, (1, n), 1)
    cmask = (cidx * D_CMP + (L_CMP - 1)) <= q_pos
    s = jnp.where(cmask, s, NEG)
    m = jnp.max(s, axis=-1, keepdims=True)
    ex = jnp.where(cmask, jnp.exp(s - m), 0.0)
    p = ex / jnp.maximum(jnp.sum(ex, axis=-1, keepdims=True), 1e-30)
    o_ref[0] = _fold_heads(_bdot(p, vc))
    ih, il = _split2(_hdot(g8_ref[...], p))
    ov = ov_ref[...]
    imps_ref[0] = (jnp.dot(ih, ov, preferred_element_type=jnp.float32)
                   + jnp.dot(il, ov, preferred_element_type=jnp.float32))


def cmp_attn_sample(part, pe, w2bd, q16ext, g8, ov_bf16, *, q_pos):
    b, n, _ = part.shape
    nbp = ov_bf16.shape[1]
    c3 = lambda bi: (0, 0, 0)
    return pl.pallas_call(
        functools.partial(_cmp_attn_sample_kernel, q_pos=q_pos),
        grid=(b,),
        in_specs=[pl.BlockSpec((1, n, 1024), lambda bi: (bi, 0, 0)),
                  pl.BlockSpec((2, 8, 256), c3),
                  pl.BlockSpec((2, 256, 256), c3),
                  pl.BlockSpec((1, N_HEADS, 256), lambda bi: (bi, 0, 0)),
                  pl.BlockSpec((8, N_HEADS), lambda bi: (0, 0)),
                  pl.BlockSpec((n, nbp), lambda bi: (0, 0))],
        out_specs=[pl.BlockSpec((1, N_HEADS, HEAD_DIM), lambda bi: (bi, 0, 0)),
                   pl.BlockSpec((1, 8, nbp), lambda bi: (bi, 0, 0))],
        out_shape=[jax.ShapeDtypeStruct((b, N_HEADS, HEAD_DIM), jnp.float32),
                   jax.ShapeDtypeStruct((b, 8, nbp), jnp.float32)],
        compiler_params=_cp(("parallel",)),
        name="cmp_attn_sample",
    )(part, pe, w2bd, q16ext, g8, ov_bf16)


def _topk_kernel(sc_ref, idx_ref, *, n_blk, q_blk, k_sel):
    imps = sc_ref[...]
    nbp = imps.shape[0]
    sidx = lax.broadcasted_iota(jnp.int32, (nbp, 1), 0)
    sf = sidx.astype(jnp.float32)
    forced = (sidx == 0) | (sidx == q_blk) | (sidx == q_blk - 1)
    causal = (sidx <= q_blk) & (sidx < n_blk)
    score = jnp.where(forced, -NEG, jnp.where(causal, imps, NEG))
    rows = []
    for _ in range(k_sel):
        m = jnp.max(score, axis=0, keepdims=True)
        pick = jnp.min(jnp.where(score == m, sf, float(nbp)), axis=0, keepdims=True)
        rows.append(jnp.where(m > 0.5 * NEG, pick, -1.0))
        score = jnp.where(sf == pick, 2.0 * NEG, score)
    idx_ref[...] = jnp.concatenate(rows, axis=0).astype(jnp.int32)


def topk_blocks(scores, *, n_blk, q_blk, k_sel):
    nbp, lanes = scores.shape
    return pl.pallas_call(
        functools.partial(_topk_kernel, n_blk=n_blk, q_blk=q_blk, k_sel=k_sel),
        out_shape=jax.ShapeDtypeStruct((k_sel, lanes), jnp.int32),
        name="topk_blocks",
    )(scores)


def _slc_sample_kernel(idx_ref, page_ref, *refs, n_sel, new_blk):
    blk_refs = refs[:N_KV * n_sel]
    q_ref, knew_ref, vnew_ref, o_ref = refs[N_KV * n_sel:]
    b = pl.program_id(0)
    page = blk_refs[0].shape[-1]
    half_of_lane = lax.broadcasted_iota(jnp.int32, (1, page), 1) // L_SLC
    for kvh in range(N_KV):
        kts, vts, biases = [], [], []
        has_new = False
        for n in range(n_sel):
            blk = blk_refs[kvh * n_sel + n]
            kts.append(blk[0, 0, 0])
            vts.append(blk[0, 1, 0])
            s_n = idx_ref[(b * N_KV + kvh) * n_sel + n]
            live = (half_of_lane == s_n % (page // L_SLC)) & ((s_n >= 0) & (s_n != new_blk))
            biases.append(jnp.where(live, 0.0, NEG))
            has_new = jnp.logical_or(has_new, s_n == new_blk)
        kt = jnp.concatenate(kts, axis=1).astype(jnp.bfloat16)
        vt = jnp.concatenate(vts, axis=1).astype(jnp.bfloat16)
        q = q_ref[0, kvh]
        s = (jnp.dot(q.astype(jnp.bfloat16), kt, preferred_element_type=jnp.float32) * SCALE
             + jnp.concatenate(biases, axis=1))
        s_new = jnp.where(has_new, jnp.sum(q * knew_ref[0, kvh], axis=-1, keepdims=True) * SCALE, NEG)
        m = jnp.maximum(jnp.max(s, axis=-1, keepdims=True), s_new)
        ex = jnp.exp(s - m)
        ex_new = jnp.where(has_new, jnp.exp(s_new - m), 0.0)
        denom = jnp.maximum(jnp.sum(ex, axis=-1, keepdims=True) + ex_new, 1e-30)
        o_ref[0, kvh] = (_dot_nt(ex.astype(jnp.bfloat16), vt) + ex_new * vnew_ref[0, kvh]) / denom


def slc_sample(idx_flat, page_table, pool_t, q4, knew, vnew, *, n_sel, new_blk):
    b, n_pages = page_table.shape
    page = pool_t.shape[-1]
    bpp = page // L_SLC
    pick_page = jnp.clip(idx_flat.reshape(b, N_KV * n_sel), 0, bpp * n_pages - 1) // bpp
    page_flat = jnp.take_along_axis(page_table, pick_page, axis=1).reshape(-1)

    def bspec(kvh, n):
        return pl.BlockSpec((1, 2, 1, HEAD_DIM, page),
                            lambda bi, idx, pg: (pg[(bi * N_KV + kvh) * n_sel + n], 0, kvh, 0, 0))

    small = lambda r: pl.BlockSpec((1, N_KV, r, HEAD_DIM), lambda bi, idx, pt: (bi, 0, 0, 0))
    grid_spec = pltpu.PrefetchScalarGridSpec(
        num_scalar_prefetch=2,
        grid=(b,),
        in_specs=[bspec(kvh, n) for kvh in range(N_KV) for n in range(n_sel)] + [small(8), small(1), small(1)],
        out_specs=small(8),
    )
    return pl.pallas_call(
        functools.partial(_slc_sample_kernel, n_sel=n_sel, new_blk=new_blk),
        grid_spec=grid_spec,
        out_shape=jax.ShapeDtypeStruct((b, N_KV, 8, HEAD_DIM), jnp.float32),
        compiler_params=_cp(("arbitrary",)),
        name="slc_sample",
    )(idx_flat, page_flat, *([pool_t] * (N_KV * n_sel)), q4, knew, vnew)


def _win_sample_kernel(w_ref, q_ref, o_ref):
    w = w_ref[0]
    s = _dot_nt(q_ref[0].astype(jnp.bfloat16), w[:, :KV_W].astype(jnp.bfloat16)) * SCALE
    m = jnp.max(s, axis=-1, keepdims=True)
    ex = jnp.exp(s - m)
    p = ex / jnp.maximum(jnp.sum(ex, axis=-1, keepdims=True), 1e-30)
    o_ref[0] = _fold_heads(_bdot(p, w[:, KV_W:]))


def win_sample(win_rows, q16ext):
    b, nk, _ = win_rows.shape
    return pl.pallas_call(
        _win_sample_kernel,
        grid=(b,),
        in_specs=[pl.BlockSpec((1, nk, 2 * KV_W), lambda bi: (bi, 0, 0)),
                  pl.BlockSpec((1, N_HEADS, KV_W), lambda bi: (bi, 0, 0))],
        out_specs=pl.BlockSpec((1, N_HEADS, HEAD_DIM), lambda bi: (bi, 0, 0)),
        out_shape=jax.ShapeDtypeStruct((b, N_HEADS, HEAD_DIM), jnp.float32),
        compiler_params=_cp(("parallel",)),
        name="win_sample",
    )(win_rows, q16ext)


def _prep_weights(w_in, mu, w2, a2, g2):
    o3 = 3 * D_RWKV
    nsa0 = R_COLS
    cols = [w_in[:, nsa0:nsa0 + D_NSA],
            w_in[:, 0:o3],
            w_in[:, nsa0 + D_NSA:nsa0 + D_NSA + 6 * KV_W],
            w_in[:, o3:R_COLS],
            w_in[:, nsa0 + D_NSA + 6 * KV_W:],
            jnp.zeros((D_MODEL, D_IN_PAD - C_GATE - 3 * N_HEADS), w_in.dtype)]
    w_in_p = jnp.concatenate(cols, axis=1).astype(jnp.bfloat16)
    z = jnp.zeros((LORA_ALL, D_RWKV), jnp.float32)
    return dict(w_in_p=w_in_p, mu_p=mu,
                w2p=z.at[0:LORA_W].set(w2), a2p=z.at[LORA_W:LORA_W + LORA_A].set(a2),
                g2p=z.at[LORA_W + LORA_A:].set(g2))


def _layer_weights(layer, g_mix, w_in, mu, w0, w2, a0, a2, g2, k_k, k_a, r_k, lnx_w, lnx_b, phi_pe, phi_w1,
                   phi_w2, w_o, g_ffn, w_gate, w_up, w_down):
    w = _prep_weights(w_in[layer], mu[layer], w2[layer], a2[layer], g2[layer])
    w1t, pe8, w2k, w2vt, w2bd = _phi_weights(phi_pe[layer], phi_w1[layer], phi_w2[layer])
    w1x = _expand_w1(phi_w1[layer])
    w.update(g_mix=g_mix[layer], w0=w0[layer], a0=a0[layer], k_k=k_k[layer], k_a=k_a[layer], r_k=r_k[layer],
             lnx_w=lnx_w[layer], lnx_b=lnx_b[layer], w1x=w1x, w1x_bf16=w1x.astype(jnp.bfloat16),
             pe=pe_term(pe8, w1t), w2k=w2k, w2vt=w2vt, w2bd=w2bd,
             wo=w_o[layer].astype(jnp.bfloat16), g_ffn=g_ffn[layer],
             wg=w_gate[layer].astype(jnp.bfloat16), wu=w_up[layer].astype(jnp.bfloat16),
             wd=w_down[layer].astype(jnp.bfloat16))
    return w


def _pick(n, pref):
    while n % pref:
        pref //= 2
    return pref


def _prompt_layer(x_prompt, W, g_final):
    B, T, _ = x_prompt.shape
    M = B * T
    x2 = x_prompt.reshape(M, D_MODEL)
    proj = proj_matmul(x2, W["g_mix"], W["w_in_p"], apply_norm=True, tm=_pick(M, 1024), tn=1024)
    proj3 = proj.reshape(B, T, D_IN_PAD)
    ro, rg, wkv = rwkv_mix(proj, jnp.zeros((B, 1, R_COLS), jnp.float32), W,
                           jnp.zeros((B, H_R, 64, 64), jnp.float32), b=B, t=T, tm=_pick(T, 256), tc=_pick(T, 64))
    cmp_kv = proj3[..., C_CMP:C_CMP + 2 * KV_W]
    slc_kv = proj3[..., C_SLC:C_SLC + 2 * KV_W]
    win_kv = proj3[..., C_WIN:C_WIN + 2 * KV_W]
    nch = T // D_CMP
    part = cmp_part_prompt(cmp_kv.reshape(B, nch, D_CMP * 2 * KV_W), W["w1x"])
    ovt = _overlap_t(128, nch - 1, nch).astype(jnp.bfloat16)
    o_cmp, sel = cmp_select_prompt(part, W["pe"], W["w2k"], W["w2vt"], proj3, ovt, tq=_pick(T, 256))
    ks, vs = _kv_heads(slc_kv, B, T)
    kw, vw = _kv_blocks(win_kv, B, T)
    nsa = slc_win_prompt(proj3, ks, vs, kw, vw, sel, o_cmp, tq=128)
    r2 = lambda z: z.reshape(M, 1024)
    x1 = out_proj(r2(ro), r2(rg), r2(nsa), x2, W["wo"], tm=_pick(M, 512), tn=1024)
    y = ffn_final(x1, W["g_ffn"], W["wg"], W["wu"], W["wd"], g_final, tm=_pick(M, 512), tf=512)
    kv5 = lambda z: z.reshape(B, T, 2, N_KV, HEAD_DIM)
    nwin = min(WINDOW, T)
    return y.reshape(B, T, D_MODEL), kv5(cmp_kv), kv5(slc_kv), kv5(win_kv)[:, T - nwin:], wkv


def _sample_layer(x_sample, xn_s, cache_cmp, cache_slc, cache_win, wkv0, shift, page_table, W, g_final):
    b = x_sample.shape[0]
    n_pool, page = cache_cmp.shape[0], cache_cmp.shape[1]
    n_pages = page_table.shape[1]
    past_len = n_pages * page
    n_rows = -(-(past_len + 1) // L_SLC) * L_SLC
    n_blk = n_rows // L_SLC
    q_blk = past_len // L_SLC
    nbp = -(-n_blk // 128) * 128

    rows = jnp.concatenate([xn_s, shift], axis=0)
    proj = proj_matmul(rows, W["g_mix"], W["w_in_p"], apply_norm=False, tm=2 * b, tn=1024)
    prev = jnp.concatenate([proj[b:, C_R:C_R + 3 * D_RWKV], proj[b:, C_LORA:C_LORA + LORA_ALL]], axis=1)
    ro, rg, wkv = rwkv_mix(proj[:b], prev, W, wkv0, b=b, t=1, tm=1, tc=1)

    q = proj[:b, C_Q:C_Q + D_NSA].reshape(b, N_KV, GQA, 1, HEAD_DIM)
    eye = jnp.eye(N_KV, dtype=jnp.float32).reshape(1, N_KV, 1, N_KV, 1)
    qext = (q * eye).reshape(b, N_KV, GQA, KV_W)
    q16ext = qext.reshape(b, N_HEADS, KV_W)
    cmp_new = proj[:b, C_CMP:C_CMP + 2 * KV_W]
    slc_new = proj[:b, C_SLC:C_SLC + 2 * KV_W]
    win_new = proj[:b, C_WIN:C_WIN + 2 * KV_W]

    part = cmp_part_sample(cache_cmp.transpose(0, 2, 3, 4, 1), page_table, W["w1x_bf16"], pp=_pick(n_pages, 32))
    nc_rows = part.shape[1]
    ov = _overlap_t(nbp, n_rows // D_CMP - L_CMP // D_CMP + 1, nc_rows)[:n_blk].T
    ov = jnp.pad(ov, ((0, 0), (0, nbp - n_blk)))
    g8 = jnp.asarray((np.arange(8)[:, None] == np.arange(N_HEADS)[None, :] // GQA).astype(np.float32))
    o_cmp, imps = cmp_attn_sample(part, W["pe"], W["w2bd"], q16ext, g8, ov.astype(jnp.bfloat16), q_pos=past_len)
    scores = imps[:, :N_KV].transpose(2, 0, 1).reshape(nbp, b * N_KV)
    k_sel = min(N_SELECT, n_blk)
    idx = topk_blocks(scores, n_blk=n_blk, q_blk=q_blk, k_sel=k_sel)
    q4 = jnp.pad(q.reshape(b, N_KV, GQA, HEAD_DIM), ((0, 0), (0, 0), (0, 8 - GQA), (0, 0)))
    slc_new5 = slc_new.reshape(b, 2, N_KV, 1, HEAD_DIM)
    o_slc = slc_sample(idx.T.reshape(-1), page_table, cache_slc.transpose(0, 2, 3, 4, 1), q4,
                       slc_new5[:, 0], slc_new5[:, 1], n_sel=k_sel, new_blk=q_blk)
    o_slc = o_slc[:, :, :GQA].reshape(b, D_NSA)

    n_buf = cache_win.shape[1]
    win_all = jnp.concatenate([cache_win, win_new.reshape(b, 1, 2, N_KV, HEAD_DIM)], axis=1)
    win_keep = win_all[:, n_buf + 1 - min(WINDOW, n_buf + 1):]
    o_win = win_sample(win_keep.reshape(b, -1, 2 * KV_W), q16ext)

    x2 = x_sample.reshape(b, D_MODEL)
    gate_logits = proj[:b, C_GATE:C_GATE + 3 * N_HEADS].reshape(b, 3, N_HEADS, 1)
    nsa = nsa_combine_sample(o_cmp, o_slc.reshape(b, N_HEADS, HEAD_DIM), o_win, gate_logits)
    x1 = out_proj(ro.reshape(b, -1), rg.reshape(b, -1), nsa.reshape(b, D_NSA), x2, W["wo"], tm=b, tn=1024)
    y = ffn_final(x1, W["g_ffn"], W["wg"], W["wu"], W["wd"], g_final, tm=b, tf=512)
    kv5 = lambda z: z.reshape(b, 1, 2, N_KV, HEAD_DIM)
    return y.reshape(b, 1, D_MODEL), kv5(cmp_new), kv5(slc_new), win_keep, wkv


def kernel(x_prompt, x_sample, cache_cmp_kv, cache_slc_kv, cache_win_kv, state_wkv, state_shift, page_table, g_mix, w_in, mu, w0, w2, a0, a2, g2, k_k, k_a, r_k, lnx_w, lnx_b, phi_pe, phi_w1, phi_w2, w_o, g_ffn, w_gate, w_up, w_down, g_final):
    W = _layer_weights(0, g_mix, w_in, mu, w0, w2, a0, a2, g2, k_k, k_a, r_k, lnx_w, lnx_b, phi_pe, phi_w1,
                       phi_w2, w_o, g_ffn, w_gate, w_up, w_down)
    bp, bs = x_prompt.shape[0], x_sample.shape[0]
    small = jnp.concatenate([x_sample.reshape(bs, D_MODEL), x_prompt[:, -1]], axis=0)
    pad = (-small.shape[0]) % 8
    xn_small = rmsnorm_rows(jnp.pad(small, ((0, pad), (0, 0))), W["g_mix"])
    xn_s, shift_p = xn_small[:bs], xn_small[bs:bs + bp]

    y_p, cmp_p, slc_p, win_p, wkv_p = _prompt_layer(x_prompt, W, g_final)
    y_s, cmp_s, slc_s, win_s, wkv_s = _sample_layer(x_sample, xn_s, cache_cmp_kv[0], cache_slc_kv[0],
                                                    cache_win_kv[0], state_wkv[0], state_shift[0], page_table,
                                                    W, g_final)
    return (y_p, y_s, cmp_p[None], slc_p[None], win_p[None], wkv_p[None], shift_p[None],
            cmp_s[None], slc_s[None], win_s[None], wkv_s[None], xn_s[None])
```
